```python
import math
import jax, jax.numpy as jnp
from jax import lax
import numpy as np

D_MODEL = 2048
BATCH = 16
SEQ = 256
DEPTH = 1
DEC_BATCH = 2
DEC_SEQ = 4096
PAST_LEN = 512

GRID_W = 64
HEAD_DIM = 128
N_DIFF_HEADS = 8
DIFF_QK_DIM = HEAD_DIM // 2
N_GQA_HEADS = 8
N_GQA_KV_HEADS = 2
GQA_GROUP = N_GQA_HEADS // N_GQA_KV_HEADS
MIX_WIDTH = (N_DIFF_HEADS + N_GQA_HEADS) * HEAD_DIM
DIFF_QK_W = N_DIFF_HEADS * 2 * DIFF_QK_DIM
DIFF_V_W = N_DIFF_HEADS * HEAD_DIM
GQA_Q_W = N_GQA_HEADS * HEAD_DIM
GQA_KV_W = N_GQA_KV_HEADS * HEAD_DIM
IN_PROJ_W = 2 * DIFF_QK_W + DIFF_V_W + GQA_Q_W + 2 * GQA_KV_W
N_EXPERTS = 16
CAPACITY_FACTOR = 2
D_EXPERT = D_MODEL // 2
Q_BLOCK = 128
ROPE_THETA = 10000.0
NORM_EPS = 1e-6
N_MOD = 6

kernel_name = "hybrid_diff_gqa_ec_diffusion_step"


def lambda_init(layer_idx):
    return 0.8 - 0.6 * math.exp(-0.3 * layer_idx)


def rmsnorm(x, g):
    xf = x.astype(jnp.float32)
    y = xf * lax.rsqrt(jnp.mean(xf * xf, axis=-1, keepdims=True) + NORM_EPS)
    return (y * g.astype(jnp.float32)).astype(x.dtype)


def axial_rope_tables(n, dim, dtype):
    t = jnp.arange(n, dtype=jnp.int32)
    row = (t // GRID_W).astype(jnp.float32)
    col = (t % GRID_W).astype(jnp.float32)
    sec = dim // 2
    inv = ROPE_THETA ** (-jnp.arange(0, sec, 2, dtype=jnp.float32) / sec)
    ang = jnp.stack([row[:, None] * inv, col[:, None] * inv], axis=1)
    ang = jnp.concatenate([ang, ang], axis=-1).reshape(n, dim)
    return jnp.cos(ang).astype(dtype), jnp.sin(ang).astype(dtype)


def apply_rope(x, cos, sin):
    shp = x.shape
    dim = shp[-1]
    xs = x.reshape(shp[:-1] + (2, 2, dim // 4))
    rot = jnp.concatenate([-xs[..., 1:, :], xs[..., :1, :]], axis=-2).reshape(shp)
    bshape = (1, shp[1]) + (1,) * (x.ndim - 3) + (dim,)
    return x * cos.reshape(bshape) + rot * sin.reshape(bshape)


def _to_blocks(q):
    b, n = q.shape[:2]
    q = q.reshape((b, n // Q_BLOCK, Q_BLOCK) + q.shape[2:])
    return jnp.moveaxis(q, 1, 0)


def _from_blocks(o):
    o = jnp.moveaxis(o, 0, 1)
    return o.reshape((o.shape[0], -1) + o.shape[3:])


def diff_attention(q1, q2, k1, k2, v, lam):
    scale = DIFF_QK_DIM ** -0.5

    def block(qs):
        qb1, qb2 = qs
        a1 = jax.nn.softmax(jnp.einsum('bqhd,bkhd->bhqk', qb1, k1).astype(jnp.float32) * scale, axis=-1)
        a2 = jax.nn.softmax(jnp.einsum('bqhd,bkhd->bhqk', qb2, k2).astype(jnp.float32) * scale, axis=-1)
        a = (a1 - lam * a2).astype(v.dtype)
        return jnp.einsum('bhqk,bkhd->bqhd', a, v)

    return _from_blocks(lax.map(block, (_to_blocks(q1), _to_blocks(q2))))


def gqa_attention(q, k, v):
    b, n = q.shape[:2]
    scale = HEAD_DIM ** -0.5
    q = q.reshape(b, n, N_GQA_KV_HEADS, GQA_GROUP, HEAD_DIM)

    def block(qb):
        s = jnp.einsum('bqgrd,bkgd->bgrqk', qb, k).astype(jnp.float32) * scale
        p = jax.nn.softmax(s, axis=-1).astype(v.dtype)
        o = jnp.einsum('bgrqk,bkgd->bqgrd', p, v)
        return o.reshape(o.shape[0], o.shape[1], N_GQA_HEADS, HEAD_DIM)

    return _from_blocks(lax.map(block, _to_blocks(q)))


def ec_moe(h, w_router, w_gate, w_up, w_down):
    b, n, d = h.shape
    cap = CAPACITY_FACTOR * n // N_EXPERTS
    logits = jnp.einsum('bnd,de->bne', h, w_router).astype(jnp.float32)
    aff = jax.nn.softmax(logits, axis=-1)
    gate, idx = lax.top_k(jnp.swapaxes(aff, 1, 2), cap)
    xs = jax.vmap(lambda hb, ib: hb[ib])(h, idx)
    hid = jax.nn.silu(jnp.einsum('becd,edf->becf', xs, w_gate)) * jnp.einsum('becd,edf->becf', xs, w_up)
    out = jnp.einsum('becf,efd->becd', hid, w_down) * gate[..., None].astype(h.dtype)
    return jax.vmap(lambda ob, ib: jnp.zeros((n, d), h.dtype).at[ib.reshape(-1)].add(ob.reshape(-1, d)))(out, idx)


def trunk_layer(x, mod, rope, ctx_kv, lam_init, norm1_g, norm2_g, w_in, w_out, lq1, lk1, lq2, lk2,
                subln_g, qn_g, kn_g, w_router, w_gate, w_up, w_down):
    b, n, _ = x.shape
    sh1, sc1, g1, sh2, sc2, g2 = jnp.split(mod[:, None, :].astype(x.dtype), N_MOD, axis=-1)
    h = rmsnorm(x, norm1_g) * (1 + sc1) + sh1
    p = jnp.einsum('bnd,de->bne', h, w_in)
    splits = [DIFF_QK_W, 2 * DIFF_QK_W, 2 * DIFF_QK_W + DIFF_V_W, 2 * DIFF_QK_W + DIFF_V_W + GQA_Q_W,
              2 * DIFF_QK_W + DIFF_V_W + GQA_Q_W + GQA_KV_W]
    dq, dk, dv, gq, gk, gv = jnp.split(p, splits, axis=-1)
    dq = dq.reshape(b, n, N_DIFF_HEADS, 2, DIFF_QK_DIM)
    dk = dk.reshape(b, n, N_DIFF_HEADS, 2, DIFF_QK_DIM)
    dv = dv.reshape(b, n, N_DIFF_HEADS, HEAD_DIM)
    gq = rmsnorm(gq.reshape(b, n, N_GQA_HEADS, HEAD_DIM), qn_g)
    gk = rmsnorm(gk.reshape(b, n, N_GQA_KV_HEADS, HEAD_DIM), kn_g)
    gv = gv.reshape(b, n, N_GQA_KV_HEADS, HEAD_DIM)
    own_kv = (dk, dv, gk, gv)
    if rope is None:
        dk_all, dv_all, gk_all, gv_all = dk, dv, gk, gv
    else:
        dcos, dsin, gcos, gsin = rope
        dq = apply_rope(dq, dcos, dsin)
        gq = apply_rope(gq, gcos, gsin)
        ck_d, cv_d, ck_g, cv_g = ctx_kv
        dk_all = jnp.concatenate([ck_d.astype(x.dtype), apply_rope(dk, dcos, dsin)], axis=1)
        dv_all = jnp.concatenate([cv_d.astype(x.dtype), dv], axis=1)
        gk_all = jnp.concatenate([ck_g.astype(x.dtype), apply_rope(gk, gcos, gsin)], axis=1)
        gv_all = jnp.concatenate([cv_g.astype(x.dtype), gv], axis=1)
    lam = (jnp.exp(jnp.sum(lq1.astype(jnp.float32) * lk1.astype(jnp.float32)))
           - jnp.exp(jnp.sum(lq2.astype(jnp.float32) * lk2.astype(jnp.float32))) + lam_init)
    o_d = diff_attention(dq[..., 0, :], dq[..., 1, :], dk_all[..., 0, :], dk_all[..., 1, :], dv_all, lam)
    o_d = rmsnorm(o_d, subln_g) * (1.0 - lam_init)
    o_g = gqa_attention(gq, gk_all, gv_all)
    o = jnp.concatenate([o_d.reshape(b, n, DIFF_V_W), o_g.reshape(b, n, GQA_Q_W)], axis=-1)
    x = x + g1 * jnp.einsum('bne,ed->bnd', o, w_out)
    h2 = rmsnorm(x, norm2_g) * (1 + sc2) + sh2
    x = x + g2 * ec_moe(h2, w_router, w_gate, w_up, w_down)
    return x, own_kv


def setup_inputs(seed: int = 0) -> dict:
    key = jax.random.key(seed)
    ks = jax.random.split(key, 32)
    f32 = jnp.float32
    nrm = lambda k, shape, s: jax.random.normal(k, shape, f32) * s
    return {
        "x_prompt": nrm(ks[0], (BATCH, SEQ, D_MODEL), 1.0),
        "x_sample": nrm(ks[1], (DEC_BATCH, DEC_SEQ, D_MODEL), 1.0),
        "c": nrm(ks[2], (DEC_BATCH, D_MODEL), 1.0),
        "cache_diff_k": nrm(ks[3], (DEC_BATCH, DEPTH, PAST_LEN, N_DIFF_HEADS, 2, DIFF_QK_DIM), 1.0),
        "cache_diff_v": nrm(ks[4], (DEC_BATCH, DEPTH, PAST_LEN, N_DIFF_HEADS, HEAD_DIM), 1.0),
        "cache_gqa_k": nrm(ks[5], (DEC_BATCH, DEPTH, PAST_LEN, N_GQA_KV_HEADS, HEAD_DIM), 1.0),
        "cache_gqa_v": nrm(ks[6], (DEC_BATCH, DEPTH, PAST_LEN, N_GQA_KV_HEADS, HEAD_DIM), 1.0),
        "c_ctx": nrm(ks[7], (D_MODEL,), 1.0),
        "norm1_g": 1.0 + nrm(ks[8], (DEPTH, D_MODEL), 0.02),
        "norm2_g": 1.0 + nrm(ks[9], (DEPTH, D_MODEL), 0.02),
        "w_mod": nrm(ks[10], (DEPTH, D_MODEL, N_MOD * D_MODEL), 0.5 * D_MODEL ** -0.5),
        "b_mod": nrm(ks[11], (DEPTH, N_MOD * D_MODEL), 0.02),
        "w_in": nrm(ks[12], (DEPTH, D_MODEL, IN_PROJ_W), D_MODEL ** -0.5),
        "w_out": nrm(ks[13], (DEPTH, MIX_WIDTH, D_MODEL), MIX_WIDTH ** -0.5),
        "diff_lambda_q1": nrm(ks[14], (DEPTH, DIFF_QK_DIM), 0.1),
        "diff_lambda_k1": nrm(ks[15], (DEPTH, DIFF_QK_DIM), 0.1),
        "diff_lambda_q2": nrm(ks[16], (DEPTH, DIFF_QK_DIM), 0.1),
        "diff_lambda_k2": nrm(ks[17], (DEPTH, DIFF_QK_DIM), 0.1),
        "diff_subln_g": 1.0 + nrm(ks[18], (DEPTH, HEAD_DIM), 0.02),
        "gqa_q_norm_g": 1.0 + nrm(ks[19], (DEPTH, HEAD_DIM), 0.02),
        "gqa_k_norm_g": 1.0 + nrm(ks[20], (DEPTH, HEAD_DIM), 0.02),
        "w_router": nrm(ks[21], (DEPTH, D_MODEL, N_EXPERTS), D_MODEL ** -0.5),
        "w_expert_gate": nrm(ks[22], (DEPTH, N_EXPERTS, D_MODEL, D_EXPERT), D_MODEL ** -0.5),
        "w_expert_up": nrm(ks[23], (DEPTH, N_EXPERTS, D_MODEL, D_EXPERT), D_MODEL ** -0.5),
        "w_expert_down": nrm(ks[24], (DEPTH, N_EXPERTS, D_EXPERT, D_MODEL), D_EXPERT ** -0.5),
        "final_norm_g": 1.0 + nrm(ks[25], (D_MODEL,), 0.02),
    }


def reference(x_prompt, x_sample, c, cache_diff_k, cache_diff_v, cache_gqa_k, cache_gqa_v, c_ctx,
              norm1_g, norm2_g, w_mod, b_mod, w_in, w_out, diff_lambda_q1, diff_lambda_k1,
              diff_lambda_q2, diff_lambda_k2, diff_subln_g, gqa_q_norm_g, gqa_k_norm_g, w_router,
              w_expert_gate, w_expert_up, w_expert_down, final_norm_g):
    n_lat = x_sample.shape[1]
    dcos, dsin = axial_rope_tables(n_lat, DIFF_QK_DIM, x_sample.dtype)
    gcos, gsin = axial_rope_tables(n_lat, HEAD_DIM, x_sample.dtype)
    rope = (dcos, dsin, gcos, gsin)
    c_ctx_b = jnp.broadcast_to(c_ctx[None, :], (x_prompt.shape[0], D_MODEL))
    xp, xs = x_prompt, x_sample
    new_dk, new_dv, new_gk, new_gv = [], [], [], []
    for l in range(DEPTH):
        lam_init = lambda_init(l)
        lw = (norm1_g[l], norm2_g[l], w_in[l], w_out[l], diff_lambda_q1[l], diff_lambda_k1[l],
              diff_lambda_q2[l], diff_lambda_k2[l], diff_subln_g[l], gqa_q_norm_g[l], gqa_k_norm_g[l],
              w_router[l], w_expert_gate[l], w_expert_up[l], w_expert_down[l])
        mod_ctx = jnp.einsum('bd,de->be', jax.nn.silu(c_ctx_b), w_mod[l]) + b_mod[l]
        mod_lat = jnp.einsum('bd,de->be', jax.nn.silu(c), w_mod[l]) + b_mod[l]
        xp, (dk, dv, gk, gv) = trunk_layer(xp, mod_ctx, None, None, lam_init, *lw)
        new_dk.append(dk)
        new_dv.append(dv)
        new_gk.append(gk)
        new_gv.append(gv)
        ctx_kv = (cache_diff_k[:, l], cache_diff_v[:, l], cache_gqa_k[:, l], cache_gqa_v[:, l])
        xs, _ = trunk_layer(xs, mod_lat, rope, ctx_kv, lam_init, *lw)
    y_prompt = rmsnorm(xp, final_norm_g)
    y_sample = rmsnorm(xs, final_norm_g)
    return (y_prompt, y_sample, jnp.stack(new_dk, axis=1), jnp.stack(new_dv, axis=1),
            jnp.stack(new_gk, axis=1), jnp.stack(new_gv, axis=1))
```

```python
import functools
import math

import jax
import jax.numpy as jnp
from jax import lax
from jax.experimental import pallas as pl
from jax.experimental.pallas import tpu as pltpu

F32 = jnp.float32
BF16 = jnp.bfloat16
I32 = jnp.int32

LANES = 128
HEAD_DIM = 128
N_DIFF_HEADS = 8
DIFF_QK_DIM = HEAD_DIM // 2
N_GQA_HEADS = 8
N_GQA_KV_HEADS = 2
GQA_GROUP = N_GQA_HEADS // N_GQA_KV_HEADS
N_EXPERTS = 16
CAPACITY_FACTOR = 2
GRID_W = 64
ROPE_THETA = 10000.0
NORM_EPS = 1e-6
N_MOD = 6
LAMBDA_INIT_L0 = 0.8 - 0.6 * math.exp(-0.3 * 0)

DQ_BLK = 0
DK_BLK = DQ_BLK + N_DIFF_HEADS
DV_BLK = DK_BLK + N_DIFF_HEADS
GQ_BLK = DV_BLK + N_DIFF_HEADS
GK_BLK = GQ_BLK + N_GQA_HEADS
GV_BLK = GK_BLK + N_GQA_KV_HEADS
IN_BLKS = GV_BLK + N_GQA_KV_HEADS

TOK_TILE = 256
KV_CHUNK = 1024
DISPATCH_WIN = 64
NEG_BIG = -1e30
BISECT_STEPS = 152

VMEM_LIMIT_MB = 56


def _params(n_axes, vmem_mb=VMEM_LIMIT_MB):
    return pltpu.CompilerParams(dimension_semantics=("arbitrary",) * n_axes,
                                vmem_limit_bytes=vmem_mb << 20)


def _sigmoid(x):
    return 1.0 / (1.0 + jnp.exp(-x))


def _rms(x, eps=NORM_EPS):
    return x * lax.rsqrt(jnp.mean(x * x, axis=-1, keepdims=True) + eps)


def _mod_kernel(c_ref, w_ref, b_ref, o_ref):
    c = c_ref[...]
    a = (c * _sigmoid(c)).astype(BF16)
    o_ref[...] = jnp.dot(a, w_ref[...].astype(BF16), preferred_element_type=F32) + b_ref[...]


def _mod_call(cond, w_mod, b_mod):
    d, m = w_mod.shape
    tn = next(t for t in (1024, 512, 256, LANES) if m % t == 0)
    return pl.pallas_call(
        _mod_kernel, name="mod",
        grid=(m // tn,),
        in_specs=[pl.BlockSpec((8, d), lambda j: (0, 0)),
                  pl.BlockSpec((d, tn), lambda j: (0, j)),
                  pl.BlockSpec((1, tn), lambda j: (0, j))],
        out_specs=pl.BlockSpec((8, tn), lambda j: (0, j)),
        out_shape=jax.ShapeDtypeStruct((8, m), F32),
        compiler_params=_params(1),
    )(cond, w_mod, b_mod)


def _rope(x, cos, sin_signed, half):
    lane = lax.broadcasted_iota(I32, x.shape, 1)
    first = (lane % (2 * half)) < half
    rot = jnp.where(first, pltpu.roll(x, LANES - half, 1), pltpu.roll(x, half, 1))
    return x * cos + rot * sin_signed


def _inproj_kernel(xs_ref, xp_ref, mod_ref, n1g_ref, w_ref, qn_ref, kn_ref,
                   cd_ref, sd_ref, cg_ref, sg_ref,
                   qkv_ref, ndk_ref, ndv_ref, ngk_ref, ngv_ref, *, n_lat_tiles):
    i = pl.program_id(0)
    is_ctx = i >= n_lat_tiles
    x = jnp.where(is_ctx, xp_ref[...], xs_ref[...])
    sh1 = mod_ref[0:1, :]
    sc1 = mod_ref[1:2, :]
    h = (_rms(x) * n1g_ref[...]) * (1.0 + sc1) + sh1
    hb = h.astype(BF16)
    cd, sd, cg, sg = cd_ref[...], sd_ref[...], cg_ref[...], sg_ref[...]
    qn, kn = qn_ref[...], kn_ref[...]
    cache_stores = []

    def seg(blk0, nblk):
        return jnp.dot(hb, w_ref[:, blk0 * LANES:(blk0 + nblk) * LANES], preferred_element_type=F32)

    def put(blk, val):
        qkv_ref[:, blk * LANES:(blk + 1) * LANES] = val.astype(BF16)

    p = seg(DQ_BLK, N_DIFF_HEADS)
    for j in range(N_DIFF_HEADS):
        v = p[:, j * LANES:(j + 1) * LANES]
        put(DQ_BLK + j, _rope(v, cd, sd, DIFF_QK_DIM // 4) * (DIFF_QK_DIM ** -0.5))
    p = seg(DK_BLK, N_DIFF_HEADS)
    for j in range(N_DIFF_HEADS):
        v = p[:, j * LANES:(j + 1) * LANES]
        cache_stores.append((ndk_ref, j, v))
        put(DK_BLK + j, _rope(v, cd, sd, DIFF_QK_DIM // 4))
    p = seg(DV_BLK, N_DIFF_HEADS)
    for j in range(N_DIFF_HEADS):
        v = p[:, j * LANES:(j + 1) * LANES]
        cache_stores.append((ndv_ref, j, v))
        put(DV_BLK + j, v)
    p = seg(GQ_BLK, N_GQA_HEADS)
    for j in range(N_GQA_HEADS):
        v = _rms(p[:, j * LANES:(j + 1) * LANES]) * qn
        put(GQ_BLK + j, _rope(v, cg, sg, HEAD_DIM // 4) * (HEAD_DIM ** -0.5))
    p = seg(GK_BLK, 2 * N_GQA_KV_HEADS)
    for j in range(N_GQA_KV_HEADS):
        v = _rms(p[:, j * LANES:(j + 1) * LANES]) * kn
        cache_stores.append((ngk_ref, j, v))
        put(GK_BLK + j, _rope(v, cg, sg, HEAD_DIM // 4))
    for j in range(N_GQA_KV_HEADS):
        v = p[:, (N_GQA_KV_HEADS + j) * LANES:(N_GQA_KV_HEADS + j + 1) * LANES]
        cache_stores.append((ngv_ref, j, v))
        put(GV_BLK + j, v)

    @pl.when(is_ctx)
    def _():
        for ref, j, v in cache_stores:
            ref[:, j * LANES:(j + 1) * LANES] = v


def _inproj_call(xs, xp, mod, n1g, w_in_b, qn, kn, tables, n_lat):
    tl, d = xs.shape
    tc = xp.shape[0]
    tm = TOK_TILE
    nl, nc = tl // tm, tc // tm
    tpb = n_lat // tm
    nbl = tl // n_lat
    width = IN_BLKS * LANES

    def tab_spec():
        return pl.BlockSpec((tm, LANES), lambda i: (jnp.where(i < nl, i % tpb, tpb), 0))

    def cache_spec(w):
        return pl.BlockSpec((tm, w), lambda i: (jnp.maximum(i - nl, 0), 0))

    dw, gw = N_DIFF_HEADS * LANES, N_GQA_KV_HEADS * LANES
    return pl.pallas_call(
        functools.partial(_inproj_kernel, n_lat_tiles=nl), name="inproj",
        grid=(nl + nc,),
        in_specs=[pl.BlockSpec((tm, d), lambda i: (jnp.minimum(i, nl - 1), 0)),
                  pl.BlockSpec((tm, d), lambda i: (jnp.maximum(i - nl, 0), 0)),
                  pl.BlockSpec((None, N_MOD, d), lambda i: (jnp.where(i < nl, i // tpb, nbl), 0, 0)),
                  pl.BlockSpec((1, d), lambda i: (0, 0)),
                  pl.BlockSpec((d, width), lambda i: (0, 0), pipeline_mode=pl.Buffered(1)),
                  pl.BlockSpec((1, LANES), lambda i: (0, 0)),
                  pl.BlockSpec((1, LANES), lambda i: (0, 0)),
                  tab_spec(), tab_spec(), tab_spec(), tab_spec()],
        out_specs=[pl.BlockSpec((tm, width), lambda i: (i, 0)),
                   cache_spec(dw), cache_spec(dw), cache_spec(gw), cache_spec(gw)],
        out_shape=[jax.ShapeDtypeStruct((tl + tc, width), BF16),
                   jax.ShapeDtypeStruct((tc, dw), F32), jax.ShapeDtypeStruct((tc, dw), F32),
                   jax.ShapeDtypeStruct((tc, gw), F32), jax.ShapeDtypeStruct((tc, gw), F32)],
        compiler_params=_params(1),
    )(xs, xp, mod, n1g, w_in_b, qn, kn, *tables)


def _rope_tables(n, dim, pad_rows):
    t = jnp.arange(n, dtype=I32)
    row = (t // GRID_W).astype(F32)
    col = (t % GRID_W).astype(F32)
    sec = dim // 2
    inv = ROPE_THETA ** (-jnp.arange(0, sec, 2, dtype=F32) / sec)
    ang = jnp.stack([row[:, None] * inv, col[:, None] * inv], axis=1)
    ang = jnp.concatenate([ang, ang], axis=-1).reshape(n, dim)
    sign = jnp.where((jnp.arange(dim) % sec) < sec // 2, -1.0, 1.0).astype(F32)
    cos, sin = jnp.cos(ang), jnp.sin(ang) * sign
    reps = LANES // dim
    cos, sin = jnp.tile(cos, (1, reps)), jnp.tile(sin, (1, reps))
    cos = jnp.concatenate([cos, jnp.ones((pad_rows, LANES), F32)], axis=0)
    sin = jnp.concatenate([sin, jnp.zeros((pad_rows, LANES), F32)], axis=0)
    return cos, sin


def _flash(q, chunks):
    m_rows = q.shape[0]
    m = jnp.full((m_rows, 1), NEG_BIG, F32)
    l = jnp.zeros((m_rows, 1), F32)
    acc = jnp.zeros((m_rows, HEAD_DIM), F32)
    for get_k, get_v in chunks:
        s = lax.dot_general(q, get_k(), (((1,), (1,)), ((), ())), preferred_element_type=F32)
        m_new = jnp.maximum(m, jnp.max(s, axis=-1, keepdims=True))
        alpha = jnp.exp(m - m_new)
        p = jnp.exp(s - m_new)
        l = alpha * l + jnp.sum(p, axis=-1, keepdims=True)
        acc = alpha * acc + jnp.dot(p.astype(BF16), get_v(), preferred_element_type=F32)
        m = m_new
    return acc, l


def _kv_chunks(k_ref, v_ref, ck_ref, cv_ref, col):
    cs = slice(col * LANES, (col + 1) * LANES)
    chunks = []
    if ck_ref is not None:
        chunks.append((lambda: ck_ref[:, cs].astype(BF16), lambda: cv_ref[:, cs].astype(BF16)))
    n = k_ref.shape[0]
    step = min(n, KV_CHUNK)
    for c0 in range(0, n, step):
        rs = slice(c0, c0 + step)
        chunks.append((lambda rs=rs: k_ref[rs, cs], lambda rs=rs: v_ref[rs, cs]))
    return chunks


def _attn_diff_kernel(*refs, hps, has_cache):
    if has_cache:
        q_ref, k_ref, v_ref, ck_ref, cv_ref, lam_ref, g_ref, o_ref = refs
    else:
        q_ref, k_ref, v_ref, lam_ref, g_ref, o_ref = refs
        ck_ref = cv_ref = None
    lv = lam_ref[...]
    lam = (jnp.exp(jnp.sum(lv[0:1] * lv[1:2], axis=-1, keepdims=True))
           - jnp.exp(jnp.sum(lv[2:3] * lv[3:4], axis=-1, keepdims=True)) + LAMBDA_INIT_L0)
    for j in range(hps):
        q = q_ref[:, j * LANES:(j + 1) * LANES]
        lane = lax.broadcasted_iota(I32, q.shape, 1)
        zero = jnp.zeros_like(q)
        chunks = _kv_chunks(k_ref, v_ref, ck_ref, cv_ref, j)
        acc1, l1 = _flash(jnp.where(lane < DIFF_QK_DIM, q, zero), chunks)
        acc2, l2 = _flash(jnp.where(lane >= DIFF_QK_DIM, q, zero), chunks)
        o = acc1 / l1 - lam * (acc2 / l2)
        o = (_rms(o) * g_ref[...]) * (1.0 - LAMBDA_INIT_L0)
        o_ref[:, j * LANES:(j + 1) * LANES] = o.astype(o_ref.dtype)


def _attn_diff_call(qkv, ck, cv, lamv, subg, *, row0, nb, n, tq, hps):
    w = hps * LANES
    hb_n = N_DIFF_HEADS // hps
    qpb = n // tq
    in_specs = [pl.BlockSpec((tq, w), lambda b, h, qi: (row0 // tq + b * qpb + qi, DQ_BLK // hps + h)),
                pl.BlockSpec((n, w), lambda b, h, qi: (row0 // n + b, DK_BLK // hps + h)),
                pl.BlockSpec((n, w), lambda b, h, qi: (row0 // n + b, DV_BLK // hps + h))]
    args = [qkv, qkv, qkv]
    if ck is not None:
        past = ck.shape[0] // nb
        in_specs += [pl.BlockSpec((past, w), lambda b, h, qi: (b, h)),
                     pl.BlockSpec((past, w), lambda b, h, qi: (b, h))]
        args += [ck, cv]
    in_specs += [pl.BlockSpec(lamv.shape, lambda b, h, qi: (0, 0)),
                 pl.BlockSpec((1, LANES), lambda b, h, qi: (0, 0))]
    args += [lamv, subg]
    return pl.pallas_call(
        functools.partial(_attn_diff_kernel, hps=hps, has_cache=ck is not None),
        name="attn_diff_lat" if ck is not None else "attn_diff_ctx",
        grid=(nb, hb_n, qpb),
        in_specs=in_specs,
        out_specs=pl.BlockSpec((tq, w), lambda b, h, qi: (b * qpb + qi, h)),
        out_shape=jax.ShapeDtypeStruct((nb * n, N_DIFF_HEADS * LANES), BF16),
        compiler_params=_params(3),
    )(*args)


def _attn_gqa_kernel(*refs, has_cache):
    if has_cache:
        q_ref, k_ref, v_ref, ck_ref, cv_ref, o_ref = refs
    else:
        q_ref, k_ref, v_ref, o_ref = refs
        ck_ref = cv_ref = None
    tq = q_ref.shape[0]
    q = jnp.concatenate([q_ref[:, r * LANES:(r + 1) * LANES] for r in range(GQA_GROUP)], axis=0)
    acc, l = _flash(q, _kv_chunks(k_ref, v_ref, ck_ref, cv_ref, 0))
    o = (acc / l).astype(o_ref.dtype)
    for r in range(GQA_GROUP):
        o_ref[:, r * LANES:(r + 1) * LANES] = o[r * tq:(r + 1) * tq]


def _attn_gqa_call(qkv, ck, cv, *, row0, nb, n, tq):
    w = GQA_GROUP * LANES
    qpb = n // tq
    in_specs = [pl.BlockSpec((tq, w), lambda b, g, qi: (row0 // tq + b * qpb + qi, GQ_BLK // GQA_GROUP + g)),
                pl.BlockSpec((n, LANES), lambda b, g, qi: (row0 // n + b, GK_BLK + g)),
                pl.BlockSpec((n, LANES), lambda b, g, qi: (row0 // n + b, GV_BLK + g))]
    args = [qkv, qkv, qkv]
    if ck is not None:
        past = ck.shape[0] // nb
        in_specs += [pl.BlockSpec((past, LANES), lambda b, g, qi: (b, g)),
                     pl.BlockSpec((past, LANES), lambda b, g, qi: (b, g))]
        args += [ck, cv]
    return pl.pallas_call(
        functools.partial(_attn_gqa_kernel, has_cache=ck is not None),
        name="attn_gqa_lat" if ck is not None else "attn_gqa_ctx",
        grid=(nb, N_GQA_KV_HEADS, qpb),
        in_specs=in_specs,
        out_specs=pl.BlockSpec((tq, w), lambda b, g, qi: (b * qpb + qi, g)),
        out_shape=jax.ShapeDtypeStruct((nb * n, N_GQA_HEADS * LANES), BF16),
        compiler_params=_params(3),
    )(*args)


def _outproj_kernel(odl_ref, ogl_ref, odc_ref, ogc_ref, xs_ref, xp_ref, mod_ref, n2g_ref, w_ref, wr_ref,
                    x1_ref, h2_ref, aff_ref, *, n_lat_tiles):
    i = pl.program_id(0)
    is_ctx = i >= n_lat_tiles
    od = jnp.where(is_ctx, odc_ref[...], odl_ref[...])
    og = jnp.where(is_ctx, ogc_ref[...], ogl_ref[...])
    x = jnp.where(is_ctx, xp_ref[...], xs_ref[...])
    dw = od.shape[1]
    a = (jnp.dot(od, w_ref[0:dw, :], preferred_element_type=F32)
         + jnp.dot(og, w_ref[dw:, :], preferred_element_type=F32))
    g1 = mod_ref[2:3, :]
    sh2 = mod_ref[3:4, :]
    sc2 = mod_ref[4:5, :]
    x1 = x + g1 * a
    x1_ref[...] = x1
    h2 = (_rms(x1) * n2g_ref[...]) * (1.0 + sc2) + sh2
    h2_ref[...] = h2.astype(BF16)
    logits = lax.dot_general(wr_ref[...], h2, (((1,), (1,)), ((), ())),
                             precision=lax.Precision.HIGHEST, preferred_element_type=F32)
    z = jnp.exp(logits - jnp.max(logits, axis=0, keepdims=True))
    aff_ref[...] = z / jnp.sum(z, axis=0, keepdims=True)


def _outproj_call(odl, ogl, odc, ogc, xs, xp, mod, n2g, w_out_b, wr_t, n_lat):
    tl, d = xs.shape
    tc = xp.shape[0]
    tm = TOK_TILE
    nl, nc = tl // tm, tc // tm
    tpb = n_lat // tm
    nbl = tl // n_lat
    aw = odl.shape[1]
    ne = wr_t.shape[0]

    def lat_spec(w):
        return pl.BlockSpec((tm, w), lambda i: (jnp.minimum(i, nl - 1), 0))

    def ctx_spec(w):
        return pl.BlockSpec((tm, w), lambda i: (jnp.maximum(i - nl, 0), 0))

    return pl.pallas_call(
        functools.partial(_outproj_kernel, n_lat_tiles=nl), name="outproj",
        grid=(nl + nc,),
        in_specs=[lat_spec(aw), lat_spec(aw), ctx_spec(aw), ctx_spec(aw), lat_spec(d), ctx_spec(d),
                  pl.BlockSpec((None, N_MOD, d), lambda i: (jnp.where(i < nl, i // tpb, nbl), 0, 0)),
                  pl.BlockSpec((1, d), lambda i: (0, 0)),
                  pl.BlockSpec(w_out_b.shape, lambda i: (0, 0), pipeline_mode=pl.Buffered(1)),
                  pl.BlockSpec((ne, d), lambda i: (0, 0))],
        out_specs=[pl.BlockSpec((tm, d), lambda i: (i, 0)),
                   pl.BlockSpec((tm, d), lambda i: (i, 0)),
                   pl.BlockSpec((ne, tm), lambda i: (0, i))],
        out_shape=[jax.ShapeDtypeStruct((tl + tc, d), F32),
                   jax.ShapeDtypeStruct((tl + tc, d), BF16),
                   jax.ShapeDtypeStruct((ne, tl + tc), F32)],
        compiler_params=_params(1),
    )(odl, ogl, odc, ogc, xs, xp, mod, n2g, w_out_b, wr_t)


def _route_kernel(aff_ref, slot_ref, starts_ref, *, cap, sets_per_group):
    ne, n = aff_ref.shape
    tt = TOK_TILE
    aff = aff_ref[...]
    capf = float(cap)

    def count(mask):
        return jnp.sum(mask.astype(F32), axis=1, keepdims=True)

    def bisect(_, carry):
        lo, hi = carry
        mid = 0.5 * (lo + hi)
        ok = count(aff >= mid) >= capf
        return jnp.where(ok, mid, lo), jnp.where(ok, hi, mid)

    thr, _ = lax.fori_loop(0, BISECT_STEPS, bisect,
                           (jnp.zeros((ne, 1), F32), jnp.full((ne, 1), 2.0, F32)))
    gt = aff > thr
    tie = aff == thr
    need = capf - count(gt)

    ri = lax.broadcasted_iota(I32, (tt, tt), 0)
    ci = lax.broadcasted_iota(I32, (tt, tt), 1)
    upper = (ri < ci).astype(BF16)
    offset = ((pl.program_id(0) % sets_per_group) * cap).astype(F32)
    tie_pref = jnp.zeros((ne, 1), F32)
    sel_pref = jnp.zeros((ne, 1), F32) + offset
    lane = lax.broadcasted_iota(I32, (ne, LANES), 1)
    starts = jnp.zeros((ne, LANES), F32)
    for t in range(n // tt):
        cs = slice(t * tt, (t + 1) * tt)
        tie_t = tie[:, cs]
        tie_rank = tie_pref + jnp.dot(tie_t.astype(BF16), upper, preferred_element_type=F32)
        sel_t = gt[:, cs] | (tie_t & (tie_rank < need))
        slot_t = sel_pref + jnp.dot(sel_t.astype(BF16), upper, preferred_element_type=F32)
        slot_ref[:, cs] = jnp.where(sel_t, slot_t, -1.0).astype(I32)
        starts = jnp.where(lane == t, sel_pref, starts)
        tie_pref = tie_pref + count(tie_t)
        sel_pref = sel_pref + count(sel_t)
    starts = jnp.where(lane == n // tt, sel_pref, starts)
    starts_ref[...] = starts.astype(I32)


def _route_call(aff_t, *, col0, n_sets, n, sets_per_group, name):
    ne = aff_t.shape[0]
    cap = CAPACITY_FACTOR * n // N_EXPERTS
    return pl.pallas_call(
        functools.partial(_route_kernel, cap=cap, sets_per_group=sets_per_group), name=name,
        grid=(n_sets,),
        in_specs=[pl.BlockSpec((ne, n), lambda s: (0, col0 // n + s))],
        out_specs=[pl.BlockSpec((ne, n), lambda s: (0, s)),
                   pl.BlockSpec((None, ne, LANES), lambda s: (s, 0, 0))],
        out_shape=[jax.ShapeDtypeStruct((ne, n_sets * n), I32),
                   jax.ShapeDtypeStruct((n_sets, ne, LANES), I32)],
        compiler_params=_params(1),
    )(aff_t)


def _moe_kernel(tab_ref, h2_ref, slot_ref, wg_ref, wu_ref, wd_ref, y_ref, xacc_ref, xb_ref, yacc_ref,
                *, tiles_per_group):
    g, e, f = pl.program_id(0), pl.program_id(1), pl.program_id(2)
    ne = pl.num_programs(1)
    rows = xb_ref.shape[0]
    tt = TOK_TILE
    win = DISPATCH_WIN

    @pl.when(f == 0)
    def _():
        xacc_ref[...] = jnp.zeros_like(xacc_ref)
        base = (g * ne + e) * (tiles_per_group + 1)
        sub = lax.broadcasted_iota(I32, (win, tt), 0)

        def tile_body(t, carry):
            rs = tab_ref[base + t]
            re = tab_ref[base + t + 1]
            a0 = (rs // 8) * 8
            h_t = h2_ref[pl.ds(pl.multiple_of(t * tt, tt), tt), :]
            srow = slot_ref[pl.ds(t, 1), :]

            def win_body(w, c):
                off = pl.multiple_of(a0 + w * win, 8)
                onehot = (srow - off == sub).astype(BF16)
                xacc_ref[pl.ds(off, win), :] += jnp.dot(onehot, h_t, preferred_element_type=F32)
                return c

            return lax.fori_loop(0, (re - a0 + win - 1) // win, win_body, carry)

        lax.fori_loop(0, tiles_per_group, tile_body, 0)
        xb_ref[...] = xacc_ref[0:rows, :].astype(BF16)
        yacc_ref[...] = jnp.zeros_like(yacc_ref)

    xb = xb_ref[...]
    gate = jnp.dot(xb, wg_ref[...].astype(BF16), preferred_element_type=F32)
    up = jnp.dot(xb, wu_ref[...].astype(BF16), preferred_element_type=F32)
    hid = (gate * _sigmoid(gate) * up).astype(BF16)
    yacc_ref[...] += jnp.dot(hid, wd_ref[...].astype(BF16), preferred_element_type=F32)

    @pl.when(f == pl.num_programs(2) - 1)
    def _():
        y_ref[...] = yacc_ref[...].astype(y_ref.dtype)


def _moe_call(tab, h2g, slot_g, w_gate, w_up, w_down, rows):
    ng, nt, d = h2g.shape
    ne, _, fdim = w_gate.shape
    tpg = nt // TOK_TILE
    tf = min(fdim, 256)
    grid_spec = pltpu.PrefetchScalarGridSpec(
        num_scalar_prefetch=1,
        grid=(ng, ne, fdim // tf),
        in_specs=[pl.BlockSpec((None, nt, d), lambda g, e, f, tab: (g, 0, 0), pipeline_mode=pl.Buffered(1)),
                  pl.BlockSpec((None, None, tpg, TOK_TILE), lambda g, e, f, tab: (g, e, 0, 0)),
                  pl.BlockSpec((None, d, tf), lambda g, e, f, tab: (e, 0, f)),
                  pl.BlockSpec((None, d, tf), lambda g, e, f, tab: (e, 0, f)),
                  pl.BlockSpec((None, tf, d), lambda g, e, f, tab: (e, f, 0))],
        out_specs=pl.BlockSpec((None, None, rows, d), lambda g, e, f, tab: (g, e, 0, 0)),
        scratch_shapes=[pltpu.VMEM((rows + DISPATCH_WIN, d), F32),
                        pltpu.VMEM((rows, d), BF16),
                        pltpu.VMEM((rows, d), F32)])
    return pl.pallas_call(
        functools.partial(_moe_kernel, tiles_per_group=tpg), name="moe",
        grid_spec=grid_spec,
        out_shape=jax.ShapeDtypeStruct((ng, ne, rows, d), BF16),
        compiler_params=_params(3),
    )(tab, h2g, slot_g, w_gate, w_up, w_down)


def _combine_kernel(tab_ref, x1_ref, mod_ref, slot_ref, gate_ref, fng_ref, y_hbm, ys_ref, yp_ref,
                    ybuf, acc_ref, sem, *, tiles_per_group, n_lat_tiles, rows):
    i = pl.program_id(0)
    g = i // tiles_per_group
    t = i % tiles_per_group
    ne = ybuf.shape[0]
    win = DISPATCH_WIN
    tt = TOK_TILE

    def window_copy(e, start):
        return pltpu.make_async_copy(y_hbm.at[g, e, pl.ds(start, win), :], ybuf.at[e], sem.at[e])

    def first_start(e):
        rs = tab_ref[(g * ne + e) * (tiles_per_group + 1) + t]
        return pl.multiple_of(jnp.minimum((rs // 16) * 16, rows - win), 16)

    for e in range(ne):
        window_copy(e, first_start(e)).start()

    acc_ref[...] = jnp.zeros_like(acc_ref)
    lane = lax.broadcasted_iota(I32, (tt, win), 1)
    for e in range(ne):
        a0 = first_start(e)
        re = tab_ref[(g * ne + e) * (tiles_per_group + 1) + t + 1]
        slot = slot_ref[:, e:e + 1]
        gate = gate_ref[:, e:e + 1]
        window_copy(e, a0).wait()
        onehot = (slot - a0 == lane).astype(BF16)
        acc_ref[...] += jnp.dot(onehot, ybuf[e], preferred_element_type=F32) * gate

        def extra(w, c, e=e, a0=a0, slot=slot, gate=gate):
            lo = a0 + w * win
            start = pl.multiple_of(jnp.minimum(lo, rows - win), 16)
            cp = window_copy(e, start)
            cp.start()
            cp.wait()
            onehot = ((slot - start == lane) & (slot >= lo)).astype(BF16)
            acc_ref[...] += jnp.dot(onehot, ybuf[e], preferred_element_type=F32) * gate
            return c

        lax.fori_loop(1, (re - a0 + win - 1) // win, extra, 0)

    g2 = mod_ref[5:6, :]
    x2 = x1_ref[...] + g2 * acc_ref[...]
    y = _rms(x2) * fng_ref[...]

    @pl.when(i < n_lat_tiles)
    def _():
        ys_ref[...] = y

    @pl.when(i >= n_lat_tiles)
    def _():
        yp_ref[...] = y


def _combine_call(tab, x1, mod, slot_n, gate_n, fng, y, *, n_lat_tokens, n_ctx_tokens, tiles_per_group):
    t_all, d = x1.shape
    tm = TOK_TILE
    nl = n_lat_tokens // tm
    ne = slot_n.shape[1]
    rows = y.shape[2]
    grid_spec = pltpu.PrefetchScalarGridSpec(
        num_scalar_prefetch=1,
        grid=(t_all // tm,),
        in_specs=[pl.BlockSpec((tm, d), lambda i, tab: (i, 0)),
                  pl.BlockSpec((None, N_MOD, d), lambda i, tab: (i // tiles_per_group, 0, 0)),
                  pl.BlockSpec((tm, ne), lambda i, tab: (i, 0)),
                  pl.BlockSpec((tm, ne), lambda i, tab: (i, 0)),
                  pl.BlockSpec((1, d), lambda i, tab: (0, 0)),
                  pl.BlockSpec(memory_space=pl.ANY)],
        out_specs=[pl.BlockSpec((tm, d), lambda i, tab: (jnp.minimum(i, nl - 1), 0)),
                   pl.BlockSpec((tm, d), lambda i, tab: (jnp.maximum(i - nl, 0), 0))],
        scratch_shapes=[pltpu.VMEM((ne, DISPATCH_WIN, d), BF16),
                        pltpu.VMEM((tm, d), F32),
                        pltpu.SemaphoreType.DMA((ne,))])
    return pl.pallas_call(
        functools.partial(_combine_kernel, tiles_per_group=tiles_per_group, n_lat_tiles=nl, rows=rows),
        name="combine",
        grid_spec=grid_spec,
        out_shape=[jax.ShapeDtypeStruct((n_lat_tokens, d), F32),
                   jax.ShapeDtypeStruct((n_ctx_tokens, d), F32)],
        compiler_params=_params(1),
    )(tab, x1, mod, slot_n, gate_n, fng, y)


def kernel(x_prompt, x_sample, c, cache_diff_k, cache_diff_v, cache_gqa_k, cache_gqa_v, c_ctx, norm1_g, norm2_g, w_mod, b_mod, w_in, w_out, diff_lambda_q1, diff_lambda_k1, diff_lambda_q2, diff_lambda_k2, diff_subln_g, gqa_q_norm_g, gqa_k_norm_g, w_router, w_expert_gate, w_expert_up, w_expert_down, final_norm_g):
    nbc, seq, d = x_prompt.shape
    nbl, n_lat, _ = x_sample.shape
    depth, past = cache_diff_k.shape[1], cache_diff_k.shape[2]
    tc, tl = nbc * seq, nbl * n_lat
    assert depth == 1, "single trunk layer"
    assert tc == n_lat, "the context tokens must fill exactly one routing group"
    assert seq == TOK_TILE and n_lat % KV_CHUNK == 0 and nbl + 1 <= 8
    ne = N_EXPERTS
    tpg = n_lat // TOK_TILE
    ng = nbl + 1
    rows = CAPACITY_FACTOR * n_lat // ne

    xs = x_sample.reshape(tl, d)
    xp = x_prompt.reshape(tc, d)

    cond = jnp.concatenate([c, c_ctx[None, :], jnp.zeros((8 - ng, d), F32)], axis=0)
    mod = _mod_call(cond, w_mod[0], b_mod).reshape(8, N_MOD, d)

    tables = (_rope_tables(n_lat, DIFF_QK_DIM, TOK_TILE) + _rope_tables(n_lat, HEAD_DIM, TOK_TILE))
    qkv, ndk, ndv, ngk, ngv = _inproj_call(
        xs, xp, mod, norm1_g, w_in[0].astype(BF16), gqa_q_norm_g, gqa_k_norm_g, tables, n_lat)

    lamv = jnp.concatenate([diff_lambda_q1, diff_lambda_k1, diff_lambda_q2, diff_lambda_k2], axis=0)
    ckd = cache_diff_k.reshape(nbl * past, N_DIFF_HEADS * LANES)
    cvd = cache_diff_v.reshape(nbl * past, N_DIFF_HEADS * LANES)
    ckg = cache_gqa_k.reshape(nbl * past, N_GQA_KV_HEADS * LANES)
    cvg = cache_gqa_v.reshape(nbl * past, N_GQA_KV_HEADS * LANES)
    odl = _attn_diff_call(qkv, ckd, cvd, lamv, diff_subln_g, row0=0, nb=nbl, n=n_lat, tq=256, hps=1)
    ogl = _attn_gqa_call(qkv, ckg, cvg, row0=0, nb=nbl, n=n_lat, tq=128)
    odc = _attn_diff_call(qkv, None, None, lamv, diff_subln_g, row0=tl, nb=nbc, n=seq, tq=seq,
                          hps=N_DIFF_HEADS)
    ogc = _attn_gqa_call(qkv, None, None, row0=tl, nb=nbc, n=seq, tq=seq)

    x1, h2, aff_t = _outproj_call(odl, ogl, odc, ogc, xs, xp, mod, norm2_g, w_out[0].astype(BF16),
                                  w_router[0].T, n_lat)

    slot_l, starts_l = _route_call(aff_t, col0=0, n_sets=nbl, n=n_lat, sets_per_group=1, name="route_lat")
    slot_c, starts_c = _route_call(aff_t, col0=tl, n_sets=nbc, n=seq, sets_per_group=nbc, name="route_ctx")
    slot_t = jnp.concatenate([slot_l, slot_c], axis=1)
    tab_l = starts_l[:, :, :tpg + 1]
    tab_c = jnp.concatenate([starts_c[:, :, 0].T, starts_c[-1:, :, 1].T], axis=1)[None]
    tab = jnp.concatenate([tab_l, tab_c], axis=0).reshape(-1)

    slot_g = slot_t.reshape(ne, ng, tpg, TOK_TILE).transpose(1, 0, 2, 3)
    y = _moe_call(tab, h2.reshape(ng, n_lat, d), slot_g, w_expert_gate[0], w_expert_up[0],
                  w_expert_down[0], rows)

    ys, yp = _combine_call(tab, x1, mod, slot_t.T, aff_t.T, final_norm_g[None, :], y,
                           n_lat_tokens=tl, n_ctx_tokens=tc, tiles_per_group=tpg)

    return (yp.reshape(nbc, seq, d), ys.reshape(nbl, n_lat, d),
            ndk.reshape(nbc, 1, seq, N_DIFF_HEADS, 2, DIFF_QK_DIM),
            ndv.reshape(nbc, 1, seq, N_DIFF_HEADS, HEAD_DIM),
            ngk.reshape(nbc, 1, seq, N_GQA_KV_HEADS, HEAD_DIM),
            ngv.reshape(nbc, 1, seq, N_GQA_KV_HEADS, HEAD_DIM))
```

```python
import functools
import math

import jax
import jax.numpy as jnp
from jax import lax
from jax.experimental import pallas as pl
from jax.experimental.pallas import tpu as pltpu

F32 = jnp.float32
BF16 = jnp.bfloat16
I32 = jnp.int32

LANES = 128
HEAD_DIM = 128
N_DIFF_HEADS = 8
DIFF_QK_DIM = HEAD_DIM // 2
N_GQA_HEADS = 8
N_GQA_KV_HEADS = 2
GQA_GROUP = N_GQA_HEADS // N_GQA_KV_HEADS
N_EXPERTS = 16
CAPACITY_FACTOR = 2
GRID_W = 64
ROPE_THETA = 10000.0
NORM_EPS = 1e-6
N_MOD = 6
LAMBDA_INIT_L0 = 0.8 - 0.6 * math.exp(-0.3 * 0)

DQ_BLK = 0
DK_BLK = DQ_BLK + N_DIFF_HEADS
DV_BLK = DK_BLK + N_DIFF_HEADS
GQ_BLK = DV_BLK + N_DIFF_HEADS
GK_BLK = GQ_BLK + N_GQA_HEADS
GV_BLK = GK_BLK + N_GQA_KV_HEADS
IN_BLKS = GV_BLK + N_GQA_KV_HEADS

TOK_TILE = 256
KV_CHUNK = 1024
DISPATCH_WIN = 64
NEG_BIG = -1e30
BISECT_STEPS = 152

VMEM_LIMIT_MB = 56


def _params(n_axes, vmem_mb=VMEM_LIMIT_MB):
    return pltpu.CompilerParams(dimension_semantics=("arbitrary",) * n_axes,
                                vmem_limit_bytes=vmem_mb << 20)


def _sigmoid(x):
    return 1.0 / (1.0 + jnp.exp(-x))


def _rms(x, eps=NORM_EPS):
    return x * lax.rsqrt(jnp.mean(x * x, axis=-1, keepdims=True) + eps)


def _mod_kernel(c_ref, w_ref, b_ref, o_ref):
    c = c_ref[...]
    a = (c * _sigmoid(c)).astype(BF16)
    o_ref[...] = jnp.dot(a, w_ref[...].astype(BF16), preferred_element_type=F32) + b_ref[...]


def _mod_call(cond, w_mod, b_mod):
    d, m = w_mod.shape
    tn = next(t for t in (1024, 512, 256, LANES) if m % t == 0)
    return pl.pallas_call(
        _mod_kernel, name="mod",
        grid=(m // tn,),
        in_specs=[pl.BlockSpec((8, d), lambda j: (0, 0)),
                  pl.BlockSpec((d, tn), lambda j: (0, j)),
                  pl.BlockSpec((1, tn), lambda j: (0, j))],
        out_specs=pl.BlockSpec((8, tn), lambda j: (0, j)),
        out_shape=jax.ShapeDtypeStruct((8, m), F32),
        compiler_params=_params(1),
    )(cond, w_mod, b_mod)


def _rope(x, cos, sin_signed, half):
    lane = lax.broadcasted_iota(I32, x.shape, 1)
    first = (lane % (2 * half)) < half
    rot = jnp.where(first, pltpu.roll(x, LANES - half, 1), pltpu.roll(x, half, 1))
    return x * cos + rot * sin_signed


def _inproj_kernel(xs_ref, xp_ref, mod_ref, n1g_ref, w_ref, qn_ref, kn_ref,
                   cd_ref, sd_ref, cg_ref, sg_ref,
                   qkv_ref, ndk_ref, ndv_ref, ngk_ref, ngv_ref, *, n_lat_tiles):
    i = pl.program_id(0)
    is_ctx = i >= n_lat_tiles
    x = jnp.where(is_ctx, xp_ref[...], xs_ref[...])
    sh1 = mod_ref[0:1, :]
    sc1 = mod_ref[1:2, :]
    h = (_rms(x) * n1g_ref[...]) * (1.0 + sc1) + sh1
    hb = h.astype(BF16)
    cd, sd, cg, sg = cd_ref[...], sd_ref[...], cg_ref[...], sg_ref[...]
    qn, kn = qn_ref[...], kn_ref[...]
    cache_stores = []

    def seg(blk0, nblk):
        return jnp.dot(hb, w_ref[:, blk0 * LANES:(blk0 + nblk) * LANES], preferred_element_type=F32)

    def put(blk, val):
        qkv_ref[:, blk * LANES:(blk + 1) * LANES] = val.astype(BF16)

    p = seg(DQ_BLK, N_DIFF_HEADS)
    for j in range(N_DIFF_HEADS):
        v = p[:, j * LANES:(j + 1) * LANES]
        put(DQ_BLK + j, _rope(v, cd, sd, DIFF_QK_DIM // 4) * (DIFF_QK_DIM ** -0.5))
    p = seg(DK_BLK, N_DIFF_HEADS)
    for j in range(N_DIFF_HEADS):
        v = p[:, j * LANES:(j + 1) * LANES]
        cache_stores.append((ndk_ref, j, v))
        put(DK_BLK + j, _rope(v, cd, sd, DIFF_QK_DIM // 4))
    p = seg(DV_BLK, N_DIFF_HEADS)
    for j in range(N_DIFF_HEADS):
        v = p[:, j * LANES:(j + 1) * LANES]
        cache_stores.append((ndv_ref, j, v))
        put(DV_BLK + j, v)
    p = seg(GQ_BLK, N_GQA_HEADS)
    for j in range(N_GQA_HEADS):
        v = _rms(p[:, j * LANES:(j + 1) * LANES]) * qn
        put(GQ_BLK + j, _rope(v, cg, sg, HEAD_DIM // 4) * (HEAD_DIM ** -0.5))
    p = seg(GK_BLK, 2 * N_GQA_KV_HEADS)
    for j in range(N_GQA_KV_HEADS):
        v = _rms(p[:, j * LANES:(j + 1) * LANES]) * kn
        cache_stores.append((ngk_ref, j, v))
        put(GK_BLK + j, _rope(v, cg, sg, HEAD_DIM // 4))
    for j in range(N_GQA_KV_HEADS):
        v = p[:, (N_GQA_KV_HEADS + j) * LANES:(N_GQA_KV_HEADS + j + 1) * LANES]
        cache_stores.append((ngv_ref, j, v))
        put(GV_BLK + j, v)

    @pl.when(is_ctx)
    def _():
        for ref, j, v in cache_stores:
            ref[:, j * LANES:(j + 1) * LANES] = v


def _inproj_call(xs, xp, mod, n1g, w_in_b, qn, kn, tables, n_lat):
    tl, d = xs.shape
    tc = xp.shape[0]
    tm = TOK_TILE
    nl, nc = tl // tm, tc // tm
    tpb = n_lat // tm
    nbl = tl // n_lat
    width = IN_BLKS * LANES

    def tab_spec():
        return pl.BlockSpec((tm, LANES), lambda i: (jnp.where(i < nl, i % tpb, tpb), 0))

    def cache_spec(w):
        return pl.BlockSpec((tm, w), lambda i: (jnp.maximum(i - nl, 0), 0))

    dw, gw = N_DIFF_HEADS * LANES, N_GQA_KV_HEADS * LANES
    return pl.pallas_call(
        functools.partial(_inproj_kernel, n_lat_tiles=nl), name="inproj",
        grid=(nl + nc,),
        in_specs=[pl.BlockSpec((tm, d), lambda i: (jnp.minimum(i, nl - 1), 0)),
                  pl.BlockSpec((tm, d), lambda i: (jnp.maximum(i - nl, 0), 0)),
                  pl.BlockSpec((None, N_MOD, d), lambda i: (jnp.where(i < nl, i // tpb, nbl), 0, 0)),
                  pl.BlockSpec((1, d), lambda i: (0, 0)),
                  pl.BlockSpec((d, width), lambda i: (0, 0), pipeline_mode=pl.Buffered(1)),
                  pl.BlockSpec((1, LANES), lambda i: (0, 0)),
                  pl.BlockSpec((1, LANES), lambda i: (0, 0)),
                  tab_spec(), tab_spec(), tab_spec(), tab_spec()],
        out_specs=[pl.BlockSpec((tm, width), lambda i: (i, 0)),
                   cache_spec(dw), cache_spec(dw), cache_spec(gw), cache_spec(gw)],
        out_shape=[jax.ShapeDtypeStruct((tl + tc, width), BF16),
                   jax.ShapeDtypeStruct((tc, dw), F32), jax.ShapeDtypeStruct((tc, dw), F32),
                   jax.ShapeDtypeStruct((tc, gw), F32), jax.ShapeDtypeStruct((tc, gw), F32)],
        compiler_params=_params(1),
    )(xs, xp, mod, n1g, w_in_b, qn, kn, *tables)


def _rope_tables(n, dim, pad_rows):
    t = jnp.arange(n, dtype=I32)
    row = (t // GRID_W).astype(F32)
    col = (t % GRID_W).astype(F32)
    sec = dim // 2
    inv = ROPE_THETA ** (-jnp.arange(0, sec, 2, dtype=F32) / sec)
    ang = jnp.stack([row[:, None] * inv, col[:, None] * inv], axis=1)
    ang = jnp.concatenate([ang, ang], axis=-1).reshape(n, dim)
    sign = jnp.where((jnp.arange(dim) % sec) < sec // 2, -1.0, 1.0).astype(F32)
    cos, sin = jnp.cos(ang), jnp.sin(ang) * sign
    reps = LANES // dim
    cos, sin = jnp.tile(cos, (1, reps)), jnp.tile(sin, (1, reps))
    cos = jnp.concatenate([cos, jnp.ones((pad_rows, LANES), F32)], axis=0)
    sin = jnp.concatenate([sin, jnp.zeros((pad_rows, LANES), F32)], axis=0)
    return cos, sin


def _flash(q, chunks):
    m_rows = q.shape[0]
    m = jnp.full((m_rows, 1), NEG_BIG, F32)
    l = jnp.zeros((m_rows, 1), F32)
    acc = jnp.zeros((m_rows, HEAD_DIM), F32)
    for get_k, get_v in chunks:
        s = lax.dot_general(q, get_k(), (((1,), (1,)), ((), ())), preferred_element_type=F32)
        m_new = jnp.maximum(m, jnp.max(s, axis=-1, keepdims=True))
        alpha = jnp.exp(m - m_new)
        p = jnp.exp(s - m_new)
        l = alpha * l + jnp.sum(p, axis=-1, keepdims=True)
        acc = alpha * acc + jnp.dot(p.astype(BF16), get_v(), preferred_element_type=F32)
        m = m_new
    return acc, l


def _kv_chunks(k_ref, v_ref, ck_ref, cv_ref, col):
    cs = slice(col * LANES, (col + 1) * LANES)
    chunks = []
    if ck_ref is not None:
        chunks.append((lambda: ck_ref[:, cs].astype(BF16), lambda: cv_ref[:, cs].astype(BF16)))
    n = k_ref.shape[0]
    step = min(n, KV_CHUNK)
    for c0 in range(0, n, step):
        rs = slice(c0, c0 + step)
        chunks.append((lambda rs=rs: k_ref[rs, cs], lambda rs=rs: v_ref[rs, cs]))
    return chunks


def _attn_diff_kernel(*refs, hps, has_cache):
    if has_cache:
        q_ref, k_ref, v_ref, ck_ref, cv_ref, lam_ref, g_ref, o_ref = refs
    else:
        q_ref, k_ref, v_ref, lam_ref, g_ref, o_ref = refs
        ck_ref = cv_ref = None
    lv = lam_ref[...]
    lam = (jnp.exp(jnp.sum(lv[0:1] * lv[1:2], axis=-1, keepdims=True))
           - jnp.exp(jnp.sum(lv[2:3] * lv[3:4], axis=-1, keepdims=True)) + LAMBDA_INIT_L0)
    for j in range(hps):
        q = q_ref[:, j * LANES:(j + 1) * LANES]
        lane = lax.broadcasted_iota(I32, q.shape, 1)
        zero = jnp.zeros_like(q)
        chunks = _kv_chunks(k_ref, v_ref, ck_ref, cv_ref, j)
        acc1, l1 = _flash(jnp.where(lane < DIFF_QK_DIM, q, zero), chunks)
        acc2, l2 = _flash(jnp.where(lane >= DIFF_QK_DIM, q, zero), chunks)
        o = acc1 / l1 - lam * (acc2 / l2)
        o = (_rms(o) * g_ref[...]) * (1.0 - LAMBDA_INIT_L0)
        o_ref[:, j * LANES:(j + 1) * LANES] = o.astype(o_ref.dtype)


def _attn_diff_call(qkv, ck, cv, lamv, subg, *, row0, nb, n, tq, hps):
    w = hps * LANES
    hb_n = N_DIFF_HEADS // hps
    qpb = n // tq
    in_specs = [pl.BlockSpec((tq, w), lambda b, h, qi: (row0 // tq + b * qpb + qi, DQ_BLK // hps + h)),
                pl.BlockSpec((n, w), lambda b, h, qi: (row0 // n + b, DK_BLK // hps + h)),
                pl.BlockSpec((n, w), lambda b, h, qi: (row0 // n + b, DV_BLK // hps + h))]
    args = [qkv, qkv, qkv]
    if ck is not None:
        past = ck.shape[0] // nb
        in_specs += [pl.BlockSpec((past, w), lambda b, h, qi: (b, h)),
                     pl.BlockSpec((past, w), lambda b, h, qi: (b, h))]
        args += [ck, cv]
    in_specs += [pl.BlockSpec(lamv.shape, lambda b, h, qi: (0, 0)),
                 pl.BlockSpec((1, LANES), lambda b, h, qi: (0, 0))]
    args += [lamv, subg]
    return pl.pallas_call(
        functools.partial(_attn_diff_kernel, hps=hps, has_cache=ck is not None),
        name="attn_diff_lat" if ck is not None else "attn_diff_ctx",
        grid=(nb, hb_n, qpb),
        in_specs=in_specs,
        out_specs=pl.BlockSpec((tq, w), lambda b, h, qi: (b * qpb + qi, h)),
        out_shape=jax.ShapeDtypeStruct((nb * n, N_DIFF_HEADS * LANES), BF16),
        compiler_params=_params(3),
    )(*args)


def _attn_gqa_kernel(*refs, has_cache):
    if has_cache:
        q_ref, k_ref, v_ref, ck_ref, cv_ref, o_ref = refs
    else:
        q_ref, k_ref, v_ref, o_ref = refs
        ck_ref = cv_ref = None
    tq = q_ref.shape[0]
    q = jnp.concatenate([q_ref[:, r * LANES:(r + 1) * LANES] for r in range(GQA_GROUP)], axis=0)
    acc, l = _flash(q, _kv_chunks(k_ref, v_ref, ck_ref, cv_ref, 0))
    o = (acc / l).astype(o_ref.dtype)
    for r in range(GQA_GROUP):
        o_ref[:, r * LANES:(r + 1) * LANES] = o[r * tq:(r + 1) * tq]


def _attn_gqa_call(qkv, ck, cv, *, row0, nb, n, tq):
    w = GQA_GROUP * LANES
    qpb = n // tq
    in_specs = [pl.BlockSpec((tq, w), lambda b, g, qi: (row0 // tq + b * qpb + qi, GQ_BLK // GQA_GROUP + g)),
                pl.BlockSpec((n, LANES), lambda b, g, qi: (row0 // n + b, GK_BLK + g)),
                pl.BlockSpec((n, LANES), lambda b, g, qi: (row0 // n + b, GV_BLK + g))]
    args = [qkv, qkv, qkv]
    if ck is not None:
        past = ck.shape[0] // nb
        in_specs += [pl.BlockSpec((past, LANES), lambda b, g, qi: (b, g)),
                     pl.BlockSpec((past, LANES), lambda b, g, qi: (b, g))]
        args += [ck, cv]
    return pl.pallas_call(
        functools.partial(_attn_gqa_kernel, has_cache=ck is not None),
        name="attn_gqa_lat" if ck is not None else "attn_gqa_ctx",
        grid=(nb, N_GQA_KV_HEADS, qpb),
        in_specs=in_specs,
        out_specs=pl.BlockSpec((tq, w), lambda b, g, qi: (b * qpb + qi, g)),
        out_shape=jax.ShapeDtypeStruct((nb * n, N_GQA_HEADS * LANES), BF16),
        compiler_params=_params(3),
    )(*args)


def _outproj_kernel(odl_ref, ogl_ref, odc_ref, ogc_ref, xs_ref, xp_ref, mod_ref, n2g_ref, w_ref, wr3_ref,
                    x1_ref, h2_ref, aff_ref, *, n_lat_tiles):
    i = pl.program_id(0)
    is_ctx = i >= n_lat_tiles
    od = jnp.where(is_ctx, odc_ref[...], odl_ref[...])
    og = jnp.where(is_ctx, ogc_ref[...], ogl_ref[...])
    x = jnp.where(is_ctx, xp_ref[...], xs_ref[...])
    dw = od.shape[1]
    a = (jnp.dot(od, w_ref[0:dw, :], preferred_element_type=F32)
         + jnp.dot(og, w_ref[dw:, :], preferred_element_type=F32))
    g1 = mod_ref[2:3, :]
    sh2 = mod_ref[3:4, :]
    sc2 = mod_ref[4:5, :]
    x1 = x + g1 * a
    x1_ref[...] = x1
    h2 = (_rms(x1) * n2g_ref[...]) * (1.0 + sc2) + sh2
    h_hi = h2.astype(BF16)
    h2_ref[...] = h_hi
    ne = aff_ref.shape[1]
    rem = h2 - h_hi.astype(F32)
    h_mid = rem.astype(BF16)
    h_lo = (rem - h_mid.astype(F32)).astype(BF16)
    w3 = wr3_ref[...]
    p_hi = jnp.dot(h_hi, w3, preferred_element_type=F32)
    p_mid = jnp.dot(h_mid, w3, preferred_element_type=F32)
    p_lo = jnp.dot(h_lo, w3, preferred_element_type=F32)
    logits = (((p_lo[:, 0:ne] + p_mid[:, ne:2 * ne] + p_hi[:, 2 * ne:3 * ne])
               + (p_mid[:, 0:ne] + p_hi[:, ne:2 * ne])) + p_hi[:, 0:ne])
    z = jnp.exp(logits - jnp.max(logits, axis=-1, keepdims=True))
    aff_ref[...] = z / jnp.sum(z, axis=-1, keepdims=True)


def _split3_bf16(w):
    hi = w.astype(BF16)
    rem = w - hi.astype(F32)
    mid = rem.astype(BF16)
    lo = (rem - mid.astype(F32)).astype(BF16)
    pad = jnp.zeros((w.shape[0], LANES - 3 * w.shape[1]), BF16)
    return jnp.concatenate([hi, mid, lo, pad], axis=1)


def _outproj_call(odl, ogl, odc, ogc, xs, xp, mod, n2g, w_out_b, wr3, ne, n_lat):
    tl, d = xs.shape
    tc = xp.shape[0]
    tm = TOK_TILE
    nl, nc = tl // tm, tc // tm
    tpb = n_lat // tm
    nbl = tl // n_lat
    aw = odl.shape[1]

    def lat_spec(w):
        return pl.BlockSpec((tm, w), lambda i: (jnp.minimum(i, nl - 1), 0))

    def ctx_spec(w):
        return pl.BlockSpec((tm, w), lambda i: (jnp.maximum(i - nl, 0), 0))

    return pl.pallas_call(
        functools.partial(_outproj_kernel, n_lat_tiles=nl), name="outproj",
        grid=(nl + nc,),
        in_specs=[lat_spec(aw), lat_spec(aw), ctx_spec(aw), ctx_spec(aw), lat_spec(d), ctx_spec(d),
                  pl.BlockSpec((None, N_MOD, d), lambda i: (jnp.where(i < nl, i // tpb, nbl), 0, 0)),
                  pl.BlockSpec((1, d), lambda i: (0, 0)),
                  pl.BlockSpec(w_out_b.shape, lambda i: (0, 0), pipeline_mode=pl.Buffered(1)),
                  pl.BlockSpec((d, LANES), lambda i: (0, 0))],
        out_specs=[pl.BlockSpec((tm, d), lambda i: (i, 0)),
                   pl.BlockSpec((tm, d), lambda i: (i, 0)),
                   pl.BlockSpec((tm, ne), lambda i: (i, 0))],
        out_shape=[jax.ShapeDtypeStruct((tl + tc, d), F32),
                   jax.ShapeDtypeStruct((tl + tc, d), BF16),
                   jax.ShapeDtypeStruct((tl + tc, ne), F32)],
        compiler_params=_params(1),
    )(odl, ogl, odc, ogc, xs, xp, mod, n2g, w_out_b, wr3)


def _route_kernel(aff_ref, slot_ref, starts_ref, *, cap, n, offset_stride):
    ne = aff_ref.shape[0]
    n_sets = aff_ref.shape[1] // n
    tt = TOK_TILE
    aff = jnp.concatenate([aff_ref[:, s * n:(s + 1) * n] for s in range(n_sets)], axis=0)
    capf = float(cap)
    ne_all = n_sets * ne

    def count(mask):
        return jnp.sum(mask.astype(F32), axis=1, keepdims=True)

    def bisect(_, carry):
        lo, hi = carry
        mid = 0.5 * (lo + hi)
        ok = count(aff >= mid) >= capf
        return jnp.where(ok, mid, lo), jnp.where(ok, hi, mid)

    thr, _ = lax.fori_loop(0, BISECT_STEPS, bisect,
                           (jnp.zeros((ne_all, 1), F32), jnp.full((ne_all, 1), 2.0, F32)))
    gt = aff > thr
    tie = aff == thr
    need = capf - count(gt)

    ri = lax.broadcasted_iota(I32, (tt, tt), 0)
    ci = lax.broadcasted_iota(I32, (tt, tt), 1)
    upper = (ri < ci).astype(BF16)
    set_id = lax.broadcasted_iota(I32, (ne_all, 1), 0) // ne
    tie_pref = jnp.zeros((ne_all, 1), F32)
    sel_pref = (set_id * offset_stride).astype(F32)
    lane = lax.broadcasted_iota(I32, (ne_all, LANES), 1)
    starts = jnp.zeros((ne_all, LANES), F32)
    for t in range(n // tt):
        cs = slice(t * tt, (t + 1) * tt)
        tie_t = tie[:, cs]
        tie_rank = tie_pref + jnp.dot(tie_t.astype(BF16), upper, preferred_element_type=F32)
        sel_t = gt[:, cs] | (tie_t & (tie_rank < need))
        slot_t = sel_pref + jnp.dot(sel_t.astype(BF16), upper, preferred_element_type=F32)
        slot_t = jnp.where(sel_t, slot_t, -1.0).astype(I32)
        for s in range(n_sets):
            slot_ref[:, s * n + t * tt:s * n + (t + 1) * tt] = slot_t[s * ne:(s + 1) * ne]
        starts = jnp.where(lane == t, sel_pref, starts)
        tie_pref = tie_pref + count(tie_t)
        sel_pref = sel_pref + count(sel_t)
    starts = jnp.where(lane == n // tt, sel_pref, starts).astype(I32)
    for s in range(n_sets):
        starts_ref[s] = starts[s * ne:(s + 1) * ne]


def _route_call(aff_t, *, col0, n_sets, n, offset_stride, name):
    ne = aff_t.shape[0]
    cap = CAPACITY_FACTOR * n // N_EXPERTS
    w = n_sets * n
    return pl.pallas_call(
        functools.partial(_route_kernel, cap=cap, n=n, offset_stride=offset_stride), name=name,
        grid=(1,),
        in_specs=[pl.BlockSpec((ne, w), lambda s: (0, col0 // w))],
        out_specs=[pl.BlockSpec((ne, w), lambda s: (0, 0)),
                   pl.BlockSpec((n_sets, ne, LANES), lambda s: (0, 0, 0))],
        out_shape=[jax.ShapeDtypeStruct((ne, w), I32),
                   jax.ShapeDtypeStruct((n_sets, ne, LANES), I32)],
        compiler_params=_params(1),
    )(aff_t)


def _window_start(first_slot, rows):
    return pl.multiple_of(jnp.minimum((first_slot // 16) * 16, rows - DISPATCH_WIN), 16)


def _dispatch_kernel(tab_ref, h2_ref, slot_ref, gate_ref, x_ref, gc_ref, *, tiles_per_group, ne):
    g, eb, t = pl.program_id(0), pl.program_id(1), pl.program_id(2)
    epb, rows = x_ref.shape[0], x_ref.shape[1]
    tt = TOK_TILE
    win = DISPATCH_WIN

    @pl.when(t == 0)
    def _():
        x_ref[...] = jnp.zeros_like(x_ref)
        gc_ref[...] = jnp.zeros_like(gc_ref)

    sub = lax.broadcasted_iota(I32, (win, tt), 0)
    h_t = h2_ref[...]

    def scatter(i, start, hit):
        gwin = jnp.sum(jnp.where(hit, gate_ref[i:i + 1, :], 0.0), axis=1, keepdims=True)
        gc_ref[i, pl.ds(start, win), :] += jnp.broadcast_to(gwin, (win, LANES))

    starts, hits = [], []
    for i in range(epb):
        base = (g * ne + eb * epb + i) * (tiles_per_group + 1) + t
        start = _window_start(tab_ref[base], rows)
        starts.append(start)
        hits.append(slot_ref[i:i + 1, :] - start == sub)
    onehot = jnp.concatenate([h.astype(BF16) for h in hits], axis=0)
    moved = jnp.dot(onehot, h_t, preferred_element_type=F32).astype(BF16)
    for i in range(epb):
        x_ref[i, pl.ds(starts[i], win), :] += moved[i * win:(i + 1) * win]
        scatter(i, starts[i], hits[i])

        base = (g * ne + eb * epb + i) * (tiles_per_group + 1) + t
        last = tab_ref[base + 1]

        def extra(w, c, i=i, first=starts[i]):
            lo = first + w * win
            start = _window_start(lo, rows)
            srow = slot_ref[i:i + 1, :]
            hit = (srow - start == sub) & (srow >= lo)
            x_ref[i, pl.ds(start, win), :] += jnp.dot(hit.astype(BF16), h_t,
                                                      preferred_element_type=F32).astype(BF16)
            scatter(i, start, hit)
            return c

        lax.fori_loop(1, (last - starts[i] + win - 1) // win, extra, 0)


def _dispatch_call(tab, h2, slot_t, gate_t, *, ng, rows, tiles_per_group):
    t_all, d = h2.shape
    ne = slot_t.shape[0]
    epb = 8
    tt = TOK_TILE

    def tok_spec(w):
        return pl.BlockSpec((w, tt), lambda g, eb, t, tab: (eb, g * tiles_per_group + t))

    grid_spec = pltpu.PrefetchScalarGridSpec(
        num_scalar_prefetch=1,
        grid=(ng, ne // epb, tiles_per_group),
        in_specs=[pl.BlockSpec((tt, d), lambda g, eb, t, tab: (g * tiles_per_group + t, 0)),
                  tok_spec(epb), tok_spec(epb)],
        out_specs=[pl.BlockSpec((None, epb, rows, d), lambda g, eb, t, tab: (g, eb, 0, 0)),
                   pl.BlockSpec((None, epb, rows, LANES), lambda g, eb, t, tab: (g, eb, 0, 0))])
    return pl.pallas_call(
        functools.partial(_dispatch_kernel, tiles_per_group=tiles_per_group, ne=ne), name="dispatch",
        grid_spec=grid_spec,
        out_shape=[jax.ShapeDtypeStruct((ng, ne, rows, d), BF16),
                   jax.ShapeDtypeStruct((ng, ne, rows, LANES), F32)],
        compiler_params=_params(3),
    )(tab, h2, slot_t, gate_t)


def _moe_kernel(x_ref, gc_ref, wg_ref, wu_ref, wd_ref, y_ref, yacc_ref):
    f = pl.program_id(1)
    ng = x_ref.shape[0]
    tf = wg_ref.shape[1]

    @pl.when(f == 0)
    def _():
        yacc_ref[...] = jnp.zeros_like(yacc_ref)

    w_gu = jnp.concatenate([wg_ref[...].astype(BF16), wu_ref[...].astype(BF16)], axis=1)
    wd = wd_ref[...].astype(BF16)
    for g in range(ng):
        gu = jnp.dot(x_ref[g], w_gu, preferred_element_type=F32)
        gate, up = gu[:, :tf], gu[:, tf:]
        hid = (gate * _sigmoid(gate) * up).astype(BF16)
        yacc_ref[g] += jnp.dot(hid, wd, preferred_element_type=F32)

    @pl.when(f == pl.num_programs(1) - 1)
    def _():
        for g in range(ng):
            y_ref[g] = (yacc_ref[g] * gc_ref[g][:, 0:1]).astype(y_ref.dtype)


def _moe_call(x, gc, w_gate, w_up, w_down):
    ng, ne, rows, d = x.shape
    fdim = w_gate.shape[2]
    tf = min(fdim, 256)
    return pl.pallas_call(
        _moe_kernel, name="moe",
        grid=(ne, fdim // tf),
        in_specs=[pl.BlockSpec((ng, None, rows, d), lambda e, f: (0, e, 0, 0), pipeline_mode=pl.Buffered(1)),
                  pl.BlockSpec((ng, None, rows, LANES), lambda e, f: (0, e, 0, 0)),
                  pl.BlockSpec((None, d, tf), lambda e, f: (e, 0, f)),
                  pl.BlockSpec((None, d, tf), lambda e, f: (e, 0, f)),
                  pl.BlockSpec((None, tf, d), lambda e, f: (e, f, 0))],
        out_specs=pl.BlockSpec((ng, None, rows, d), lambda e, f: (0, e, 0, 0)),
        out_shape=jax.ShapeDtypeStruct((ng, ne, rows, d), BF16),
        scratch_shapes=[pltpu.VMEM((ng, rows, d), F32)],
        compiler_params=_params(2),
    )(x, gc, w_gate, w_up, w_down)


def _combine_kernel(tab_ref, x1_ref, mod_ref, slot_ref, fng_ref, y_hbm, ys_ref, yp_ref,
                    ybuf, obuf, acc_ref, sem, osem, *, tiles_per_group, n_lat_tiles, rows):
    i = pl.program_id(0)
    n_tiles = pl.num_programs(0)
    ne = sem.shape[1]
    win = DISPATCH_WIN
    tt = TOK_TILE
    per = LANES // win

    def first_slot(tile, e):
        return tab_ref[((tile // tiles_per_group) * ne + e) * (tiles_per_group + 1) + tile % tiles_per_group]

    def window_copies(tile, buf):
        g = tile // tiles_per_group
        return [pltpu.make_async_copy(y_hbm.at[g, e, pl.ds(_window_start(first_slot(tile, e), rows), win), :],
                                      ybuf.at[buf, pl.ds(e * win, win), :], sem.at[buf, e])
                for e in range(ne)]

    buf = i % 2

    @pl.when(i == 0)
    def _():
        for cp in window_copies(i, buf):
            cp.start()

    @pl.when(i + 1 < n_tiles)
    def _():
        for cp in window_copies(i + 1, 1 - buf):
            cp.start()

    starts = [_window_start(first_slot(i, e), rows) for e in range(ne)]
    lane = lax.broadcasted_iota(I32, (tt, LANES), 1)
    blocks = []
    for k in range(ne // per):
        slot = slot_ref[:, k * per:k * per + 1]
        start = starts[k * per]
        for j in range(1, per):
            pick = lane >= j * win
            slot = jnp.where(pick, slot_ref[:, k * per + j:k * per + j + 1], slot)
            start = jnp.where(pick, starts[k * per + j], start)
        blocks.append((slot - start == lane % win).astype(BF16))
    onehot = jnp.concatenate(blocks, axis=1)

    for cp in window_copies(i, buf):
        cp.wait()
    acc_ref[...] = jnp.dot(onehot, ybuf[buf], preferred_element_type=F32)

    lane_w = lax.broadcasted_iota(I32, (tt, win), 1)
    g = i // tiles_per_group
    for e in range(ne):
        last = tab_ref[(g * ne + e) * (tiles_per_group + 1) + i % tiles_per_group + 1]

        def extra(w, c, e=e, first=starts[e]):
            lo = first + w * win
            start = _window_start(lo, rows)
            cp = pltpu.make_async_copy(y_hbm.at[g, e, pl.ds(start, win), :], obuf, osem.at[0])
            cp.start()
            cp.wait()
            slot = slot_ref[:, e:e + 1]
            hit = ((slot - start == lane_w) & (slot >= lo)).astype(BF16)
            acc_ref[...] += jnp.dot(hit, obuf[...], preferred_element_type=F32)
            return c

        lax.fori_loop(1, (last - starts[e] + win - 1) // win, extra, 0)

    g2 = mod_ref[5:6, :]
    x2 = x1_ref[...] + g2 * acc_ref[...]
    y = _rms(x2) * fng_ref[...]

    @pl.when(i < n_lat_tiles)
    def _():
        ys_ref[...] = y

    @pl.when(i >= n_lat_tiles)
    def _():
        yp_ref[...] = y


def _combine_call(tab, x1, mod, slot_n, fng, y, *, n_lat_tokens, n_ctx_tokens, tiles_per_group):
    t_all, d = x1.shape
    tm = TOK_TILE
    nl = n_lat_tokens // tm
    ne = slot_n.shape[1]
    rows = y.shape[2]
    assert LANES % DISPATCH_WIN == 0 and ne % (LANES // DISPATCH_WIN) == 0
    grid_spec = pltpu.PrefetchScalarGridSpec(
        num_scalar_prefetch=1,
        grid=(t_all // tm,),
        in_specs=[pl.BlockSpec((tm, d), lambda i, tab: (i, 0)),
                  pl.BlockSpec((None, N_MOD, d), lambda i, tab: (i // tiles_per_group, 0, 0)),
                  pl.BlockSpec((tm, ne), lambda i, tab: (i, 0)),
                  pl.BlockSpec((1, d), lambda i, tab: (0, 0)),
                  pl.BlockSpec(memory_space=pl.ANY)],
        out_specs=[pl.BlockSpec((tm, d), lambda i, tab: (jnp.minimum(i, nl - 1), 0)),
                   pl.BlockSpec((tm, d), lambda i, tab: (jnp.maximum(i - nl, 0), 0))],
        scratch_shapes=[pltpu.VMEM((2, ne * DISPATCH_WIN, d), BF16),
                        pltpu.VMEM((DISPATCH_WIN, d), BF16),
                        pltpu.VMEM((tm, d), F32),
                        pltpu.SemaphoreType.DMA((2, ne)),
                        pltpu.SemaphoreType.DMA((1,))])
    return pl.pallas_call(
        functools.partial(_combine_kernel, tiles_per_group=tiles_per_group, n_lat_tiles=nl, rows=rows),
        name="combine",
        grid_spec=grid_spec,
        out_shape=[jax.ShapeDtypeStruct((n_lat_tokens, d), F32),
                   jax.ShapeDtypeStruct((n_ctx_tokens, d), F32)],
        compiler_params=_params(1),
    )(tab, x1, mod, slot_n, fng, y)


def kernel(x_prompt, x_sample, c, cache_diff_k, cache_diff_v, cache_gqa_k, cache_gqa_v, c_ctx, norm1_g, norm2_g, w_mod, b_mod, w_in, w_out, diff_lambda_q1, diff_lambda_k1, diff_lambda_q2, diff_lambda_k2, diff_subln_g, gqa_q_norm_g, gqa_k_norm_g, w_router, w_expert_gate, w_expert_up, w_expert_down, final_norm_g):
    nbc, seq, d = x_prompt.shape
    nbl, n_lat, _ = x_sample.shape
    depth, past = cache_diff_k.shape[1], cache_diff_k.shape[2]
    tc, tl = nbc * seq, nbl * n_lat
    assert depth == 1, "single trunk layer"
    assert tc == n_lat, "the context tokens must fill exactly one routing group"
    assert seq == TOK_TILE and n_lat % KV_CHUNK == 0 and nbl + 1 <= 8
    ne = N_EXPERTS
    tpg = n_lat // TOK_TILE
    ng = nbl + 1
    rows = CAPACITY_FACTOR * n_lat // ne

    xs = x_sample.reshape(tl, d)
    xp = x_prompt.reshape(tc, d)

    cond = jnp.concatenate([c, c_ctx[None, :], jnp.zeros((8 - ng, d), F32)], axis=0)
    mod = _mod_call(cond, w_mod[0], b_mod).reshape(8, N_MOD, d)

    tables = (_rope_tables(n_lat, DIFF_QK_DIM, TOK_TILE) + _rope_tables(n_lat, HEAD_DIM, TOK_TILE))
    qkv, ndk, ndv, ngk, ngv = _inproj_call(
        xs, xp, mod, norm1_g, w_in[0].astype(BF16), gqa_q_norm_g, gqa_k_norm_g, tables, n_lat)

    lamv = jnp.concatenate([diff_lambda_q1, diff_lambda_k1, diff_lambda_q2, diff_lambda_k2], axis=0)
    ckd = cache_diff_k.reshape(nbl * past, N_DIFF_HEADS * LANES)
    cvd = cache_diff_v.reshape(nbl * past, N_DIFF_HEADS * LANES)
    ckg = cache_gqa_k.reshape(nbl * past, N_GQA_KV_HEADS * LANES)
    cvg = cache_gqa_v.reshape(nbl * past, N_GQA_KV_HEADS * LANES)
    odl = _attn_diff_call(qkv, ckd, cvd, lamv, diff_subln_g, row0=0, nb=nbl, n=n_lat, tq=256, hps=1)
    ogl = _attn_gqa_call(qkv, ckg, cvg, row0=0, nb=nbl, n=n_lat, tq=128)
    odc = _attn_diff_call(qkv, None, None, lamv, diff_subln_g, row0=tl, nb=nbc, n=seq, tq=seq,
                          hps=N_DIFF_HEADS)
    ogc = _attn_gqa_call(qkv, None, None, row0=tl, nb=nbc, n=seq, tq=seq)

    x1, h2, aff = _outproj_call(odl, ogl, odc, ogc, xs, xp, mod, norm2_g, w_out[0].astype(BF16),
                                _split3_bf16(w_router[0]), ne, n_lat)
    aff_t = aff.T

    slot_l, starts_l = _route_call(aff_t, col0=0, n_sets=nbl, n=n_lat, offset_stride=0, name="route_lat")
    slot_c, starts_c = _route_call(aff_t, col0=tl, n_sets=nbc, n=seq,
                                   offset_stride=CAPACITY_FACTOR * seq // ne, name="route_ctx")
    slot_t = jnp.concatenate([slot_l, slot_c], axis=1)
    tab_l = starts_l[:, :, :tpg + 1]
    tab_c = jnp.concatenate([starts_c[:, :, 0].T, starts_c[-1:, :, 1].T], axis=1)[None]
    tab = jnp.concatenate([tab_l, tab_c], axis=0).reshape(-1)

    x_slots, gate_slots = _dispatch_call(tab, h2, slot_t, aff_t, ng=ng, rows=rows, tiles_per_group=tpg)
    y = _moe_call(x_slots, gate_slots, w_expert_gate[0], w_expert_up[0], w_expert_down[0])

    ys, yp = _combine_call(tab, x1, mod, slot_t.T, final_norm_g[None, :], y,
                           n_lat_tokens=tl, n_ctx_tokens=tc, tiles_per_group=tpg)

    return (yp.reshape(nbc, seq, d), ys.reshape(nbl, n_lat, d),
            ndk.reshape(nbc, 1, seq, N_DIFF_HEADS, 2, DIFF_QK_DIM),
            ndv.reshape(nbc, 1, seq, N_DIFF_HEADS, HEAD_DIM),
            ngk.reshape(nbc, 1, seq, N_GQA_KV_HEADS, HEAD_DIM),
            ngv.reshape(nbc, 1, seq, N_GQA_KV_HEADS, HEAD_DIM))
```

```python
import functools
import math

import jax
import jax.numpy as jnp
from jax import lax
from jax.experimental import pallas as pl
from jax.experimental.pallas import tpu as pltpu

F32 = jnp.float32
BF16 = jnp.bfloat16
I32 = jnp.int32

LANES = 128
HEAD_DIM = 128
N_DIFF_HEADS = 8
DIFF_QK_DIM = HEAD_DIM // 2
N_GQA_HEADS = 8
N_GQA_KV_HEADS = 2
GQA_GROUP = N_GQA_HEADS // N_GQA_KV_HEADS
N_EXPERTS = 16
CAPACITY_FACTOR = 2
GRID_W = 64
ROPE_THETA = 10000.0
NORM_EPS = 1e-6
N_MOD = 6
LAMBDA_INIT_L0 = 0.8 - 0.6 * math.exp(-0.3 * 0)

DQ_BLK = 0
DK_BLK = DQ_BLK + N_DIFF_HEADS
DV_BLK = DK_BLK + N_DIFF_HEADS
GQ_BLK = DV_BLK + N_DIFF_HEADS
GK_BLK = GQ_BLK + N_GQA_HEADS
GV_BLK = GK_BLK + N_GQA_KV_HEADS
IN_BLKS = GV_BLK + N_GQA_KV_HEADS

TOK_TILE = 256
OUTPROJ_TILE = 512
KV_CHUNK_MAX = 1536
DISPATCH_WIN = 64
NEG_BIG = -1e30
LOG2_E = math.log2(math.e)
BISECT_STEPS = 152

VMEM_LIMIT_MB = 56


def _params(n_axes, vmem_mb=VMEM_LIMIT_MB):
    return pltpu.CompilerParams(dimension_semantics=("arbitrary",) * n_axes,
                                vmem_limit_bytes=vmem_mb << 20)


def _sigmoid(x):
    return 1.0 / (1.0 + jnp.exp(-x))


def _rms(x, eps=NORM_EPS):
    return x * lax.rsqrt(jnp.mean(x * x, axis=-1, keepdims=True) + eps)


def _mod_kernel(c_ref, w_ref, b_ref, o_ref):
    c = c_ref[...]
    a = (c * _sigmoid(c)).astype(BF16)
    o_ref[...] = jnp.dot(a, w_ref[...].astype(BF16), preferred_element_type=F32) + b_ref[...]


def _mod_call(cond, w_mod, b_mod):
    d, m = w_mod.shape
    tn = next(t for t in (1024, 512, 256, LANES) if m % t == 0)
    return pl.pallas_call(
        _mod_kernel, name="mod",
        grid=(m // tn,),
        in_specs=[pl.BlockSpec((8, d), lambda j: (0, 0)),
                  pl.BlockSpec((d, tn), lambda j: (0, j)),
                  pl.BlockSpec((1, tn), lambda j: (0, j))],
        out_specs=pl.BlockSpec((8, tn), lambda j: (0, j)),
        out_shape=jax.ShapeDtypeStruct((8, m), F32),
        compiler_params=_params(1),
    )(cond, w_mod, b_mod)


def _rope(x, cos, sin_signed, half):
    lane = lax.broadcasted_iota(I32, x.shape, 1)
    first = (lane % (2 * half)) < half
    rot = jnp.where(first, pltpu.roll(x, LANES - half, 1), pltpu.roll(x, half, 1))
    return x * cos + rot * sin_signed


def _inproj_kernel(xs_ref, xp_ref, mod_ref, n1g_ref, w_ref, qn_ref, kn_ref,
                   cd_ref, sd_ref, cg_ref, sg_ref,
                   qkv_ref, ndk_ref, ndv_ref, ngk_ref, ngv_ref, *, n_lat_tiles):
    i = pl.program_id(0)
    is_ctx = i >= n_lat_tiles
    x = jnp.where(is_ctx, xp_ref[...], xs_ref[...])
    sh1 = mod_ref[0:1, :]
    sc1 = mod_ref[1:2, :]
    h = (_rms(x) * n1g_ref[...]) * (1.0 + sc1) + sh1
    hb = h.astype(BF16)
    cd, sd, cg, sg = cd_ref[...], sd_ref[...], cg_ref[...], sg_ref[...]
    qn, kn = qn_ref[...], kn_ref[...]
    cache_stores = []

    def seg(blk0, nblk):
        return jnp.dot(hb, w_ref[:, blk0 * LANES:(blk0 + nblk) * LANES], preferred_element_type=F32)

    def put(blk, val):
        qkv_ref[:, blk * LANES:(blk + 1) * LANES] = val.astype(BF16)

    p = seg(DQ_BLK, N_DIFF_HEADS)
    for j in range(N_DIFF_HEADS):
        v = p[:, j * LANES:(j + 1) * LANES]
        put(DQ_BLK + j, _rope(v, cd, sd, DIFF_QK_DIM // 4) * (DIFF_QK_DIM ** -0.5 * LOG2_E))
    p = seg(DK_BLK, N_DIFF_HEADS)
    for j in range(N_DIFF_HEADS):
        v = p[:, j * LANES:(j + 1) * LANES]
        cache_stores.append((ndk_ref, j, v))
        put(DK_BLK + j, _rope(v, cd, sd, DIFF_QK_DIM // 4))
    p = seg(DV_BLK, N_DIFF_HEADS)
    for j in range(N_DIFF_HEADS):
        v = p[:, j * LANES:(j + 1) * LANES]
        cache_stores.append((ndv_ref, j, v))
        put(DV_BLK + j, v)
    p = seg(GQ_BLK, N_GQA_HEADS)
    for j in range(N_GQA_HEADS):
        v = _rms(p[:, j * LANES:(j + 1) * LANES]) * qn
        put(GQ_BLK + j, _rope(v, cg, sg, HEAD_DIM // 4) * (HEAD_DIM ** -0.5 * LOG2_E))
    p = seg(GK_BLK, 2 * N_GQA_KV_HEADS)
    for j in range(N_GQA_KV_HEADS):
        v = _rms(p[:, j * LANES:(j + 1) * LANES]) * kn
        cache_stores.append((ngk_ref, j, v))
        put(GK_BLK + j, _rope(v, cg, sg, HEAD_DIM // 4))
    for j in range(N_GQA_KV_HEADS):
        v = p[:, (N_GQA_KV_HEADS + j) * LANES:(N_GQA_KV_HEADS + j + 1) * LANES]
        cache_stores.append((ngv_ref, j, v))
        put(GV_BLK + j, v)

    @pl.when(is_ctx)
    def _():
        for ref, j, v in cache_stores:
            ref[:, j * LANES:(j + 1) * LANES] = v


def _inproj_call(xs, xp, mod, n1g, w_in_b, qn, kn, tables, n_lat):
    tl, d = xs.shape
    tc = xp.shape[0]
    tm = TOK_TILE
    nl, nc = tl // tm, tc // tm
    tpb = n_lat // tm
    nbl = tl // n_lat
    width = IN_BLKS * LANES

    def tab_spec():
        return pl.BlockSpec((tm, LANES), lambda i: (jnp.where(i < nl, i % tpb, tpb), 0))

    def cache_spec(w):
        return pl.BlockSpec((tm, w), lambda i: (jnp.maximum(i - nl, 0), 0))

    dw, gw = N_DIFF_HEADS * LANES, N_GQA_KV_HEADS * LANES
    return pl.pallas_call(
        functools.partial(_inproj_kernel, n_lat_tiles=nl), name="inproj",
        grid=(nl + nc,),
        in_specs=[pl.BlockSpec((tm, d), lambda i: (jnp.minimum(i, nl - 1), 0)),
                  pl.BlockSpec((tm, d), lambda i: (jnp.maximum(i - nl, 0), 0)),
                  pl.BlockSpec((None, N_MOD, d), lambda i: (jnp.where(i < nl, i // tpb, nbl), 0, 0)),
                  pl.BlockSpec((1, d), lambda i: (0, 0)),
                  pl.BlockSpec((d, width), lambda i: (0, 0), pipeline_mode=pl.Buffered(1)),
                  pl.BlockSpec((1, LANES), lambda i: (0, 0)),
                  pl.BlockSpec((1, LANES), lambda i: (0, 0)),
                  tab_spec(), tab_spec(), tab_spec(), tab_spec()],
        out_specs=[pl.BlockSpec((tm, width), lambda i: (i, 0)),
                   cache_spec(dw), cache_spec(dw), cache_spec(gw), cache_spec(gw)],
        out_shape=[jax.ShapeDtypeStruct((tl + tc, width), BF16),
                   jax.ShapeDtypeStruct((tc, dw), F32), jax.ShapeDtypeStruct((tc, dw), F32),
                   jax.ShapeDtypeStruct((tc, gw), F32), jax.ShapeDtypeStruct((tc, gw), F32)],
        compiler_params=_params(1),
    )(xs, xp, mod, n1g, w_in_b, qn, kn, *tables)


def _rope_tables(n, dim, pad_rows):
    t = jnp.arange(n, dtype=I32)
    row = (t // GRID_W).astype(F32)
    col = (t % GRID_W).astype(F32)
    sec = dim // 2
    inv = ROPE_THETA ** (-jnp.arange(0, sec, 2, dtype=F32) / sec)
    ang = jnp.stack([row[:, None] * inv, col[:, None] * inv], axis=1)
    ang = jnp.concatenate([ang, ang], axis=-1).reshape(n, dim)
    sign = jnp.where((jnp.arange(dim) % sec) < sec // 2, -1.0, 1.0).astype(F32)
    cos, sin = jnp.cos(ang), jnp.sin(ang) * sign
    reps = LANES // dim
    cos, sin = jnp.tile(cos, (1, reps)), jnp.tile(sin, (1, reps))
    cos = jnp.concatenate([cos, jnp.ones((pad_rows, LANES), F32)], axis=0)
    sin = jnp.concatenate([sin, jnp.zeros((pad_rows, LANES), F32)], axis=0)
    return cos, sin


def _flash(q, kx_ref, vx_ref):
    m_rows = q.shape[0]
    nk = kx_ref.shape[0]
    n_chunks = pl.cdiv(nk, KV_CHUNK_MAX)
    assert nk % n_chunks == 0
    step = nk // n_chunks
    m = jnp.full((m_rows, 1), NEG_BIG, F32)
    acc = jnp.zeros((m_rows, HEAD_DIM), F32)
    l = jnp.zeros((m_rows, 1), F32)
    for c in range(n_chunks):
        rs = slice(c * step, (c + 1) * step)
        s = lax.dot_general(q, kx_ref[rs, :], (((1,), (1,)), ((), ())), preferred_element_type=F32)
        m_new = jnp.maximum(m, jnp.max(s, axis=-1, keepdims=True))
        alpha = jnp.exp2(m - m_new)
        p = jnp.exp2(s - m_new)
        l = alpha * l + jnp.sum(p, axis=-1, keepdims=True)
        acc = alpha * acc + jnp.dot(p.astype(BF16), vx_ref[rs, :], preferred_element_type=F32)
        m = m_new
    return acc / l


def _stage_kv(kx_ref, vx_ref, k_ref, v_ref, ck_ref, cv_ref, col):
    cs = slice(col * LANES, (col + 1) * LANES)
    past = 0
    if ck_ref is not None:
        past = ck_ref.shape[0]
        kx_ref[0:past, :] = ck_ref[:, cs].astype(BF16)
        vx_ref[0:past, :] = cv_ref[:, cs].astype(BF16)
    kx_ref[past:, :] = k_ref[:, cs]
    vx_ref[past:, :] = v_ref[:, cs]


def _attn_diff_kernel(*refs, hps, has_cache):
    if has_cache:
        q_ref, k_ref, v_ref, ck_ref, cv_ref, lam_ref, g_ref, o_ref, kx_ref, vx_ref = refs
    else:
        q_ref, k_ref, v_ref, lam_ref, g_ref, o_ref, kx_ref, vx_ref = refs
        ck_ref = cv_ref = None

    @pl.when(pl.program_id(2) == 0)
    def _():
        for j in range(hps):
            _stage_kv(kx_ref.at[j], vx_ref.at[j], k_ref, v_ref, ck_ref, cv_ref, j)

    lv = lam_ref[...]
    lam = (jnp.exp(jnp.sum(lv[0:1] * lv[1:2], axis=-1, keepdims=True))
           - jnp.exp(jnp.sum(lv[2:3] * lv[3:4], axis=-1, keepdims=True)) + LAMBDA_INIT_L0)
    for j in range(hps):
        q = q_ref[:, j * LANES:(j + 1) * LANES]
        lane = lax.broadcasted_iota(I32, q.shape, 1)
        zero = jnp.zeros_like(q)
        o1 = _flash(jnp.where(lane < DIFF_QK_DIM, q, zero), kx_ref.at[j], vx_ref.at[j])
        o2 = _flash(jnp.where(lane >= DIFF_QK_DIM, q, zero), kx_ref.at[j], vx_ref.at[j])
        o = o1 - lam * o2
        o = (_rms(o) * g_ref[...]) * (1.0 - LAMBDA_INIT_L0)
        o_ref[:, j * LANES:(j + 1) * LANES] = o.astype(o_ref.dtype)


def _attn_diff_call(qkv, ck, cv, lamv, subg, *, row0, nb, n, tq, hps):
    w = hps * LANES
    hb_n = N_DIFF_HEADS // hps
    qpb = n // tq
    in_specs = [pl.BlockSpec((tq, w), lambda b, h, qi: (row0 // tq + b * qpb + qi, DQ_BLK // hps + h)),
                pl.BlockSpec((n, w), lambda b, h, qi: (row0 // n + b, DK_BLK // hps + h)),
                pl.BlockSpec((n, w), lambda b, h, qi: (row0 // n + b, DV_BLK // hps + h))]
    args = [qkv, qkv, qkv]
    if ck is not None:
        past = ck.shape[0] // nb
        in_specs += [pl.BlockSpec((past, w), lambda b, h, qi: (b, h)),
                     pl.BlockSpec((past, w), lambda b, h, qi: (b, h))]
        args += [ck, cv]
    in_specs += [pl.BlockSpec(lamv.shape, lambda b, h, qi: (0, 0)),
                 pl.BlockSpec((1, LANES), lambda b, h, qi: (0, 0))]
    args += [lamv, subg]
    nk = n + (0 if ck is None else ck.shape[0] // nb)
    return pl.pallas_call(
        functools.partial(_attn_diff_kernel, hps=hps, has_cache=ck is not None),
        name="attn_diff_lat" if ck is not None else "attn_diff_ctx",
        grid=(nb, hb_n, qpb),
        in_specs=in_specs,
        out_specs=pl.BlockSpec((tq, w), lambda b, h, qi: (b * qpb + qi, h)),
        out_shape=jax.ShapeDtypeStruct((nb * n, N_DIFF_HEADS * LANES), BF16),
        scratch_shapes=[pltpu.VMEM((hps, nk, HEAD_DIM), BF16), pltpu.VMEM((hps, nk, HEAD_DIM), BF16)],
        compiler_params=_params(3),
    )(*args)


def _attn_gqa_kernel(*refs, has_cache):
    if has_cache:
        q_ref, k_ref, v_ref, ck_ref, cv_ref, o_ref, kx_ref, vx_ref = refs
    else:
        q_ref, k_ref, v_ref, o_ref, kx_ref, vx_ref = refs
        ck_ref = cv_ref = None

    @pl.when(pl.program_id(2) == 0)
    def _():
        _stage_kv(kx_ref, vx_ref, k_ref, v_ref, ck_ref, cv_ref, 0)

    tq = q_ref.shape[0]
    q = jnp.concatenate([q_ref[:, r * LANES:(r + 1) * LANES] for r in range(GQA_GROUP)], axis=0)
    o = _flash(q, kx_ref, vx_ref).astype(o_ref.dtype)
    for r in range(GQA_GROUP):
        o_ref[:, r * LANES:(r + 1) * LANES] = o[r * tq:(r + 1) * tq]


def _attn_gqa_call(qkv, ck, cv, *, row0, nb, n, tq):
    w = GQA_GROUP * LANES
    qpb = n // tq
    in_specs = [pl.BlockSpec((tq, w), lambda b, g, qi: (row0 // tq + b * qpb + qi, GQ_BLK // GQA_GROUP + g)),
                pl.BlockSpec((n, LANES), lambda b, g, qi: (row0 // n + b, GK_BLK + g)),
                pl.BlockSpec((n, LANES), lambda b, g, qi: (row0 // n + b, GV_BLK + g))]
    args = [qkv, qkv, qkv]
    if ck is not None:
        past = ck.shape[0] // nb
        in_specs += [pl.BlockSpec((past, LANES), lambda b, g, qi: (b, g)),
                     pl.BlockSpec((past, LANES), lambda b, g, qi: (b, g))]
        args += [ck, cv]
    nk = n + (0 if ck is None else ck.shape[0] // nb)
    return pl.pallas_call(
        functools.partial(_attn_gqa_kernel, has_cache=ck is not None),
        name="attn_gqa_lat" if ck is not None else "attn_gqa_ctx",
        grid=(nb, N_GQA_KV_HEADS, qpb),
        in_specs=in_specs,
        out_specs=pl.BlockSpec((tq, w), lambda b, g, qi: (b * qpb + qi, g)),
        out_shape=jax.ShapeDtypeStruct((nb * n, N_GQA_HEADS * LANES), BF16),
        scratch_shapes=[pltpu.VMEM((nk, HEAD_DIM), BF16), pltpu.VMEM((nk, HEAD_DIM), BF16)],
        compiler_params=_params(3),
    )(*args)


def _outproj_kernel(odl_ref, ogl_ref, odc_ref, ogc_ref, xs_ref, xp_ref, mod_ref, n2g_ref, w_ref, wr3_ref,
                    x1_ref, h2_ref, aff_ref, *, n_lat_tiles):
    i = pl.program_id(0)
    is_ctx = i >= n_lat_tiles
    od = jnp.where(is_ctx, odc_ref[...], odl_ref[...])
    og = jnp.where(is_ctx, ogc_ref[...], ogl_ref[...])
    x = jnp.where(is_ctx, xp_ref[...], xs_ref[...])
    dw = od.shape[1]
    a = (jnp.dot(od, w_ref[0:dw, :], preferred_element_type=F32)
         + jnp.dot(og, w_ref[dw:, :], preferred_element_type=F32))
    g1 = mod_ref[2:3, :]
    sh2 = mod_ref[3:4, :]
    sc2 = mod_ref[4:5, :]
    x1 = x + g1 * a
    x1_ref[...] = x1
    h2 = (_rms(x1) * n2g_ref[...]) * (1.0 + sc2) + sh2
    h_hi = h2.astype(BF16)
    h2_ref[...] = h_hi
    ne = aff_ref.shape[1]
    rem = h2 - h_hi.astype(F32)
    h_mid = rem.astype(BF16)
    h_lo = (rem - h_mid.astype(F32)).astype(BF16)
    w3 = wr3_ref[...]
    p_hi = jnp.dot(h_hi, w3, preferred_element_type=F32)
    p_mid = jnp.dot(h_mid, w3, preferred_element_type=F32)
    p_lo = jnp.dot(h_lo, w3, preferred_element_type=F32)
    logits = (((p_lo[:, 0:ne] + p_mid[:, ne:2 * ne] + p_hi[:, 2 * ne:3 * ne])
               + (p_mid[:, 0:ne] + p_hi[:, ne:2 * ne])) + p_hi[:, 0:ne])
    z = jnp.exp(logits - jnp.max(logits, axis=-1, keepdims=True))
    aff_ref[...] = z / jnp.sum(z, axis=-1, keepdims=True)


def _split3_bf16(w):
    hi = w.astype(BF16)
    rem = w - hi.astype(F32)
    mid = rem.astype(BF16)
    lo = (rem - mid.astype(F32)).astype(BF16)
    pad = jnp.zeros((w.shape[0], LANES - 3 * w.shape[1]), BF16)
    return jnp.concatenate([hi, mid, lo, pad], axis=1)


def _outproj_call(odl, ogl, odc, ogc, xs, xp, mod, n2g, w_out_b, wr3, ne, n_lat):
    tl, d = xs.shape
    tc = xp.shape[0]
    tm = OUTPROJ_TILE
    nl, nc = tl // tm, tc // tm
    tpb = n_lat // tm
    nbl = tl // n_lat
    aw = odl.shape[1]

    def lat_spec(w):
        return pl.BlockSpec((tm, w), lambda i: (jnp.minimum(i, nl - 1), 0))

    def ctx_spec(w):
        return pl.BlockSpec((tm, w), lambda i: (jnp.maximum(i - nl, 0), 0))

    return pl.pallas_call(
        functools.partial(_outproj_kernel, n_lat_tiles=nl), name="outproj",
        grid=(nl + nc,),
        in_specs=[lat_spec(aw), lat_spec(aw), ctx_spec(aw), ctx_spec(aw), lat_spec(d), ctx_spec(d),
                  pl.BlockSpec((None, N_MOD, d), lambda i: (jnp.where(i < nl, i // tpb, nbl), 0, 0)),
                  pl.BlockSpec((1, d), lambda i: (0, 0)),
                  pl.BlockSpec(w_out_b.shape, lambda i: (0, 0), pipeline_mode=pl.Buffered(1)),
                  pl.BlockSpec((d, LANES), lambda i: (0, 0))],
        out_specs=[pl.BlockSpec((tm, d), lambda i: (i, 0)),
                   pl.BlockSpec((tm, d), lambda i: (i, 0)),
                   pl.BlockSpec((tm, ne), lambda i: (i, 0))],
        out_shape=[jax.ShapeDtypeStruct((tl + tc, d), F32),
                   jax.ShapeDtypeStruct((tl + tc, d), BF16),
                   jax.ShapeDtypeStruct((tl + tc, ne), F32)],
        compiler_params=_params(1),
    )(odl, ogl, odc, ogc, xs, xp, mod, n2g, w_out_b, wr3)


def _route_kernel(aff_ref, slot_ref, starts_ref, *, cap, n, offset_stride):
    ne = aff_ref.shape[0]
    n_sets = aff_ref.shape[1] // n
    tt = TOK_TILE
    aff = jnp.concatenate([aff_ref[:, s * n:(s + 1) * n] for s in range(n_sets)], axis=0)
    capf = float(cap)
    ne_all = n_sets * ne

    def count(mask):
        return jnp.sum(mask.astype(F32), axis=1, keepdims=True)

    def bisect(_, carry):
        lo, hi = carry
        mid = 0.5 * (lo + hi)
        ok = count(aff >= mid) >= capf
        return jnp.where(ok, mid, lo), jnp.where(ok, hi, mid)

    thr, _ = lax.fori_loop(0, BISECT_STEPS, bisect,
                           (jnp.zeros((ne_all, 1), F32), jnp.full((ne_all, 1), 2.0, F32)))
    gt = aff > thr
    tie = aff == thr
    need = capf - count(gt)

    ri = lax.broadcasted_iota(I32, (tt, tt), 0)
    ci = lax.broadcasted_iota(I32, (tt, tt), 1)
    upper = (ri < ci).astype(BF16)
    set_id = lax.broadcasted_iota(I32, (ne_all, 1), 0) // ne
    tie_pref = jnp.zeros((ne_all, 1), F32)
    sel_pref = (set_id * offset_stride).astype(F32)
    lane = lax.broadcasted_iota(I32, (ne_all, LANES), 1)
    starts = jnp.zeros((ne_all, LANES), F32)
    for t in range(n // tt):
        cs = slice(t * tt, (t + 1) * tt)
        tie_t = tie[:, cs]
        tie_rank = tie_pref + jnp.dot(tie_t.astype(BF16), upper, preferred_element_type=F32)
        sel_t = gt[:, cs] | (tie_t & (tie_rank < need))
        slot_t = sel_pref + jnp.dot(sel_t.astype(BF16), upper, preferred_element_type=F32)
        slot_t = jnp.where(sel_t, slot_t, -1.0).astype(I32)
        for s in range(n_sets):
            slot_ref[:, s * n + t * tt:s * n + (t + 1) * tt] = slot_t[s * ne:(s + 1) * ne]
        starts = jnp.where(lane == t, sel_pref, starts)
        tie_pref = tie_pref + count(tie_t)
        sel_pref = sel_pref + count(sel_t)
    starts = jnp.where(lane == n // tt, sel_pref, starts).astype(I32)
    for s in range(n_sets):
        starts_ref[s] = starts[s * ne:(s + 1) * ne]


def _route_call(aff_t, *, col0, n_sets, n, offset_stride, name):
    ne = aff_t.shape[0]
    cap = CAPACITY_FACTOR * n // N_EXPERTS
    w = n_sets * n
    return pl.pallas_call(
        functools.partial(_route_kernel, cap=cap, n=n, offset_stride=offset_stride), name=name,
        grid=(1,),
        in_specs=[pl.BlockSpec((ne, w), lambda s: (0, col0 // w))],
        out_specs=[pl.BlockSpec((ne, w), lambda s: (0, 0)),
                   pl.BlockSpec((n_sets, ne, LANES), lambda s: (0, 0, 0))],
        out_shape=[jax.ShapeDtypeStruct((ne, w), I32),
                   jax.ShapeDtypeStruct((n_sets, ne, LANES), I32)],
        compiler_params=_params(1),
    )(aff_t)


def _window_start(first_slot, rows):
    return pl.multiple_of(jnp.minimum((first_slot // 16) * 16, rows - DISPATCH_WIN), 16)


def _dispatch_kernel(tab_ref, h2_ref, slot_ref, gate_ref, x_ref, gc_ref, *, tiles_per_group, ne):
    g, eb, t = pl.program_id(0), pl.program_id(1), pl.program_id(2)
    epb, rows = x_ref.shape[0], x_ref.shape[1]
    tt = TOK_TILE
    win = DISPATCH_WIN

    @pl.when(t == 0)
    def _():
        x_ref[...] = jnp.zeros_like(x_ref)
        gc_ref[...] = jnp.zeros_like(gc_ref)

    sub = lax.broadcasted_iota(I32, (win, tt), 0)
    h_t = h2_ref[...]

    def scatter(i, start, hit):
        gwin = jnp.sum(jnp.where(hit, gate_ref[i:i + 1, :], 0.0), axis=1, keepdims=True)
        gc_ref[i, pl.ds(start, win), :] += jnp.broadcast_to(gwin, (win, LANES))

    starts, hits = [], []
    for i in range(epb):
        base = (g * ne + eb * epb + i) * (tiles_per_group + 1) + t
        start = _window_start(tab_ref[base], rows)
        starts.append(start)
        hits.append(slot_ref[i:i + 1, :] - start == sub)
    onehot = jnp.concatenate([h.astype(BF16) for h in hits], axis=0)
    moved = jnp.dot(onehot, h_t, preferred_element_type=F32).astype(BF16)
    for i in range(epb):
        x_ref[i, pl.ds(starts[i], win), :] += moved[i * win:(i + 1) * win]
        scatter(i, starts[i], hits[i])

        base = (g * ne + eb * epb + i) * (tiles_per_group + 1) + t
        last = tab_ref[base + 1]

        def extra(w, c, i=i, first=starts[i]):
            lo = first + w * win
            start = _window_start(lo, rows)
            srow = slot_ref[i:i + 1, :]
            hit = (srow - start == sub) & (srow >= lo)
            x_ref[i, pl.ds(start, win), :] += jnp.dot(hit.astype(BF16), h_t,
                                                      preferred_element_type=F32).astype(BF16)
            scatter(i, start, hit)
            return c

        lax.fori_loop(1, (last - starts[i] + win - 1) // win, extra, 0)


def _dispatch_call(tab, h2, slot_t, gate_t, *, ng, rows, tiles_per_group):
    t_all, d = h2.shape
    ne = slot_t.shape[0]
    epb = 8
    tt = TOK_TILE

    def tok_spec(w):
        return pl.BlockSpec((w, tt), lambda g, eb, t, tab: (eb, g * tiles_per_group + t))

    grid_spec = pltpu.PrefetchScalarGridSpec(
        num_scalar_prefetch=1,
        grid=(ng, ne // epb, tiles_per_group),
        in_specs=[pl.BlockSpec((tt, d), lambda g, eb, t, tab: (g * tiles_per_group + t, 0)),
                  tok_spec(epb), tok_spec(epb)],
        out_specs=[pl.BlockSpec((None, epb, rows, d), lambda g, eb, t, tab: (g, eb, 0, 0)),
                   pl.BlockSpec((None, epb, rows, LANES), lambda g, eb, t, tab: (g, eb, 0, 0))])
    return pl.pallas_call(
        functools.partial(_dispatch_kernel, tiles_per_group=tiles_per_group, ne=ne), name="dispatch",
        grid_spec=grid_spec,
        out_shape=[jax.ShapeDtypeStruct((ng, ne, rows, d), BF16),
                   jax.ShapeDtypeStruct((ng, ne, rows, LANES), F32)],
        compiler_params=_params(3),
    )(tab, h2, slot_t, gate_t)


def _moe_kernel(x_ref, gc_ref, wg_ref, wu_ref, wd_ref, y_ref, yacc_ref):
    f = pl.program_id(1)
    ng = x_ref.shape[0]
    tf = wg_ref.shape[1]

    @pl.when(f == 0)
    def _():
        yacc_ref[...] = jnp.zeros_like(yacc_ref)

    w_gu = jnp.concatenate([wg_ref[...].astype(BF16), wu_ref[...].astype(BF16)], axis=1)
    wd = wd_ref[...].astype(BF16)
    for g in range(ng):
        gu = jnp.dot(x_ref[g], w_gu, preferred_element_type=F32)
        gate, up = gu[:, :tf], gu[:, tf:]
        hid = (gate * _sigmoid(gate) * up).astype(BF16)
        yacc_ref[g] += jnp.dot(hid, wd, preferred_element_type=F32)

    @pl.when(f == pl.num_programs(1) - 1)
    def _():
        for g in range(ng):
            y_ref[g] = (yacc_ref[g] * gc_ref[g][:, 0:1]).astype(y_ref.dtype)


def _moe_call(x, gc, w_gate, w_up, w_down):
    ng, ne, rows, d = x.shape
    fdim = w_gate.shape[2]
    tf = min(fdim, 256)
    return pl.pallas_call(
        _moe_kernel, name="moe",
        grid=(ne, fdim // tf),
        in_specs=[pl.BlockSpec((ng, None, rows, d), lambda e, f: (0, e, 0, 0), pipeline_mode=pl.Buffered(1)),
                  pl.BlockSpec((ng, None, rows, LANES), lambda e, f: (0, e, 0, 0)),
                  pl.BlockSpec((None, d, tf), lambda e, f: (e, 0, f)),
                  pl.BlockSpec((None, d, tf), lambda e, f: (e, 0, f)),
                  pl.BlockSpec((None, tf, d), lambda e, f: (e, f, 0))],
        out_specs=pl.BlockSpec((ng, None, rows, d), lambda e, f: (0, e, 0, 0)),
        out_shape=jax.ShapeDtypeStruct((ng, ne, rows, d), BF16),
        scratch_shapes=[pltpu.VMEM((ng, rows, d), F32)],
        compiler_params=_params(2),
    )(x, gc, w_gate, w_up, w_down)


def _combine_kernel(tab_ref, x1_ref, mod_ref, slot_ref, fng_ref, y_hbm, ys_ref, yp_ref,
                    ybuf, obuf, acc_ref, sem, osem, *, tiles_per_group, n_lat_tiles, rows):
    i = pl.program_id(0)
    n_tiles = pl.num_programs(0)
    ne = sem.shape[1]
    win = DISPATCH_WIN
    tt = TOK_TILE
    per = LANES // win

    def first_slot(tile, e):
        return tab_ref[((tile // tiles_per_group) * ne + e) * (tiles_per_group + 1) + tile % tiles_per_group]

    def window_copies(tile, buf):
        g = tile // tiles_per_group
        return [pltpu.make_async_copy(y_hbm.at[g, e, pl.ds(_window_start(first_slot(tile, e), rows), win), :],
                                      ybuf.at[buf, pl.ds(e * win, win), :], sem.at[buf, e])
                for e in range(ne)]

    buf = i % 2

    @pl.when(i == 0)
    def _():
        for cp in window_copies(i, buf):
            cp.start()

    @pl.when(i + 1 < n_tiles)
    def _():
        for cp in window_copies(i + 1, 1 - buf):
            cp.start()

    starts = [_window_start(first_slot(i, e), rows) for e in range(ne)]
    lane = lax.broadcasted_iota(I32, (tt, LANES), 1)
    blocks = []
    for k in range(ne // per):
        slot = slot_ref[:, k * per:k * per + 1]
        start = starts[k * per]
        for j in range(1, per):
            pick = lane >= j * win
            slot = jnp.where(pick, slot_ref[:, k * per + j:k * per + j + 1], slot)
            start = jnp.where(pick, starts[k * per + j], start)
        blocks.append((slot - start == lane % win).astype(BF16))
    onehot = jnp.concatenate(blocks, axis=1)

    for cp in window_copies(i, buf):
        cp.wait()
    acc_ref[...] = jnp.dot(onehot, ybuf[buf], preferred_element_type=F32)

    lane_w = lax.broadcasted_iota(I32, (tt, win), 1)
    g = i // tiles_per_group
    for e in range(ne):
        last = tab_ref[(g * ne + e) * (tiles_per_group + 1) + i % tiles_per_group + 1]

        def extra(w, c, e=e, first=starts[e]):
            lo = first + w * win
            start = _window_start(lo, rows)
            cp = pltpu.make_async_copy(y_hbm.at[g, e, pl.ds(start, win), :], obuf, osem.at[0])
            cp.start()
            cp.wait()
            slot = slot_ref[:, e:e + 1]
            hit = ((slot - start == lane_w) & (slot >= lo)).astype(BF16)
            acc_ref[...] += jnp.dot(hit, obuf[...], preferred_element_type=F32)
            return c

        lax.fori_loop(1, (last - starts[e] + win - 1) // win, extra, 0)

    g2 = mod_ref[5:6, :]
    x2 = x1_ref[...] + g2 * acc_ref[...]
    y = _rms(x2) * fng_ref[...]

    @pl.when(i < n_lat_tiles)
    def _():
        ys_ref[...] = y

    @pl.when(i >= n_lat_tiles)
    def _():
        yp_ref[...] = y


def _combine_call(tab, x1, mod, slot_n, fng, y, *, n_lat_tokens, n_ctx_tokens, tiles_per_group):
    t_all, d = x1.shape
    tm = TOK_TILE
    nl = n_lat_tokens // tm
    ne = slot_n.shape[1]
    rows = y.shape[2]
    assert LANES % DISPATCH_WIN == 0 and ne % (LANES // DISPATCH_WIN) == 0
    grid_spec = pltpu.PrefetchScalarGridSpec(
        num_scalar_prefetch=1,
        grid=(t_all // tm,),
        in_specs=[pl.BlockSpec((tm, d), lambda i, tab: (i, 0)),
                  pl.BlockSpec((None, N_MOD, d), lambda i, tab: (i // tiles_per_group, 0, 0)),
                  pl.BlockSpec((tm, ne), lambda i, tab: (i, 0)),
                  pl.BlockSpec((1, d), lambda i, tab: (0, 0)),
                  pl.BlockSpec(memory_space=pl.ANY)],
        out_specs=[pl.BlockSpec((tm, d), lambda i, tab: (jnp.minimum(i, nl - 1), 0)),
                   pl.BlockSpec((tm, d), lambda i, tab: (jnp.maximum(i - nl, 0), 0))],
        scratch_shapes=[pltpu.VMEM((2, ne * DISPATCH_WIN, d), BF16),
                        pltpu.VMEM((DISPATCH_WIN, d), BF16),
                        pltpu.VMEM((tm, d), F32),
                        pltpu.SemaphoreType.DMA((2, ne)),
                        pltpu.SemaphoreType.DMA((1,))])
    return pl.pallas_call(
        functools.partial(_combine_kernel, tiles_per_group=tiles_per_group, n_lat_tiles=nl, rows=rows),
        name="combine",
        grid_spec=grid_spec,
        out_shape=[jax.ShapeDtypeStruct((n_lat_tokens, d), F32),
                   jax.ShapeDtypeStruct((n_ctx_tokens, d), F32)],
        compiler_params=_params(1),
    )(tab, x1, mod, slot_n, fng, y)


def kernel(x_prompt, x_sample, c, cache_diff_k, cache_diff_v, cache_gqa_k, cache_gqa_v, c_ctx, norm1_g, norm2_g, w_mod, b_mod, w_in, w_out, diff_lambda_q1, diff_lambda_k1, diff_lambda_q2, diff_lambda_k2, diff_subln_g, gqa_q_norm_g, gqa_k_norm_g, w_router, w_expert_gate, w_expert_up, w_expert_down, final_norm_g):
    nbc, seq, d = x_prompt.shape
    nbl, n_lat, _ = x_sample.shape
    depth, past = cache_diff_k.shape[1], cache_diff_k.shape[2]
    tc, tl = nbc * seq, nbl * n_lat
    assert depth == 1, "single trunk layer"
    assert tc == n_lat, "the context tokens must fill exactly one routing group"
    assert seq == TOK_TILE and nbl + 1 <= 8
    ne = N_EXPERTS
    tpg = n_lat // TOK_TILE
    ng = nbl + 1
    rows = CAPACITY_FACTOR * n_lat // ne

    xs = x_sample.reshape(tl, d)
    xp = x_prompt.reshape(tc, d)

    cond = jnp.concatenate([c, c_ctx[None, :], jnp.zeros((8 - ng, d), F32)], axis=0)
    mod = _mod_call(cond, w_mod[0], b_mod).reshape(8, N_MOD, d)

    tables = (_rope_tables(n_lat, DIFF_QK_DIM, TOK_TILE) + _rope_tables(n_lat, HEAD_DIM, TOK_TILE))
    qkv, ndk, ndv, ngk, ngv = _inproj_call(
        xs, xp, mod, norm1_g, w_in[0].astype(BF16), gqa_q_norm_g, gqa_k_norm_g, tables, n_lat)

    lamv = jnp.concatenate([diff_lambda_q1, diff_lambda_k1, diff_lambda_q2, diff_lambda_k2], axis=0)
    ckd = cache_diff_k.reshape(nbl * past, N_DIFF_HEADS * LANES)
    cvd = cache_diff_v.reshape(nbl * past, N_DIFF_HEADS * LANES)
    ckg = cache_gqa_k.reshape(nbl * past, N_GQA_KV_HEADS * LANES)
    cvg = cache_gqa_v.reshape(nbl * past, N_GQA_KV_HEADS * LANES)
    odl = _attn_diff_call(qkv, ckd, cvd, lamv, diff_subln_g, row0=0, nb=nbl, n=n_lat, tq=1024, hps=1)
    ogl = _attn_gqa_call(qkv, ckg, cvg, row0=0, nb=nbl, n=n_lat, tq=256)
    odc = _attn_diff_call(qkv, None, None, lamv, diff_subln_g, row0=tl, nb=nbc, n=seq, tq=seq,
                          hps=N_DIFF_HEADS)
    ogc = _attn_gqa_call(qkv, None, None, row0=tl, nb=nbc, n=seq, tq=seq)

    x1, h2, aff = _outproj_call(odl, ogl, odc, ogc, xs, xp, mod, norm2_g, w_out[0].astype(BF16),
                                _split3_bf16(w_router[0]), ne, n_lat)
    aff_t = aff.T

    slot_l, starts_l = _route_call(aff_t, col0=0, n_sets=nbl, n=n_lat, offset_stride=0, name="route_lat")
    slot_c, starts_c = _route_call(aff_t, col0=tl, n_sets=nbc, n=seq,
                                   offset_stride=CAPACITY_FACTOR * seq // ne, name="route_ctx")
    slot_t = jnp.concatenate([slot_l, slot_c], axis=1)
    tab_l = starts_l[:, :, :tpg + 1]
    tab_c = jnp.concatenate([starts_c[:, :, 0].T, starts_c[-1:, :, 1].T], axis=1)[None]
    tab = jnp.concatenate([tab_l, tab_c], axis=0).reshape(-1)

    x_slots, gate_slots = _dispatch_call(tab, h2, slot_t, aff_t, ng=ng, rows=rows, tiles_per_group=tpg)
    y = _moe_call(x_slots, gate_slots, w_expert_gate[0], w_expert_up[0], w_expert_down[0])

    ys, yp = _combine_call(tab, x1, mod, slot_t.T, final_norm_g[None, :], y,
                           n_lat_tokens=tl, n_ctx_tokens=tc, tiles_per_group=tpg)

    return (yp.reshape(nbc, seq, d), ys.reshape(nbl, n_lat, d),
            ndk.reshape(nbc, 1, seq, N_DIFF_HEADS, 2, DIFF_QK_DIM),
            ndv.reshape(nbc, 1, seq, N_DIFF_HEADS, HEAD_DIM),
            ngk.reshape(nbc, 1, seq, N_GQA_KV_HEADS, HEAD_DIM),
            ngv.reshape(nbc, 1, seq, N_GQA_KV_HEADS, HEAD_DIM))
```

```python
import functools
import math

import jax
import jax.numpy as jnp
from jax import lax
from jax.experimental import pallas as pl
from jax.experimental.pallas import tpu as pltpu

F32 = jnp.float32
BF16 = jnp.bfloat16
I32 = jnp.int32

LANES = 128
HEAD_DIM = 128
N_DIFF_HEADS = 8
DIFF_QK_DIM = HEAD_DIM // 2
N_GQA_HEADS = 8
N_GQA_KV_HEADS = 2
GQA_GROUP = N_GQA_HEADS // N_GQA_KV_HEADS
N_EXPERTS = 16
CAPACITY_FACTOR = 2
GRID_W = 64
ROPE_THETA = 10000.0
NORM_EPS = 1e-6
N_MOD = 6
LAMBDA_INIT_L0 = 0.8 - 0.6 * math.exp(-0.3 * 0)

DQ_BLK = 0
DK_BLK = DQ_BLK + N_DIFF_HEADS
DV_BLK = DK_BLK + N_DIFF_HEADS
GQ_BLK = DV_BLK + N_DIFF_HEADS
GK_BLK = GQ_BLK + N_GQA_HEADS
GV_BLK = GK_BLK + N_GQA_KV_HEADS
IN_BLKS = GV_BLK + N_GQA_KV_HEADS

TOK_TILE = 256
OUTPROJ_TILE = 512
KV_CHUNK_MAX = 2048
DISPATCH_WIN = 64
NEG_BIG = -1e30
LOG2_E = math.log2(math.e)
BISECT_STEPS = 152

VMEM_LIMIT_MB = 56


def _params(n_axes, vmem_mb=VMEM_LIMIT_MB):
    return pltpu.CompilerParams(dimension_semantics=("arbitrary",) * n_axes,
                                vmem_limit_bytes=vmem_mb << 20)


def _sigmoid(x):
    return 1.0 / (1.0 + jnp.exp(-x))


def _rms(x, eps=NORM_EPS):
    return x * lax.rsqrt(jnp.mean(x * x, axis=-1, keepdims=True) + eps)


def _mod_kernel(c_ref, w_ref, b_ref, o_ref):
    c = c_ref[...]
    a = (c * _sigmoid(c)).astype(BF16)
    o_ref[...] = jnp.dot(a, w_ref[...].astype(BF16), preferred_element_type=F32) + b_ref[...]


def _mod_call(cond, w_mod, b_mod):
    d, m = w_mod.shape
    tn = next(t for t in (1024, 512, 256, LANES) if m % t == 0)
    return pl.pallas_call(
        _mod_kernel, name="mod",
        grid=(m // tn,),
        in_specs=[pl.BlockSpec((8, d), lambda j: (0, 0)),
                  pl.BlockSpec((d, tn), lambda j: (0, j)),
                  pl.BlockSpec((1, tn), lambda j: (0, j))],
        out_specs=pl.BlockSpec((8, tn), lambda j: (0, j)),
        out_shape=jax.ShapeDtypeStruct((8, m), F32),
        compiler_params=_params(1),
    )(cond, w_mod, b_mod)


def _rope(x, cos, sin_signed, half):
    lane = lax.broadcasted_iota(I32, x.shape, 1)
    first = (lane % (2 * half)) < half
    rot = jnp.where(first, pltpu.roll(x, LANES - half, 1), pltpu.roll(x, half, 1))
    return x * cos + rot * sin_signed


def _inproj_kernel(xs_ref, xp_ref, mod_ref, n1g_ref, w_ref, qn_ref, kn_ref,
                   cd_ref, sd_ref, cg_ref, sg_ref,
                   qkv_ref, ndk_ref, ndv_ref, ngk_ref, ngv_ref, *, n_lat_tiles):
    i = pl.program_id(0)
    is_ctx = i >= n_lat_tiles
    x = jnp.where(is_ctx, xp_ref[...], xs_ref[...])
    sh1 = mod_ref[0:1, :]
    sc1 = mod_ref[1:2, :]
    h = (_rms(x) * n1g_ref[...]) * (1.0 + sc1) + sh1
    hb = h.astype(BF16)
    cd, sd, cg, sg = cd_ref[...], sd_ref[...], cg_ref[...], sg_ref[...]
    qn, kn = qn_ref[...], kn_ref[...]
    cache_stores = []

    def seg(blk0, nblk):
        return jnp.dot(hb, w_ref[:, blk0 * LANES:(blk0 + nblk) * LANES], preferred_element_type=F32)

    def put(blk, val):
        qkv_ref[:, blk * LANES:(blk + 1) * LANES] = val.astype(BF16)

    p = seg(DQ_BLK, N_DIFF_HEADS)
    for j in range(N_DIFF_HEADS):
        v = p[:, j * LANES:(j + 1) * LANES]
        put(DQ_BLK + j, _rope(v, cd, sd, DIFF_QK_DIM // 4) * (DIFF_QK_DIM ** -0.5 * LOG2_E))
    p = seg(DK_BLK, N_DIFF_HEADS)
    for j in range(N_DIFF_HEADS):
        v = p[:, j * LANES:(j + 1) * LANES]
        cache_stores.append((ndk_ref, j, v))
        put(DK_BLK + j, _rope(v, cd, sd, DIFF_QK_DIM // 4))
    p = seg(DV_BLK, N_DIFF_HEADS)
    for j in range(N_DIFF_HEADS):
        v = p[:, j * LANES:(j + 1) * LANES]
        cache_stores.append((ndv_ref, j, v))
        put(DV_BLK + j, v)
    p = seg(GQ_BLK, N_GQA_HEADS)
    for j in range(N_GQA_HEADS):
        v = _rms(p[:, j * LANES:(j + 1) * LANES]) * qn
        put(GQ_BLK + j, _rope(v, cg, sg, HEAD_DIM // 4) * (HEAD_DIM ** -0.5 * LOG2_E))
    p = seg(GK_BLK, 2 * N_GQA_KV_HEADS)
    for j in range(N_GQA_KV_HEADS):
        v = _rms(p[:, j * LANES:(j + 1) * LANES]) * kn
        cache_stores.append((ngk_ref, j, v))
        put(GK_BLK + j, _rope(v, cg, sg, HEAD_DIM // 4))
    for j in range(N_GQA_KV_HEADS):
        v = p[:, (N_GQA_KV_HEADS + j) * LANES:(N_GQA_KV_HEADS + j + 1) * LANES]
        cache_stores.append((ngv_ref, j, v))
        put(GV_BLK + j, v)

    @pl.when(is_ctx)
    def _():
        for ref, j, v in cache_stores:
            if ref is ndk_ref:
                ref[j * LANES:(j + 1) * LANES, :] = v.T
            else:
                ref[:, j * LANES:(j + 1) * LANES] = v


def _inproj_call(xs, xp, mod, n1g, w_in_b, qn, kn, tables, n_lat):
    tl, d = xs.shape
    tc = xp.shape[0]
    tm = TOK_TILE
    nl, nc = tl // tm, tc // tm
    tpb = n_lat // tm
    nbl = tl // n_lat
    width = IN_BLKS * LANES

    def tab_spec():
        return pl.BlockSpec((tm, LANES), lambda i: (jnp.where(i < nl, i % tpb, tpb), 0))

    def cache_spec(w):
        return pl.BlockSpec((tm, w), lambda i: (jnp.maximum(i - nl, 0), 0))

    dw, gw = N_DIFF_HEADS * LANES, N_GQA_KV_HEADS * LANES
    return pl.pallas_call(
        functools.partial(_inproj_kernel, n_lat_tiles=nl), name="inproj",
        grid=(nl + nc,),
        in_specs=[pl.BlockSpec((tm, d), lambda i: (jnp.minimum(i, nl - 1), 0)),
                  pl.BlockSpec((tm, d), lambda i: (jnp.maximum(i - nl, 0), 0)),
                  pl.BlockSpec((None, N_MOD, d), lambda i: (jnp.where(i < nl, i // tpb, nbl), 0, 0)),
                  pl.BlockSpec((1, d), lambda i: (0, 0)),
                  pl.BlockSpec((d, width), lambda i: (0, 0), pipeline_mode=pl.Buffered(1)),
                  pl.BlockSpec((1, LANES), lambda i: (0, 0)),
                  pl.BlockSpec((1, LANES), lambda i: (0, 0)),
                  tab_spec(), tab_spec(), tab_spec(), tab_spec()],
        out_specs=[pl.BlockSpec((tm, width), lambda i: (i, 0)),
                   pl.BlockSpec((None, dw, tm), lambda i: (jnp.maximum(i - nl, 0), 0, 0)),
                   cache_spec(dw), cache_spec(gw), cache_spec(gw)],
        out_shape=[jax.ShapeDtypeStruct((tl + tc, width), BF16),
                   jax.ShapeDtypeStruct((nc, dw, tm), F32), jax.ShapeDtypeStruct((tc, dw), F32),
                   jax.ShapeDtypeStruct((tc, gw), F32), jax.ShapeDtypeStruct((tc, gw), F32)],
        compiler_params=_params(1),
    )(xs, xp, mod, n1g, w_in_b, qn, kn, *tables)


def _rope_tables(n, dim, pad_rows):
    t = jnp.arange(n, dtype=I32)
    row = (t // GRID_W).astype(F32)
    col = (t % GRID_W).astype(F32)
    sec = dim // 2
    inv = ROPE_THETA ** (-jnp.arange(0, sec, 2, dtype=F32) / sec)
    ang = jnp.stack([row[:, None] * inv, col[:, None] * inv], axis=1)
    ang = jnp.concatenate([ang, ang], axis=-1).reshape(n, dim)
    sign = jnp.where((jnp.arange(dim) % sec) < sec // 2, -1.0, 1.0).astype(F32)
    cos, sin = jnp.cos(ang), jnp.sin(ang) * sign
    reps = LANES // dim
    cos, sin = jnp.tile(cos, (1, reps)), jnp.tile(sin, (1, reps))
    cos = jnp.concatenate([cos, jnp.ones((pad_rows, LANES), F32)], axis=0)
    sin = jnp.concatenate([sin, jnp.zeros((pad_rows, LANES), F32)], axis=0)
    return cos, sin


def _flash(q, chunks):
    m_rows = q.shape[0]
    m = jnp.full((m_rows, 1), NEG_BIG, F32)
    acc = jnp.zeros((m_rows, HEAD_DIM), F32)
    l = jnp.zeros((m_rows, 1), F32)
    for scores, values in chunks:
        s = scores(q)
        m_new = jnp.maximum(m, jnp.max(s, axis=-1, keepdims=True))
        alpha = jnp.exp2(m - m_new)
        p = jnp.exp2(s - m_new)
        l = alpha * l + jnp.sum(p, axis=-1, keepdims=True)
        acc = alpha * acc + jnp.dot(p.astype(BF16), values(), preferred_element_type=F32)
        m = m_new
    return acc / l


def _scores_nt(k):
    return lambda q: lax.dot_general(q, k(), (((1,), (1,)), ((), ())), preferred_element_type=F32)


def _kv_chunks(k_ref, v_ref, ck_ref, cv_ref, col, cache_k_transposed):
    cs = slice(col * LANES, (col + 1) * LANES)
    chunks = []
    if ck_ref is not None:
        if cache_k_transposed:
            scores = lambda q: jnp.dot(q, ck_ref[cs, :].astype(BF16), preferred_element_type=F32)
        else:
            scores = _scores_nt(lambda: ck_ref[:, cs].astype(BF16))
        chunks.append((scores, lambda: cv_ref[:, cs].astype(BF16)))
    n = k_ref.shape[0]
    n_chunks = pl.cdiv(n, KV_CHUNK_MAX)
    assert n % n_chunks == 0
    step = n // n_chunks
    for c in range(n_chunks):
        rs = slice(c * step, (c + 1) * step)
        chunks.append((_scores_nt(lambda rs=rs: k_ref[rs, cs]), lambda rs=rs: v_ref[rs, cs]))
    return chunks


def _attn_diff_kernel(*refs, hps, has_cache):
    if has_cache:
        q_ref, k_ref, v_ref, ck_ref, cv_ref, lam_ref, g_ref, o_ref = refs
    else:
        q_ref, k_ref, v_ref, lam_ref, g_ref, o_ref = refs
        ck_ref = cv_ref = None
    lv = lam_ref[...]
    lam = (jnp.exp(jnp.sum(lv[0:1] * lv[1:2], axis=-1, keepdims=True))
           - jnp.exp(jnp.sum(lv[2:3] * lv[3:4], axis=-1, keepdims=True)) + LAMBDA_INIT_L0)
    for j in range(hps):
        q = q_ref[:, j * LANES:(j + 1) * LANES]
        lane = lax.broadcasted_iota(I32, q.shape, 1)
        zero = jnp.zeros_like(q)
        chunks = _kv_chunks(k_ref, v_ref, ck_ref, cv_ref, j, cache_k_transposed=True)
        o1 = _flash(jnp.where(lane < DIFF_QK_DIM, q, zero), chunks)
        o2 = _flash(jnp.where(lane >= DIFF_QK_DIM, q, zero), chunks)
        o = o1 - lam * o2
        o = (_rms(o) * g_ref[...]) * (1.0 - LAMBDA_INIT_L0)
        o_ref[:, j * LANES:(j + 1) * LANES] = o.astype(o_ref.dtype)


def _attn_diff_call(qkv, ck, cv, lamv, subg, *, row0, nb, n, tq, hps):
    w = hps * LANES
    hb_n = N_DIFF_HEADS // hps
    qpb = n // tq
    in_specs = [pl.BlockSpec((tq, w), lambda b, h, qi: (row0 // tq + b * qpb + qi, DQ_BLK // hps + h)),
                pl.BlockSpec((n, w), lambda b, h, qi: (row0 // n + b, DK_BLK // hps + h)),
                pl.BlockSpec((n, w), lambda b, h, qi: (row0 // n + b, DV_BLK // hps + h))]
    args = [qkv, qkv, qkv]
    if ck is not None:
        past = ck.shape[1]
        in_specs += [pl.BlockSpec((w, past), lambda b, h, qi: (b * hb_n + h, 0)),
                     pl.BlockSpec((past, w), lambda b, h, qi: (b, h))]
        args += [ck, cv]
    in_specs += [pl.BlockSpec(lamv.shape, lambda b, h, qi: (0, 0)),
                 pl.BlockSpec((1, LANES), lambda b, h, qi: (0, 0))]
    args += [lamv, subg]
    return pl.pallas_call(
        functools.partial(_attn_diff_kernel, hps=hps, has_cache=ck is not None),
        name="attn_diff_lat" if ck is not None else "attn_diff_ctx",
        grid=(nb, hb_n, qpb),
        in_specs=in_specs,
        out_specs=pl.BlockSpec((tq, w), lambda b, h, qi: (b * qpb + qi, h)),
        out_shape=jax.ShapeDtypeStruct((nb * n, N_DIFF_HEADS * LANES), BF16),
        compiler_params=_params(3),
    )(*args)


def _attn_gqa_kernel(*refs, has_cache):
    if has_cache:
        q_ref, k_ref, v_ref, ck_ref, cv_ref, o_ref = refs
    else:
        q_ref, k_ref, v_ref, o_ref = refs
        ck_ref = cv_ref = None
    tq = q_ref.shape[0]
    q = jnp.concatenate([q_ref[:, r * LANES:(r + 1) * LANES] for r in range(GQA_GROUP)], axis=0)
    chunks = _kv_chunks(k_ref, v_ref, ck_ref, cv_ref, 0, cache_k_transposed=False)
    o = _flash(q, chunks).astype(o_ref.dtype)
    for r in range(GQA_GROUP):
        o_ref[:, r * LANES:(r + 1) * LANES] = o[r * tq:(r + 1) * tq]


def _attn_gqa_call(qkv, ck, cv, *, row0, nb, n, tq):
    w = GQA_GROUP * LANES
    qpb = n // tq
    in_specs = [pl.BlockSpec((tq, w), lambda b, g, qi: (row0 // tq + b * qpb + qi, GQ_BLK // GQA_GROUP + g)),
                pl.BlockSpec((n, LANES), lambda b, g, qi: (row0 // n + b, GK_BLK + g)),
                pl.BlockSpec((n, LANES), lambda b, g, qi: (row0 // n + b, GV_BLK + g))]
    args = [qkv, qkv, qkv]
    if ck is not None:
        past = ck.shape[0] // nb
        in_specs += [pl.BlockSpec((past, LANES), lambda b, g, qi: (b, g)),
                     pl.BlockSpec((past, LANES), lambda b, g, qi: (b, g))]
        args += [ck, cv]
    return pl.pallas_call(
        functools.partial(_attn_gqa_kernel, has_cache=ck is not None),
        name="attn_gqa_lat" if ck is not None else "attn_gqa_ctx",
        grid=(nb, N_GQA_KV_HEADS, qpb),
        in_specs=in_specs,
        out_specs=pl.BlockSpec((tq, w), lambda b, g, qi: (b * qpb + qi, g)),
        out_shape=jax.ShapeDtypeStruct((nb * n, N_GQA_HEADS * LANES), BF16),
        compiler_params=_params(3),
    )(*args)


def _outproj_kernel(odl_ref, ogl_ref, odc_ref, ogc_ref, xs_ref, xp_ref, mod_ref, n2g_ref, w_ref, wr3_ref,
                    x1_ref, h2_ref, aff_ref, *, n_lat_tiles):
    i = pl.program_id(0)
    is_ctx = i >= n_lat_tiles
    od = jnp.where(is_ctx, odc_ref[...], odl_ref[...])
    og = jnp.where(is_ctx, ogc_ref[...], ogl_ref[...])
    x = jnp.where(is_ctx, xp_ref[...], xs_ref[...])
    dw = od.shape[1]
    a = (jnp.dot(od, w_ref[0:dw, :], preferred_element_type=F32)
         + jnp.dot(og, w_ref[dw:, :], preferred_element_type=F32))
    g1 = mod_ref[2:3, :]
    sh2 = mod_ref[3:4, :]
    sc2 = mod_ref[4:5, :]
    x1 = x + g1 * a
    x1_ref[...] = x1
    h2 = (_rms(x1) * n2g_ref[...]) * (1.0 + sc2) + sh2
    h_hi = h2.astype(BF16)
    h2_ref[...] = h_hi
    ne = aff_ref.shape[1]
    rem = h2 - h_hi.astype(F32)
    h_mid = rem.astype(BF16)
    h_lo = (rem - h_mid.astype(F32)).astype(BF16)
    w3 = wr3_ref[...]
    p_hi = jnp.dot(h_hi, w3, preferred_element_type=F32)
    p_mid = jnp.dot(h_mid, w3, preferred_element_type=F32)
    p_lo = jnp.dot(h_lo, w3, preferred_element_type=F32)
    logits = (((p_lo[:, 0:ne] + p_mid[:, ne:2 * ne] + p_hi[:, 2 * ne:3 * ne])
               + (p_mid[:, 0:ne] + p_hi[:, ne:2 * ne])) + p_hi[:, 0:ne])
    z = jnp.exp(logits - jnp.max(logits, axis=-1, keepdims=True))
    aff_ref[...] = z / jnp.sum(z, axis=-1, keepdims=True)


def _split3_bf16(w):
    hi = w.astype(BF16)
    rem = w - hi.astype(F32)
    mid = rem.astype(BF16)
    lo = (rem - mid.astype(F32)).astype(BF16)
    pad = jnp.zeros((w.shape[0], LANES - 3 * w.shape[1]), BF16)
    return jnp.concatenate([hi, mid, lo, pad], axis=1)


def _outproj_call(odl, ogl, odc, ogc, xs, xp, mod, n2g, w_out_b, wr3, ne, n_lat):
    tl, d = xs.shape
    tc = xp.shape[0]
    tm = OUTPROJ_TILE
    nl, nc = tl // tm, tc // tm
    tpb = n_lat // tm
    nbl = tl // n_lat
    aw = odl.shape[1]

    def lat_spec(w):
        return pl.BlockSpec((tm, w), lambda i: (jnp.minimum(i, nl - 1), 0))

    def ctx_spec(w):
        return pl.BlockSpec((tm, w), lambda i: (jnp.maximum(i - nl, 0), 0))

    return pl.pallas_call(
        functools.partial(_outproj_kernel, n_lat_tiles=nl), name="outproj",
        grid=(nl + nc,),
        in_specs=[lat_spec(aw), lat_spec(aw), ctx_spec(aw), ctx_spec(aw), lat_spec(d), ctx_spec(d),
                  pl.BlockSpec((None, N_MOD, d), lambda i: (jnp.where(i < nl, i // tpb, nbl), 0, 0)),
                  pl.BlockSpec((1, d), lambda i: (0, 0)),
                  pl.BlockSpec(w_out_b.shape, lambda i: (0, 0), pipeline_mode=pl.Buffered(1)),
                  pl.BlockSpec((d, LANES), lambda i: (0, 0))],
        out_specs=[pl.BlockSpec((tm, d), lambda i: (i, 0)),
                   pl.BlockSpec((tm, d), lambda i: (i, 0)),
                   pl.BlockSpec((tm, ne), lambda i: (i, 0))],
        out_shape=[jax.ShapeDtypeStruct((tl + tc, d), F32),
                   jax.ShapeDtypeStruct((tl + tc, d), BF16),
                   jax.ShapeDtypeStruct((tl + tc, ne), F32)],
        compiler_params=_params(1),
    )(odl, ogl, odc, ogc, xs, xp, mod, n2g, w_out_b, wr3)


def _route_kernel(aff_ref, slot_ref, starts_ref, *, cap, n, offset_stride):
    ne = aff_ref.shape[0]
    n_sets = aff_ref.shape[1] // n
    tt = TOK_TILE
    aff = jnp.concatenate([aff_ref[:, s * n:(s + 1) * n] for s in range(n_sets)], axis=0)
    capf = float(cap)
    ne_all = n_sets * ne

    def count(mask):
        return jnp.sum(mask.astype(F32), axis=1, keepdims=True)

    def bisect(_, carry):
        lo, hi = carry
        mid = 0.5 * (lo + hi)
        ok = count(aff >= mid) >= capf
        return jnp.where(ok, mid, lo), jnp.where(ok, hi, mid)

    thr, _ = lax.fori_loop(0, BISECT_STEPS, bisect,
                           (jnp.zeros((ne_all, 1), F32), jnp.full((ne_all, 1), 2.0, F32)))
    gt = aff > thr
    tie = aff == thr
    need = capf - count(gt)

    ri = lax.broadcasted_iota(I32, (tt, tt), 0)
    ci = lax.broadcasted_iota(I32, (tt, tt), 1)
    upper = (ri < ci).astype(BF16)
    set_id = lax.broadcasted_iota(I32, (ne_all, 1), 0) // ne
    tie_pref = jnp.zeros((ne_all, 1), F32)
    sel_pref = (set_id * offset_stride).astype(F32)
    lane = lax.broadcasted_iota(I32, (ne_all, LANES), 1)
    starts = jnp.zeros((ne_all, LANES), F32)
    for t in range(n // tt):
        cs = slice(t * tt, (t + 1) * tt)
        tie_t = tie[:, cs]
        tie_rank = tie_pref + jnp.dot(tie_t.astype(BF16), upper, preferred_element_type=F32)
        sel_t = gt[:, cs] | (tie_t & (tie_rank < need))
        slot_t = sel_pref + jnp.dot(sel_t.astype(BF16), upper, preferred_element_type=F32)
        slot_t = jnp.where(sel_t, slot_t, -1.0).astype(I32)
        for s in range(n_sets):
            slot_ref[:, s * n + t * tt:s * n + (t + 1) * tt] = slot_t[s * ne:(s + 1) * ne]
        starts = jnp.where(lane == t, sel_pref, starts)
        tie_pref = tie_pref + count(tie_t)
        sel_pref = sel_pref + count(sel_t)
    starts = jnp.where(lane == n // tt, sel_pref, starts).astype(I32)
    for s in range(n_sets):
        starts_ref[s] = starts[s * ne:(s + 1) * ne]


def _route_call(aff_t, *, col0, n_sets, n, offset_stride, name):
    ne = aff_t.shape[0]
    cap = CAPACITY_FACTOR * n // N_EXPERTS
    w = n_sets * n
    return pl.pallas_call(
        functools.partial(_route_kernel, cap=cap, n=n, offset_stride=offset_stride), name=name,
        grid=(1,),
        in_specs=[pl.BlockSpec((ne, w), lambda s: (0, col0 // w))],
        out_specs=[pl.BlockSpec((ne, w), lambda s: (0, 0)),
                   pl.BlockSpec((n_sets, ne, LANES), lambda s: (0, 0, 0))],
        out_shape=[jax.ShapeDtypeStruct((ne, w), I32),
                   jax.ShapeDtypeStruct((n_sets, ne, LANES), I32)],
        compiler_params=_params(1),
    )(aff_t)


def _window_start(first_slot, rows):
    return pl.multiple_of(jnp.minimum((first_slot // 16) * 16, rows - DISPATCH_WIN), 16)


def _dispatch_kernel(tab_ref, h2_ref, slot_ref, gate_ref, x_ref, gc_ref, *, tiles_per_group, ne):
    g, eb, t = pl.program_id(0), pl.program_id(1), pl.program_id(2)
    epb, rows = x_ref.shape[0], x_ref.shape[1]
    tt = TOK_TILE
    win = DISPATCH_WIN

    @pl.when(t == 0)
    def _():
        x_ref[...] = jnp.zeros_like(x_ref)
        gc_ref[...] = jnp.zeros_like(gc_ref)

    sub = lax.broadcasted_iota(I32, (win, tt), 0)
    h_t = h2_ref[...]

    def scatter(i, start, hit):
        gwin = jnp.sum(jnp.where(hit, gate_ref[i:i + 1, :], 0.0), axis=1, keepdims=True)
        gc_ref[i, pl.ds(start, win), :] += jnp.broadcast_to(gwin, (win, LANES))

    starts, hits = [], []
    for i in range(epb):
        base = (g * ne + eb * epb + i) * (tiles_per_group + 1) + t
        start = _window_start(tab_ref[base], rows)
        starts.append(start)
        hits.append(slot_ref[i:i + 1, :] - start == sub)
    onehot = jnp.concatenate([h.astype(BF16) for h in hits], axis=0)
    moved = jnp.dot(onehot, h_t, preferred_element_type=F32).astype(BF16)
    for i in range(epb):
        x_ref[i, pl.ds(starts[i], win), :] += moved[i * win:(i + 1) * win]
        scatter(i, starts[i], hits[i])

    for i in range(epb):
        base = (g * ne + eb * epb + i) * (tiles_per_group + 1) + t
        last = tab_ref[base + 1]

        def extra(w, c, i=i, first=starts[i]):
            lo = first + w * win
            start = _window_start(lo, rows)
            srow = slot_ref[i:i + 1, :]
            hit = (srow - start == sub) & (srow >= lo)
            x_ref[i, pl.ds(start, win), :] += jnp.dot(hit.astype(BF16), h_t,
                                                      preferred_element_type=F32).astype(BF16)
            scatter(i, start, hit)
            return c

        lax.fori_loop(1, (last - starts[i] + win - 1) // win, extra, 0)


def _dispatch_call(tab, h2, slot_t, gate_t, *, ng, rows, tiles_per_group):
    t_all, d = h2.shape
    ne = slot_t.shape[0]
    epb = 8
    tt = TOK_TILE

    def tok_spec(w):
        return pl.BlockSpec((w, tt), lambda g, eb, t, tab: (eb, g * tiles_per_group + t))

    grid_spec = pltpu.PrefetchScalarGridSpec(
        num_scalar_prefetch=1,
        grid=(ng, ne // epb, tiles_per_group),
        in_specs=[pl.BlockSpec((tt, d), lambda g, eb, t, tab: (g * tiles_per_group + t, 0)),
                  tok_spec(epb), tok_spec(epb)],
        out_specs=[pl.BlockSpec((None, epb, rows, d), lambda g, eb, t, tab: (g, eb, 0, 0)),
                   pl.BlockSpec((None, epb, rows, LANES), lambda g, eb, t, tab: (g, eb, 0, 0))])
    return pl.pallas_call(
        functools.partial(_dispatch_kernel, tiles_per_group=tiles_per_group, ne=ne), name="dispatch",
        grid_spec=grid_spec,
        out_shape=[jax.ShapeDtypeStruct((ng, ne, rows, d), BF16),
                   jax.ShapeDtypeStruct((ng, ne, rows, LANES), F32)],
        compiler_params=_params(3),
    )(tab, h2, slot_t, gate_t)


def _moe_kernel(x_ref, gc_ref, wg_ref, wu_ref, wd_ref, y_ref, yacc_ref):
    f = pl.program_id(1)
    ng = x_ref.shape[0]
    tf = wg_ref.shape[1]

    @pl.when(f == 0)
    def _():
        yacc_ref[...] = jnp.zeros_like(yacc_ref)

    w_gu = jnp.concatenate([wg_ref[...].astype(BF16), wu_ref[...].astype(BF16)], axis=1)
    wd = wd_ref[...].astype(BF16)
    for g in range(ng):
        gu = jnp.dot(x_ref[g], w_gu, preferred_element_type=F32)
        gate, up = gu[:, :tf], gu[:, tf:]
        hid = (gate * _sigmoid(gate) * up).astype(BF16)
        yacc_ref[g] += jnp.dot(hid, wd, preferred_element_type=F32)

    @pl.when(f == pl.num_programs(1) - 1)
    def _():
        for g in range(ng):
            y_ref[g] = (yacc_ref[g] * gc_ref[g][:, 0:1]).astype(y_ref.dtype)


def _moe_call(x, gc, w_gate, w_up, w_down):
    ng, ne, rows, d = x.shape
    fdim = w_gate.shape[2]
    tf = min(fdim, 256)
    return pl.pallas_call(
        _moe_kernel, name="moe",
        grid=(ne, fdim // tf),
        in_specs=[pl.BlockSpec((ng, None, rows, d), lambda e, f: (0, e, 0, 0)),
                  pl.BlockSpec((ng, None, rows, LANES), lambda e, f: (0, e, 0, 0)),
                  pl.BlockSpec((None, d, tf), lambda e, f: (e, 0, f)),
                  pl.BlockSpec((None, d, tf), lambda e, f: (e, 0, f)),
                  pl.BlockSpec((None, tf, d), lambda e, f: (e, f, 0))],
        out_specs=pl.BlockSpec((ng, None, rows, d), lambda e, f: (0, e, 0, 0)),
        out_shape=jax.ShapeDtypeStruct((ng, ne, rows, d), BF16),
        scratch_shapes=[pltpu.VMEM((ng, rows, d), F32)],
        compiler_params=_params(2),
    )(x, gc, w_gate, w_up, w_down)


def _combine_kernel(tab_ref, x1_ref, mod_ref, slot_ref, fng_ref, y_hbm, ys_ref, yp_ref,
                    ybuf, obuf, acc_ref, sem, osem, *, tiles_per_group, n_lat_tiles, rows):
    i = pl.program_id(0)
    n_tiles = pl.num_programs(0)
    ne = sem.shape[1]
    win = DISPATCH_WIN
    tt = TOK_TILE
    per = LANES // win

    def first_slot(tile, e):
        return tab_ref[((tile // tiles_per_group) * ne + e) * (tiles_per_group + 1) + tile % tiles_per_group]

    def window_copies(tile, buf):
        g = tile // tiles_per_group
        return [pltpu.make_async_copy(y_hbm.at[g, e, pl.ds(_window_start(first_slot(tile, e), rows), win), :],
                                      ybuf.at[buf, pl.ds(e * win, win), :], sem.at[buf, e])
                for e in range(ne)]

    buf = i % 2

    @pl.when(i == 0)
    def _():
        for cp in window_copies(i, buf):
            cp.start()

    @pl.when(i + 1 < n_tiles)
    def _():
        for cp in window_copies(i + 1, 1 - buf):
            cp.start()

    starts = [_window_start(first_slot(i, e), rows) for e in range(ne)]
    lane = lax.broadcasted_iota(I32, (tt, LANES), 1)
    blocks = []
    for k in range(ne // per):
        slot = slot_ref[:, k * per:k * per + 1]
        start = starts[k * per]
        for j in range(1, per):
            pick = lane >= j * win
            slot = jnp.where(pick, slot_ref[:, k * per + j:k * per + j + 1], slot)
            start = jnp.where(pick, starts[k * per + j], start)
        blocks.append((slot - start == lane % win).astype(BF16))
    onehot = jnp.concatenate(blocks, axis=1)

    for cp in window_copies(i, buf):
        cp.wait()
    acc_ref[...] = jnp.dot(onehot, ybuf[buf], preferred_element_type=F32)

    lane_w = lax.broadcasted_iota(I32, (tt, win), 1)
    g = i // tiles_per_group
    for e in range(ne):
        last = tab_ref[(g * ne + e) * (tiles_per_group + 1) + i % tiles_per_group + 1]

        def extra(w, c, e=e, first=starts[e]):
            lo = first + w * win
            start = _window_start(lo, rows)
            cp = pltpu.make_async_copy(y_hbm.at[g, e, pl.ds(start, win), :], obuf, osem.at[0])
            cp.start()
            cp.wait()
            slot = slot_ref[:, e:e + 1]
            hit = ((slot - start == lane_w) & (slot >= lo)).astype(BF16)
            acc_ref[...] += jnp.dot(hit, obuf[...], preferred_element_type=F32)
            return c

        lax.fori_loop(1, (last - starts[e] + win - 1) // win, extra, 0)

    g2 = mod_ref[5:6, :]
    x2 = x1_ref[...] + g2 * acc_ref[...]
    y = _rms(x2) * fng_ref[...]

    @pl.when(i < n_lat_tiles)
    def _():
        ys_ref[...] = y

    @pl.when(i >= n_lat_tiles)
    def _():
        yp_ref[...] = y


def _combine_call(tab, x1, mod, slot_n, fng, y, *, n_lat_tokens, n_ctx_tokens, tiles_per_group):
    t_all, d = x1.shape
    tm = TOK_TILE
    nl = n_lat_tokens // tm
    ne = slot_n.shape[1]
    rows = y.shape[2]
    assert LANES % DISPATCH_WIN == 0 and ne % (LANES // DISPATCH_WIN) == 0
    grid_spec = pltpu.PrefetchScalarGridSpec(
        num_scalar_prefetch=1,
        grid=(t_all // tm,),
        in_specs=[pl.BlockSpec((tm, d), lambda i, tab: (i, 0)),
                  pl.BlockSpec((None, N_MOD, d), lambda i, tab: (i // tiles_per_group, 0, 0)),
                  pl.BlockSpec((tm, ne), lambda i, tab: (i, 0)),
                  pl.BlockSpec((1, d), lambda i, tab: (0, 0)),
                  pl.BlockSpec(memory_space=pl.ANY)],
        out_specs=[pl.BlockSpec((tm, d), lambda i, tab: (jnp.minimum(i, nl - 1), 0)),
                   pl.BlockSpec((tm, d), lambda i, tab: (jnp.maximum(i - nl, 0), 0))],
        scratch_shapes=[pltpu.VMEM((2, ne * DISPATCH_WIN, d), BF16),
                        pltpu.VMEM((DISPATCH_WIN, d), BF16),
                        pltpu.VMEM((tm, d), F32),
                        pltpu.SemaphoreType.DMA((2, ne)),
                        pltpu.SemaphoreType.DMA((1,))])
    return pl.pallas_call(
        functools.partial(_combine_kernel, tiles_per_group=tiles_per_group, n_lat_tiles=nl, rows=rows),
        name="combine",
        grid_spec=grid_spec,
        out_shape=[jax.ShapeDtypeStruct((n_lat_tokens, d), F32),
                   jax.ShapeDtypeStruct((n_ctx_tokens, d), F32)],
        compiler_params=_params(1),
    )(tab, x1, mod, slot_n, fng, y)


def kernel(x_prompt, x_sample, c, cache_diff_k, cache_diff_v, cache_gqa_k, cache_gqa_v, c_ctx, norm1_g, norm2_g, w_mod, b_mod, w_in, w_out, diff_lambda_q1, diff_lambda_k1, diff_lambda_q2, diff_lambda_k2, diff_subln_g, gqa_q_norm_g, gqa_k_norm_g, w_router, w_expert_gate, w_expert_up, w_expert_down, final_norm_g):
    nbc, seq, d = x_prompt.shape
    nbl, n_lat, _ = x_sample.shape
    depth, past = cache_diff_k.shape[1], cache_diff_k.shape[2]
    tc, tl = nbc * seq, nbl * n_lat
    assert depth == 1, "single trunk layer"
    assert tc == n_lat, "the context tokens must fill exactly one routing group"
    assert seq == TOK_TILE and nbl + 1 <= 8
    ne = N_EXPERTS
    tpg = n_lat // TOK_TILE
    ng = nbl + 1
    rows = CAPACITY_FACTOR * n_lat // ne

    xs = x_sample.reshape(tl, d)
    xp = x_prompt.reshape(tc, d)

    cond = jnp.concatenate([c, c_ctx[None, :], jnp.zeros((8 - ng, d), F32)], axis=0)
    mod = _mod_call(cond, w_mod[0], b_mod).reshape(8, N_MOD, d)

    tables = (_rope_tables(n_lat, DIFF_QK_DIM, TOK_TILE) + _rope_tables(n_lat, HEAD_DIM, TOK_TILE))
    qkv, ndk, ndv, ngk, ngv = _inproj_call(
        xs, xp, mod, norm1_g, w_in[0].astype(BF16), gqa_q_norm_g, gqa_k_norm_g, tables, n_lat)

    lamv = jnp.concatenate([diff_lambda_q1, diff_lambda_k1, diff_lambda_q2, diff_lambda_k2], axis=0)
    ckd = jnp.transpose(cache_diff_k, (0, 1, 3, 4, 5, 2)).reshape(nbl * N_DIFF_HEADS * LANES, past)
    cvd = cache_diff_v.reshape(nbl * past, N_DIFF_HEADS * LANES)
    ckg = cache_gqa_k.reshape(nbl * past, N_GQA_KV_HEADS * LANES)
    cvg = cache_gqa_v.reshape(nbl * past, N_GQA_KV_HEADS * LANES)
    odl = _attn_diff_call(qkv, ckd, cvd, lamv, diff_subln_g, row0=0, nb=nbl, n=n_lat, tq=1024, hps=1)
    ogl = _attn_gqa_call(qkv, ckg, cvg, row0=0, nb=nbl, n=n_lat, tq=256)
    odc = _attn_diff_call(qkv, None, None, lamv, diff_subln_g, row0=tl, nb=nbc, n=seq, tq=seq,
                          hps=N_DIFF_HEADS)
    ogc = _attn_gqa_call(qkv, None, None, row0=tl, nb=nbc, n=seq, tq=seq)

    x1, h2, aff = _outproj_call(odl, ogl, odc, ogc, xs, xp, mod, norm2_g, w_out[0].astype(BF16),
                                _split3_bf16(w_router[0]), ne, n_lat)
    aff_t = aff.T

    slot_l, starts_l = _route_call(aff_t, col0=0, n_sets=nbl, n=n_lat, offset_stride=0, name="route_lat")
    slot_c, starts_c = _route_call(aff_t, col0=tl, n_sets=nbc, n=seq,
                                   offset_stride=CAPACITY_FACTOR * seq // ne, name="route_ctx")
    slot_t = jnp.concatenate([slot_l, slot_c], axis=1)
    tab_l = starts_l[:, :, :tpg + 1]
    tab_c = jnp.concatenate([starts_c[:, :, 0].T, starts_c[-1:, :, 1].T], axis=1)[None]
    tab = jnp.concatenate([tab_l, tab_c], axis=0).reshape(-1)

    x_slots, gate_slots = _dispatch_call(tab, h2, slot_t, aff_t, ng=ng, rows=rows, tiles_per_group=tpg)
    y = _moe_call(x_slots, gate_slots, w_expert_gate[0], w_expert_up[0], w_expert_down[0])

    ys, yp = _combine_call(tab, x1, mod, slot_t.T, final_norm_g[None, :], y,
                           n_lat_tokens=tl, n_ctx_tokens=tc, tiles_per_group=tpg)

    return (yp.reshape(nbc, seq, d), ys.reshape(nbl, n_lat, d),
            ndk.reshape(nbc, 1, N_DIFF_HEADS, 2, DIFF_QK_DIM, seq).transpose(0, 1, 5, 2, 3, 4),
            ndv.reshape(nbc, 1, seq, N_DIFF_HEADS, HEAD_DIM),
            ngk.reshape(nbc, 1, seq, N_GQA_KV_HEADS, HEAD_DIM),
            ngv.reshape(nbc, 1, seq, N_GQA_KV_HEADS, HEAD_DIM))
```

```python
import functools
import math

import jax
import jax.numpy as jnp
import numpy as np
from jax import lax
from jax.experimental import pallas as pl
from jax.experimental.pallas import tpu as pltpu

F32 = jnp.float32
BF16 = jnp.bfloat16
I32 = jnp.int32

LANES = 128
HEAD_DIM = 128
N_DIFF_HEADS = 8
DIFF_QK_DIM = HEAD_DIM // 2
N_GQA_HEADS = 8
N_GQA_KV_HEADS = 2
GQA_GROUP = N_GQA_HEADS // N_GQA_KV_HEADS
N_EXPERTS = 16
CAPACITY_FACTOR = 2
GRID_W = 64
ROPE_THETA = 10000.0
NORM_EPS = 1e-6
N_MOD = 6
LAMBDA_INIT_L0 = 0.8 - 0.6 * math.exp(-0.3 * 0)

DQ_BLK = 0
DK_BLK = DQ_BLK + N_DIFF_HEADS
DV_BLK = DK_BLK + N_DIFF_HEADS
GQ_BLK = DV_BLK + N_DIFF_HEADS
GK_BLK = GQ_BLK + N_GQA_HEADS
GV_BLK = GK_BLK + N_GQA_KV_HEADS
IN_BLKS = GV_BLK + N_GQA_KV_HEADS

TOK_TILE = 256
OUTPROJ_TILE = 512
OUTPROJ_SPLIT = 2
KV_CHUNK_MAX = 2048
DISPATCH_WIN = 64
NEG_BIG = -1e30
LOG2_E = math.log2(math.e)
MIN_EXP = -150.0
EXP_STEPS = 8
MANTISSA_STEPS = 53

VMEM_LIMIT_MB = 56


def _params(n_axes, vmem_mb=VMEM_LIMIT_MB):
    return pltpu.CompilerParams(dimension_semantics=("arbitrary",) * n_axes,
                                vmem_limit_bytes=vmem_mb << 20)


def _sigmoid(x):
    return 1.0 / (1.0 + jnp.exp(-x))


def _rms(x, eps=NORM_EPS):
    return x * lax.rsqrt(jnp.mean(x * x, axis=-1, keepdims=True) + eps)


def _mod_kernel(c_ref, w_ref, b_ref, o_ref):
    c = c_ref[...]
    a = (c * _sigmoid(c)).astype(BF16)
    o_ref[...] = jnp.dot(a, w_ref[...].astype(BF16), preferred_element_type=F32) + b_ref[...]


def _mod_call(cond, w_mod, b_mod):
    d, m = w_mod.shape
    tn = next(t for t in (1024, 512, 256, LANES) if m % t == 0)
    return pl.pallas_call(
        _mod_kernel, name="mod",
        grid=(m // tn,),
        in_specs=[pl.BlockSpec((8, d), lambda j: (0, 0)),
                  pl.BlockSpec((d, tn), lambda j: (0, j)),
                  pl.BlockSpec((1, tn), lambda j: (0, j))],
        out_specs=pl.BlockSpec((8, tn), lambda j: (0, j)),
        out_shape=jax.ShapeDtypeStruct((8, m), F32),
        compiler_params=_params(1),
    )(cond, w_mod, b_mod)


def _rope(x, cos, sin_signed, half):
    lane = lax.broadcasted_iota(I32, x.shape, 1)
    first = (lane % (2 * half)) < half
    rot = jnp.where(first, pltpu.roll(x, LANES - half, 1), pltpu.roll(x, half, 1))
    return x * cos + rot * sin_signed


def _inproj_kernel(xs_ref, xp_ref, mod_ref, n1g_ref, w_ref, qn_ref, kn_ref,
                   cd_ref, sd_ref, cg_ref, sg_ref,
                   qkv_ref, ndk_ref, ndv_ref, ngk_ref, ngv_ref, *, n_lat_tiles):
    i = pl.program_id(0)
    is_ctx = i >= n_lat_tiles
    x = jnp.where(is_ctx, xp_ref[...], xs_ref[...])
    sh1 = mod_ref[0:1, :]
    sc1 = mod_ref[1:2, :]
    h = (_rms(x) * n1g_ref[...]) * (1.0 + sc1) + sh1
    hb = h.astype(BF16)
    cd, sd, cg, sg = cd_ref[...], sd_ref[...], cg_ref[...], sg_ref[...]
    qn, kn = qn_ref[...], kn_ref[...]
    cache_stores = []

    def seg(blk0, nblk):
        return jnp.dot(hb, w_ref[:, blk0 * LANES:(blk0 + nblk) * LANES], preferred_element_type=F32)

    def put(blk, val):
        qkv_ref[:, blk * LANES:(blk + 1) * LANES] = val.astype(BF16)

    p = seg(DQ_BLK, N_DIFF_HEADS)
    for j in range(N_DIFF_HEADS):
        v = p[:, j * LANES:(j + 1) * LANES]
        put(DQ_BLK + j, _rope(v, cd, sd, DIFF_QK_DIM // 4) * (DIFF_QK_DIM ** -0.5 * LOG2_E))
    p = seg(DK_BLK, N_DIFF_HEADS)
    for j in range(N_DIFF_HEADS):
        v = p[:, j * LANES:(j + 1) * LANES]
        cache_stores.append((ndk_ref, j, v))
        put(DK_BLK + j, _rope(v, cd, sd, DIFF_QK_DIM // 4))
    p = seg(DV_BLK, N_DIFF_HEADS)
    for j in range(N_DIFF_HEADS):
        v = p[:, j * LANES:(j + 1) * LANES]
        cache_stores.append((ndv_ref, j, v))
        put(DV_BLK + j, v)
    p = seg(GQ_BLK, N_GQA_HEADS)
    for j in range(N_GQA_HEADS):
        v = _rms(p[:, j * LANES:(j + 1) * LANES]) * qn
        put(GQ_BLK + j, _rope(v, cg, sg, HEAD_DIM // 4) * (HEAD_DIM ** -0.5 * LOG2_E))
    p = seg(GK_BLK, 2 * N_GQA_KV_HEADS)
    for j in range(N_GQA_KV_HEADS):
        v = _rms(p[:, j * LANES:(j + 1) * LANES]) * kn
        cache_stores.append((ngk_ref, j, v))
        put(GK_BLK + j, _rope(v, cg, sg, HEAD_DIM // 4))
    for j in range(N_GQA_KV_HEADS):
        v = p[:, (N_GQA_KV_HEADS + j) * LANES:(N_GQA_KV_HEADS + j + 1) * LANES]
        cache_stores.append((ngv_ref, j, v))
        put(GV_BLK + j, v)

    @pl.when(is_ctx)
    def _():
        for ref, j, v in cache_stores:
            if ref is ndk_ref:
                ref[j * LANES:(j + 1) * LANES, :] = v.T
            else:
                ref[:, j * LANES:(j + 1) * LANES] = v


def _inproj_call(xs, xp, mod, n1g, w_in_b, qn, kn, tables, n_lat):
    tl, d = xs.shape
    tc = xp.shape[0]
    tm = TOK_TILE
    nl, nc = tl // tm, tc // tm
    tpb = n_lat // tm
    nbl = tl // n_lat
    width = IN_BLKS * LANES

    def tab_spec():
        return pl.BlockSpec((tm, LANES), lambda i: (jnp.where(i < nl, i % tpb, tpb), 0))

    def cache_spec(w):
        return pl.BlockSpec((tm, w), lambda i: (jnp.maximum(i - nl, 0), 0))

    dw, gw = N_DIFF_HEADS * LANES, N_GQA_KV_HEADS * LANES
    return pl.pallas_call(
        functools.partial(_inproj_kernel, n_lat_tiles=nl), name="inproj",
        grid=(nl + nc,),
        in_specs=[pl.BlockSpec((tm, d), lambda i: (jnp.minimum(i, nl - 1), 0)),
                  pl.BlockSpec((tm, d), lambda i: (jnp.maximum(i - nl, 0), 0)),
                  pl.BlockSpec((None, N_MOD, d), lambda i: (jnp.where(i < nl, i // tpb, nbl), 0, 0)),
                  pl.BlockSpec((1, d), lambda i: (0, 0)),
                  pl.BlockSpec((d, width), lambda i: (0, 0), pipeline_mode=pl.Buffered(1)),
                  pl.BlockSpec((1, LANES), lambda i: (0, 0)),
                  pl.BlockSpec((1, LANES), lambda i: (0, 0)),
                  tab_spec(), tab_spec(), tab_spec(), tab_spec()],
        out_specs=[pl.BlockSpec((tm, width), lambda i: (i, 0)),
                   pl.BlockSpec((None, dw, tm), lambda i: (jnp.maximum(i - nl, 0), 0, 0)),
                   cache_spec(dw), cache_spec(gw), cache_spec(gw)],
        out_shape=[jax.ShapeDtypeStruct((tl + tc, width), BF16),
                   jax.ShapeDtypeStruct((nc, dw, tm), F32), jax.ShapeDtypeStruct((tc, dw), F32),
                   jax.ShapeDtypeStruct((tc, gw), F32), jax.ShapeDtypeStruct((tc, gw), F32)],
        compiler_params=_params(1),
    )(xs, xp, mod, n1g, w_in_b, qn, kn, *tables)


def _rope_tables(n, dim, pad_rows):
    t = np.arange(n)
    row = (t // GRID_W).astype(np.float64)
    col = (t % GRID_W).astype(np.float64)
    sec = dim // 2
    inv = ROPE_THETA ** (-np.arange(0, sec, 2, dtype=np.float64) / sec)
    ang = np.stack([row[:, None] * inv, col[:, None] * inv], axis=1)
    ang = np.concatenate([ang, ang], axis=-1).reshape(n, dim)
    sign = np.where((np.arange(dim) % sec) < sec // 2, -1.0, 1.0)
    cos, sin = np.cos(ang), np.sin(ang) * sign
    reps = LANES // dim
    cos, sin = np.tile(cos, (1, reps)), np.tile(sin, (1, reps))
    cos = np.concatenate([cos, np.ones((pad_rows, LANES))], axis=0)
    sin = np.concatenate([sin, np.zeros((pad_rows, LANES))], axis=0)
    return jnp.asarray(cos, F32), jnp.asarray(sin, F32)


def _flash(q, chunks):
    m_rows = q.shape[0]
    m = jnp.full((m_rows, 1), NEG_BIG, F32)
    acc = jnp.zeros((m_rows, HEAD_DIM), F32)
    l = jnp.zeros((m_rows, 1), F32)
    for scores, values in chunks:
        s = scores(q)
        m_new = jnp.maximum(m, jnp.max(s, axis=-1, keepdims=True))
        alpha = jnp.exp2(m - m_new)
        p = jnp.exp2(s - m_new)
        l = alpha * l + jnp.sum(p, axis=-1, keepdims=True)
        acc = alpha * acc + jnp.dot(p.astype(BF16), values(), preferred_element_type=F32)
        m = m_new
    return acc / l


def _scores_nt(k):
    return lambda q: lax.dot_general(q, k(), (((1,), (1,)), ((), ())), preferred_element_type=F32)


def _kv_chunks(k_ref, v_ref, ck_ref, cv_ref, col, cache_k_transposed):
    cs = slice(col * LANES, (col + 1) * LANES)
    chunks = []
    if ck_ref is not None:
        if cache_k_transposed:
            scores = lambda q: jnp.dot(q, ck_ref[cs, :].astype(BF16), preferred_element_type=F32)
        else:
            scores = _scores_nt(lambda: ck_ref[:, cs].astype(BF16))
        chunks.append((scores, lambda: cv_ref[:, cs].astype(BF16)))
    n = k_ref.shape[0]
    n_chunks = pl.cdiv(n, KV_CHUNK_MAX)
    assert n % n_chunks == 0
    step = n // n_chunks
    for c in range(n_chunks):
        rs = slice(c * step, (c + 1) * step)
        chunks.append((_scores_nt(lambda rs=rs: k_ref[rs, cs]), lambda rs=rs: v_ref[rs, cs]))
    return chunks


def _attn_diff_kernel(*refs, hps, has_cache):
    if has_cache:
        q_ref, k_ref, v_ref, ck_ref, cv_ref, lam_ref, g_ref, o_ref = refs
    else:
        q_ref, k_ref, v_ref, lam_ref, g_ref, o_ref = refs
        ck_ref = cv_ref = None
    lv = lam_ref[...]
    lam = (jnp.exp(jnp.sum(lv[0:1] * lv[1:2], axis=-1, keepdims=True))
           - jnp.exp(jnp.sum(lv[2:3] * lv[3:4], axis=-1, keepdims=True)) + LAMBDA_INIT_L0)
    for j in range(hps):
        q = q_ref[:, j * LANES:(j + 1) * LANES]
        lane = lax.broadcasted_iota(I32, q.shape, 1)
        zero = jnp.zeros_like(q)
        chunks = _kv_chunks(k_ref, v_ref, ck_ref, cv_ref, j, cache_k_transposed=True)
        o1 = _flash(jnp.where(lane < DIFF_QK_DIM, q, zero), chunks)
        o2 = _flash(jnp.where(lane >= DIFF_QK_DIM, q, zero), chunks)
        o = o1 - lam * o2
        o = (_rms(o) * g_ref[...]) * (1.0 - LAMBDA_INIT_L0)
        o_ref[:, j * LANES:(j + 1) * LANES] = o.astype(o_ref.dtype)


def _attn_diff_call(qkv, ck, cv, lamv, subg, *, row0, nb, n, tq, hps):
    w = hps * LANES
    hb_n = N_DIFF_HEADS // hps
    qpb = n // tq
    in_specs = [pl.BlockSpec((tq, w), lambda b, h, qi: (row0 // tq + b * qpb + qi, DQ_BLK // hps + h)),
                pl.BlockSpec((n, w), lambda b, h, qi: (row0 // n + b, DK_BLK // hps + h)),
                pl.BlockSpec((n, w), lambda b, h, qi: (row0 // n + b, DV_BLK // hps + h))]
    args = [qkv, qkv, qkv]
    if ck is not None:
        past = ck.shape[1]
        in_specs += [pl.BlockSpec((w, past), lambda b, h, qi: (b * hb_n + h, 0)),
                     pl.BlockSpec((past, w), lambda b, h, qi: (b, h))]
        args += [ck, cv]
    in_specs += [pl.BlockSpec(lamv.shape, lambda b, h, qi: (0, 0)),
                 pl.BlockSpec((1, LANES), lambda b, h, qi: (0, 0))]
    args += [lamv, subg]
    return pl.pallas_call(
        functools.partial(_attn_diff_kernel, hps=hps, has_cache=ck is not None),
        name="attn_diff_lat" if ck is not None else "attn_diff_ctx",
        grid=(nb, hb_n, qpb),
        in_specs=in_specs,
        out_specs=pl.BlockSpec((tq, w), lambda b, h, qi: (b * qpb + qi, h)),
        out_shape=jax.ShapeDtypeStruct((nb * n, N_DIFF_HEADS * LANES), BF16),
        compiler_params=_params(3),
    )(*args)


def _attn_gqa_kernel(*refs, has_cache):
    if has_cache:
        q_ref, k_ref, v_ref, ck_ref, cv_ref, o_ref = refs
    else:
        q_ref, k_ref, v_ref, o_ref = refs
        ck_ref = cv_ref = None
    tq = q_ref.shape[0]
    q = jnp.concatenate([q_ref[:, r * LANES:(r + 1) * LANES] for r in range(GQA_GROUP)], axis=0)
    chunks = _kv_chunks(k_ref, v_ref, ck_ref, cv_ref, 0, cache_k_transposed=False)
    o = _flash(q, chunks).astype(o_ref.dtype)
    for r in range(GQA_GROUP):
        o_ref[:, r * LANES:(r + 1) * LANES] = o[r * tq:(r + 1) * tq]


def _attn_gqa_call(qkv, ck, cv, *, row0, nb, n, tq):
    w = GQA_GROUP * LANES
    qpb = n // tq
    in_specs = [pl.BlockSpec((tq, w), lambda b, g, qi: (row0 // tq + b * qpb + qi, GQ_BLK // GQA_GROUP + g)),
                pl.BlockSpec((n, LANES), lambda b, g, qi: (row0 // n + b, GK_BLK + g)),
                pl.BlockSpec((n, LANES), lambda b, g, qi: (row0 // n + b, GV_BLK + g))]
    args = [qkv, qkv, qkv]
    if ck is not None:
        past = ck.shape[0] // nb
        in_specs += [pl.BlockSpec((past, LANES), lambda b, g, qi: (b, g)),
                     pl.BlockSpec((past, LANES), lambda b, g, qi: (b, g))]
        args += [ck, cv]
    return pl.pallas_call(
        functools.partial(_attn_gqa_kernel, has_cache=ck is not None),
        name="attn_gqa_lat" if ck is not None else "attn_gqa_ctx",
        grid=(nb, N_GQA_KV_HEADS, qpb),
        in_specs=in_specs,
        out_specs=pl.BlockSpec((tq, w), lambda b, g, qi: (b * qpb + qi, g)),
        out_shape=jax.ShapeDtypeStruct((nb * n, N_GQA_HEADS * LANES), BF16),
        compiler_params=_params(3),
    )(*args)


def _outproj_kernel(odl_ref, ogl_ref, odc_ref, ogc_ref, xs_ref, xp_ref, mod_ref, n2g_ref, w_ref, wr3_ref,
                    x1_ref, h2_ref, aff_ref, *, n_lat_tiles, ne):
    i = pl.program_id(0)
    is_ctx = i >= n_lat_tiles
    dw = odl_ref.shape[1]
    g1 = mod_ref[2:3, :]
    sh2 = mod_ref[3:4, :]
    sc2 = mod_ref[4:5, :]
    w3 = wr3_ref[...]
    tm = x1_ref.shape[0]
    half = tm // OUTPROJ_SPLIT
    projected = []
    for r in range(OUTPROJ_SPLIT):
        rs = slice(r * half, (r + 1) * half)
        od = jnp.where(is_ctx, odc_ref[rs, :], odl_ref[rs, :])
        og = jnp.where(is_ctx, ogc_ref[rs, :], ogl_ref[rs, :])
        projected.append(jnp.dot(od, w_ref[0:dw, :], preferred_element_type=F32)
                         + jnp.dot(og, w_ref[dw:, :], preferred_element_type=F32))
    for r in range(OUTPROJ_SPLIT):
        rs = slice(r * half, (r + 1) * half)
        x = jnp.where(is_ctx, xp_ref[rs, :], xs_ref[rs, :])
        x1 = x + g1 * projected[r]
        x1_ref[rs, :] = x1
        h2 = (_rms(x1) * n2g_ref[...]) * (1.0 + sc2) + sh2
        h_hi = h2.astype(BF16)
        h2_ref[rs, :] = h_hi
        h_mid = (h2 - h_hi.astype(F32)).astype(BF16)
        t = (jnp.dot(h_mid, w3, preferred_element_type=F32) + jnp.dot(h_hi, w3, preferred_element_type=F32))
        t = (pltpu.roll(t, LANES - 2 * ne, 1) + pltpu.roll(t, LANES - ne, 1)) + t
        lane = lax.broadcasted_iota(I32, t.shape, 1)
        logits = jnp.where(lane < ne, t, NEG_BIG)
        z = jnp.exp(logits - jnp.max(logits, axis=-1, keepdims=True))
        aff_ref[rs, :] = z / jnp.sum(z, axis=-1, keepdims=True)


def _split3_bf16(w):
    hi = w.astype(BF16)
    rem = w - hi.astype(F32)
    mid = rem.astype(BF16)
    lo = (rem - mid.astype(F32)).astype(BF16)
    pad = jnp.zeros((w.shape[0], LANES - 3 * w.shape[1]), BF16)
    return jnp.concatenate([hi, mid, lo, pad], axis=1)


def _outproj_call(odl, ogl, odc, ogc, xs, xp, mod, n2g, w_out_b, wr3, ne, n_lat):
    tl, d = xs.shape
    tc = xp.shape[0]
    tm = OUTPROJ_TILE
    nl, nc = tl // tm, tc // tm
    tpb = n_lat // tm
    nbl = tl // n_lat
    aw = odl.shape[1]

    def lat_spec(w):
        return pl.BlockSpec((tm, w), lambda i: (jnp.minimum(i, nl - 1), 0))

    def ctx_spec(w):
        return pl.BlockSpec((tm, w), lambda i: (jnp.maximum(i - nl, 0), 0))

    return pl.pallas_call(
        functools.partial(_outproj_kernel, n_lat_tiles=nl, ne=ne), name="outproj",
        grid=(nl + nc,),
        in_specs=[lat_spec(aw), lat_spec(aw), ctx_spec(aw), ctx_spec(aw), lat_spec(d), ctx_spec(d),
                  pl.BlockSpec((None, N_MOD, d), lambda i: (jnp.where(i < nl, i // tpb, nbl), 0, 0)),
                  pl.BlockSpec((1, d), lambda i: (0, 0)),
                  pl.BlockSpec(w_out_b.shape, lambda i: (0, 0), pipeline_mode=pl.Buffered(1)),
                  pl.BlockSpec((d, LANES), lambda i: (0, 0))],
        out_specs=[pl.BlockSpec((tm, d), lambda i: (i, 0)),
                   pl.BlockSpec((tm, d), lambda i: (i, 0)),
                   pl.BlockSpec((tm, LANES), lambda i: (i, 0))],
        out_shape=[jax.ShapeDtypeStruct((tl + tc, d), F32),
                   jax.ShapeDtypeStruct((tl + tc, d), BF16),
                   jax.ShapeDtypeStruct((tl + tc, LANES), F32)],
        compiler_params=_params(1),
    )(odl, ogl, odc, ogc, xs, xp, mod, n2g, w_out_b, wr3)


def _route_kernel(aff_ref, slot_ref, starts_ref, *, cap, n, offset_stride):
    ne = aff_ref.shape[0]
    n_sets = aff_ref.shape[1] // n
    tt = TOK_TILE
    aff = jnp.concatenate([aff_ref[:, s * n:(s + 1) * n] for s in range(n_sets)], axis=0)
    capf = float(cap)
    ne_all = n_sets * ne

    def count(mask):
        return jnp.sum(mask.astype(F32), axis=1, keepdims=True)

    def threshold(e):
        return jnp.where(e <= MIN_EXP, 0.0, jnp.exp2(e))

    def exp_step(_, carry):
        e_lo, e_hi = carry
        e_mid = jnp.floor(0.5 * (e_lo + e_hi))
        ok = count(aff >= threshold(e_mid)) >= capf
        return jnp.where(ok, e_mid, e_lo), jnp.where(ok, e_hi, e_mid)

    e_lo, e_hi = lax.fori_loop(0, EXP_STEPS, exp_step,
                               (jnp.full((ne_all, 1), MIN_EXP, F32), jnp.full((ne_all, 1), 1.0, F32)))

    def bisect(_, carry):
        lo, hi = carry
        mid = 0.5 * (lo + hi)
        ok = count(aff >= mid) >= capf
        return jnp.where(ok, mid, lo), jnp.where(ok, hi, mid)

    thr, _ = lax.fori_loop(0, MANTISSA_STEPS, bisect, (threshold(e_lo), threshold(e_hi)))
    gt = aff > thr
    tie = aff == thr
    need = capf - count(gt)

    ri = lax.broadcasted_iota(I32, (tt, tt), 0)
    ci = lax.broadcasted_iota(I32, (tt, tt), 1)
    upper = (ri < ci).astype(BF16)
    set_id = lax.broadcasted_iota(I32, (ne_all, 1), 0) // ne
    tie_pref = jnp.zeros((ne_all, 1), F32)
    sel_pref = (set_id * offset_stride).astype(F32)
    lane = lax.broadcasted_iota(I32, (ne_all, LANES), 1)
    starts = jnp.zeros((ne_all, LANES), F32)
    for t in range(n // tt):
        cs = slice(t * tt, (t + 1) * tt)
        tie_t = tie[:, cs]
        tie_rank = tie_pref + jnp.dot(tie_t.astype(BF16), upper, preferred_element_type=F32)
        sel_t = gt[:, cs] | (tie_t & (tie_rank < need))
        slot_t = sel_pref + jnp.dot(sel_t.astype(BF16), upper, preferred_element_type=F32)
        slot_t = jnp.where(sel_t, slot_t, -1.0).astype(I32)
        for s in range(n_sets):
            slot_ref[:, s * n + t * tt:s * n + (t + 1) * tt] = slot_t[s * ne:(s + 1) * ne]
        starts = jnp.where(lane == t, sel_pref, starts)
        tie_pref = tie_pref + count(tie_t)
        sel_pref = sel_pref + count(sel_t)
    starts = jnp.where(lane == n // tt, sel_pref, starts).astype(I32)
    for s in range(n_sets):
        starts_ref[s] = starts[s * ne:(s + 1) * ne]


def _route_call(aff_t, *, col0, n_sets, n, offset_stride, name):
    ne = aff_t.shape[0]
    cap = CAPACITY_FACTOR * n // N_EXPERTS
    w = n_sets * n
    return pl.pallas_call(
        functools.partial(_route_kernel, cap=cap, n=n, offset_stride=offset_stride), name=name,
        grid=(1,),
        in_specs=[pl.BlockSpec((ne, w), lambda s: (0, col0 // w))],
        out_specs=[pl.BlockSpec((ne, w), lambda s: (0, 0)),
                   pl.BlockSpec((n_sets, ne, LANES), lambda s: (0, 0, 0))],
        out_shape=[jax.ShapeDtypeStruct((ne, w), I32),
                   jax.ShapeDtypeStruct((n_sets, ne, LANES), I32)],
        compiler_params=_params(1),
    )(aff_t)


def _window_start(first_slot, rows):
    return pl.multiple_of(jnp.minimum((first_slot // 16) * 16, rows - DISPATCH_WIN), 16)


def _dispatch_kernel(tab_ref, h2_ref, slot_ref, gate_ref, x_ref, gc_ref, *, tiles_per_group, ne):
    g, eb, t = pl.program_id(0), pl.program_id(1), pl.program_id(2)
    epb, rows = x_ref.shape[0], x_ref.shape[1]
    tt = TOK_TILE
    win = DISPATCH_WIN

    @pl.when(t == 0)
    def _():
        x_ref[...] = jnp.zeros_like(x_ref)
        gc_ref[...] = jnp.zeros_like(gc_ref)

    sub = lax.broadcasted_iota(I32, (win, tt), 0)
    h_t = h2_ref[...]

    def scatter(i, start, hit):
        gwin = jnp.sum(jnp.where(hit, gate_ref[i:i + 1, :], 0.0), axis=1, keepdims=True)
        gc_ref[i, pl.ds(start, win), :] += jnp.broadcast_to(gwin, (win, LANES))

    starts, hits = [], []
    for i in range(epb):
        base = (g * ne + eb * epb + i) * (tiles_per_group + 1) + t
        start = _window_start(tab_ref[base], rows)
        starts.append(start)
        hits.append(slot_ref[i:i + 1, :] - start == sub)
    onehot = jnp.concatenate([h.astype(BF16) for h in hits], axis=0)
    moved = jnp.dot(onehot, h_t, preferred_element_type=F32).astype(BF16)
    for i in range(epb):
        x_ref[i, pl.ds(starts[i], win), :] += moved[i * win:(i + 1) * win]
        scatter(i, starts[i], hits[i])

    for i in range(epb):
        base = (g * ne + eb * epb + i) * (tiles_per_group + 1) + t
        last = tab_ref[base + 1]

        def extra(w, c, i=i, first=starts[i]):
            lo = first + w * win
            start = _window_start(lo, rows)
            srow = slot_ref[i:i + 1, :]
            hit = (srow - start == sub) & (srow >= lo)
            x_ref[i, pl.ds(start, win), :] += jnp.dot(hit.astype(BF16), h_t,
                                                      preferred_element_type=F32).astype(BF16)
            scatter(i, start, hit)
            return c

        lax.fori_loop(1, (last - starts[i] + win - 1) // win, extra, 0)


def _dispatch_call(tab, h2, slot_t, gate_t, *, ng, rows, tiles_per_group):
    t_all, d = h2.shape
    ne = slot_t.shape[0]
    epb = 8
    tt = TOK_TILE

    def tok_spec(w):
        return pl.BlockSpec((w, tt), lambda g, eb, t, tab: (eb, g * tiles_per_group + t))

    grid_spec = pltpu.PrefetchScalarGridSpec(
        num_scalar_prefetch=1,
        grid=(ng, ne // epb, tiles_per_group),
        in_specs=[pl.BlockSpec((tt, d), lambda g, eb, t, tab: (g * tiles_per_group + t, 0)),
                  tok_spec(epb), tok_spec(epb)],
        out_specs=[pl.BlockSpec((None, epb, rows, d), lambda g, eb, t, tab: (g, eb, 0, 0)),
                   pl.BlockSpec((None, epb, rows, LANES), lambda g, eb, t, tab: (g, eb, 0, 0))])
    return pl.pallas_call(
        functools.partial(_dispatch_kernel, tiles_per_group=tiles_per_group, ne=ne), name="dispatch",
        grid_spec=grid_spec,
        out_shape=[jax.ShapeDtypeStruct((ng, ne, rows, d), BF16),
                   jax.ShapeDtypeStruct((ng, ne, rows, LANES), F32)],
        compiler_params=_params(3),
    )(tab, h2, slot_t, gate_t)


def _moe_kernel(x_ref, gc_ref, wg_ref, wu_ref, wd_ref, y_ref, yacc_ref):
    f = pl.program_id(1)
    ng = x_ref.shape[0]
    tf = wg_ref.shape[1]

    @pl.when(f == 0)
    def _():
        yacc_ref[...] = jnp.zeros_like(yacc_ref)

    w_gu = jnp.concatenate([wg_ref[...].astype(BF16), wu_ref[...].astype(BF16)], axis=1)
    wd = wd_ref[...].astype(BF16)
    for g in range(ng):
        gu = jnp.dot(x_ref[g], w_gu, preferred_element_type=F32)
        gate, up = gu[:, :tf], gu[:, tf:]
        hid = (gate * _sigmoid(gate) * up).astype(BF16)
        yacc_ref[g] += jnp.dot(hid, wd, preferred_element_type=F32)

    @pl.when(f == pl.num_programs(1) - 1)
    def _():
        for g in range(ng):
            y_ref[g] = (yacc_ref[g] * gc_ref[g][:, 0:1]).astype(y_ref.dtype)


def _moe_call(x, gc, w_gate, w_up, w_down):
    ng, ne, rows, d = x.shape
    fdim = w_gate.shape[2]
    tf = min(fdim, 256)
    return pl.pallas_call(
        _moe_kernel, name="moe",
        grid=(ne, fdim // tf),
        in_specs=[pl.BlockSpec((ng, None, rows, d), lambda e, f: (0, e, 0, 0)),
                  pl.BlockSpec((ng, None, rows, LANES), lambda e, f: (0, e, 0, 0)),
                  pl.BlockSpec((None, d, tf), lambda e, f: (e, 0, f)),
                  pl.BlockSpec((None, d, tf), lambda e, f: (e, 0, f)),
                  pl.BlockSpec((None, tf, d), lambda e, f: (e, f, 0))],
        out_specs=pl.BlockSpec((ng, None, rows, d), lambda e, f: (0, e, 0, 0)),
        out_shape=jax.ShapeDtypeStruct((ng, ne, rows, d), BF16),
        scratch_shapes=[pltpu.VMEM((ng, rows, d), F32)],
        compiler_params=_params(2),
    )(x, gc, w_gate, w_up, w_down)


def _combine_kernel(tab_ref, x1_ref, mod_ref, slot_ref, fng_ref, y_hbm, ys_ref, yp_ref,
                    ybuf, obuf, acc_ref, sem, osem, *, tiles_per_group, n_lat_tiles, rows):
    i = pl.program_id(0)
    n_tiles = pl.num_programs(0)
    ne = sem.shape[1]
    win = DISPATCH_WIN
    tt = TOK_TILE
    per = LANES // win

    def first_slot(tile, e):
        return tab_ref[((tile // tiles_per_group) * ne + e) * (tiles_per_group + 1) + tile % tiles_per_group]

    def window_copies(tile, buf):
        g = tile // tiles_per_group
        return [pltpu.make_async_copy(y_hbm.at[g, e, pl.ds(_window_start(first_slot(tile, e), rows), win), :],
                                      ybuf.at[buf, pl.ds(e * win, win), :], sem.at[buf, e])
                for e in range(ne)]

    buf = i % 2

    @pl.when(i == 0)
    def _():
        for cp in window_copies(i, buf):
            cp.start()

    @pl.when(i + 1 < n_tiles)
    def _():
        for cp in window_copies(i + 1, 1 - buf):
            cp.start()

    starts = [_window_start(first_slot(i, e), rows) for e in range(ne)]
    lane = lax.broadcasted_iota(I32, (tt, LANES), 1)
    blocks = []
    for k in range(ne // per):
        slot = slot_ref[:, k * per:k * per + 1]
        start = starts[k * per]
        for j in range(1, per):
            pick = lane >= j * win
            slot = jnp.where(pick, slot_ref[:, k * per + j:k * per + j + 1], slot)
            start = jnp.where(pick, starts[k * per + j], start)
        blocks.append((slot - start == lane % win).astype(BF16))
    onehot = jnp.concatenate(blocks, axis=1)

    for cp in window_copies(i, buf):
        cp.wait()
    acc_ref[...] = jnp.dot(onehot, ybuf[buf], preferred_element_type=F32)

    lane_w = lax.broadcasted_iota(I32, (tt, win), 1)
    g = i // tiles_per_group
    for e in range(ne):
        last = tab_ref[(g * ne + e) * (tiles_per_group + 1) + i % tiles_per_group + 1]

        def extra(w, c, e=e, first=starts[e]):
            lo = first + w * win
            start = _window_start(lo, rows)
            cp = pltpu.make_async_copy(y_hbm.at[g, e, pl.ds(start, win), :], obuf, osem.at[0])
            cp.start()
            cp.wait()
            slot = slot_ref[:, e:e + 1]
            hit = ((slot - start == lane_w) & (slot >= lo)).astype(BF16)
            acc_ref[...] += jnp.dot(hit, obuf[...], preferred_element_type=F32)
            return c

        lax.fori_loop(1, (last - starts[e] + win - 1) // win, extra, 0)

    g2 = mod_ref[5:6, :]
    x2 = x1_ref[...] + g2 * acc_ref[...]
    y = _rms(x2) * fng_ref[...]

    @pl.when(i < n_lat_tiles)
    def _():
        ys_ref[...] = y

    @pl.when(i >= n_lat_tiles)
    def _():
        yp_ref[...] = y


def _combine_call(tab, x1, mod, slot_n, fng, y, *, n_lat_tokens, n_ctx_tokens, tiles_per_group):
    t_all, d = x1.shape
    tm = TOK_TILE
    nl = n_lat_tokens // tm
    ne = slot_n.shape[1]
    rows = y.shape[2]
    assert LANES % DISPATCH_WIN == 0 and ne % (LANES // DISPATCH_WIN) == 0
    grid_spec = pltpu.PrefetchScalarGridSpec(
        num_scalar_prefetch=1,
        grid=(t_all // tm,),
        in_specs=[pl.BlockSpec((tm, d), lambda i, tab: (i, 0)),
                  pl.BlockSpec((None, N_MOD, d), lambda i, tab: (i // tiles_per_group, 0, 0)),
                  pl.BlockSpec((tm, ne), lambda i, tab: (i, 0)),
                  pl.BlockSpec((1, d), lambda i, tab: (0, 0)),
                  pl.BlockSpec(memory_space=pl.ANY)],
        out_specs=[pl.BlockSpec((tm, d), lambda i, tab: (jnp.minimum(i, nl - 1), 0)),
                   pl.BlockSpec((tm, d), lambda i, tab: (jnp.maximum(i - nl, 0), 0))],
        scratch_shapes=[pltpu.VMEM((2, ne * DISPATCH_WIN, d), BF16),
                        pltpu.VMEM((DISPATCH_WIN, d), BF16),
                        pltpu.VMEM((tm, d), F32),
                        pltpu.SemaphoreType.DMA((2, ne)),
                        pltpu.SemaphoreType.DMA((1,))])
    return pl.pallas_call(
        functools.partial(_combine_kernel, tiles_per_group=tiles_per_group, n_lat_tiles=nl, rows=rows),
        name="combine",
        grid_spec=grid_spec,
        out_shape=[jax.ShapeDtypeStruct((n_lat_tokens, d), F32),
                   jax.ShapeDtypeStruct((n_ctx_tokens, d), F32)],
        compiler_params=_params(1),
    )(tab, x1, mod, slot_n, fng, y)


def kernel(x_prompt, x_sample, c, cache_diff_k, cache_diff_v, cache_gqa_k, cache_gqa_v, c_ctx, norm1_g, norm2_g, w_mod, b_mod, w_in, w_out, diff_lambda_q1, diff_lambda_k1, diff_lambda_q2, diff_lambda_k2, diff_subln_g, gqa_q_norm_g, gqa_k_norm_g, w_router, w_expert_gate, w_expert_up, w_expert_down, final_norm_g):
    nbc, seq, d = x_prompt.shape
    nbl, n_lat, _ = x_sample.shape
    depth, past = cache_diff_k.shape[1], cache_diff_k.shape[2]
    tc, tl = nbc * seq, nbl * n_lat
    assert depth == 1, "single trunk layer"
    assert tc == n_lat, "the context tokens must fill exactly one routing group"
    assert seq == TOK_TILE and nbl + 1 <= 8
    ne = N_EXPERTS
    tpg = n_lat // TOK_TILE
    ng = nbl + 1
    rows = CAPACITY_FACTOR * n_lat // ne

    xs = x_sample.reshape(tl, d)
    xp = x_prompt.reshape(tc, d)

    cond = jnp.concatenate([c, c_ctx[None, :], jnp.zeros((8 - ng, d), F32)], axis=0)
    mod = _mod_call(cond, w_mod[0], b_mod).reshape(8, N_MOD, d)

    tables = (_rope_tables(n_lat, DIFF_QK_DIM, TOK_TILE) + _rope_tables(n_lat, HEAD_DIM, TOK_TILE))
    qkv, ndk, ndv, ngk, ngv = _inproj_call(
        xs, xp, mod, norm1_g, w_in[0].astype(BF16), gqa_q_norm_g, gqa_k_norm_g, tables, n_lat)

    lamv = jnp.concatenate([diff_lambda_q1, diff_lambda_k1, diff_lambda_q2, diff_lambda_k2], axis=0)
    ckd = jnp.transpose(cache_diff_k, (0, 1, 3, 4, 5, 2)).reshape(nbl * N_DIFF_HEADS * LANES, past)
    cvd = cache_diff_v.reshape(nbl * past, N_DIFF_HEADS * LANES)
    ckg = cache_gqa_k.reshape(nbl * past, N_GQA_KV_HEADS * LANES)
    cvg = cache_gqa_v.reshape(nbl * past, N_GQA_KV_HEADS * LANES)
    odl = _attn_diff_call(qkv, ckd, cvd, lamv, diff_subln_g, row0=0, nb=nbl, n=n_lat, tq=1024, hps=1)
    ogl = _attn_gqa_call(qkv, ckg, cvg, row0=0, nb=nbl, n=n_lat, tq=256)
    odc = _attn_diff_call(qkv, None, None, lamv, diff_subln_g, row0=tl, nb=nbc, n=seq, tq=seq,
                          hps=N_DIFF_HEADS)
    ogc = _attn_gqa_call(qkv, None, None, row0=tl, nb=nbc, n=seq, tq=seq)

    x1, h2, aff = _outproj_call(odl, ogl, odc, ogc, xs, xp, mod, norm2_g, w_out[0].astype(BF16),
                                _split3_bf16(w_router[0]), ne, n_lat)
    aff_t = aff[:, :ne].T

    slot_l, starts_l = _route_call(aff_t, col0=0, n_sets=nbl, n=n_lat, offset_stride=0, name="route_lat")
    slot_c, starts_c = _route_call(aff_t, col0=tl, n_sets=nbc, n=seq,
                                   offset_stride=CAPACITY_FACTOR * seq // ne, name="route_ctx")
    slot_t = jnp.concatenate([slot_l, slot_c], axis=1)
    tab_l = starts_l[:, :, :tpg + 1]
    tab_c = jnp.concatenate([starts_c[:, :, 0].T, starts_c[-1:, :, 1].T], axis=1)[None]
    tab = jnp.concatenate([tab_l, tab_c], axis=0).reshape(-1)

    x_slots, gate_slots = _dispatch_call(tab, h2, slot_t, aff_t, ng=ng, rows=rows, tiles_per_group=tpg)
    y = _moe_call(x_slots, gate_slots, w_expert_gate[0], w_expert_up[0], w_expert_down[0])

    ys, yp = _combine_call(tab, x1, mod, slot_t.T, final_norm_g[None, :], y,
                           n_lat_tokens=tl, n_ctx_tokens=tc, tiles_per_group=tpg)

    return (yp.reshape(nbc, seq, d), ys.reshape(nbl, n_lat, d),
            ndk.reshape(nbc, 1, N_DIFF_HEADS, 2, DIFF_QK_DIM, seq).transpose(0, 1, 5, 2, 3, 4),
            ndv.reshape(nbc, 1, seq, N_DIFF_HEADS, HEAD_DIM),
            ngk.reshape(nbc, 1, seq, N_GQA_KV_HEADS, HEAD_DIM),
            ngv.reshape(nbc, 1, seq, N_GQA_KV_HEADS, HEAD_DIM))
```

```python
import functools
import math

import jax
import jax.numpy as jnp
import numpy as np
from jax import lax
from jax.experimental import pallas as pl
from jax.experimental.pallas import tpu as pltpu

F32 = jnp.float32
BF16 = jnp.bfloat16
I32 = jnp.int32

LANES = 128
HEAD_DIM = 128
N_DIFF_HEADS = 8
DIFF_QK_DIM = HEAD_DIM // 2
N_GQA_HEADS = 8
N_GQA_KV_HEADS = 2
GQA_GROUP = N_GQA_HEADS // N_GQA_KV_HEADS
N_EXPERTS = 16
CAPACITY_FACTOR = 2
GRID_W = 64
ROPE_THETA = 10000.0
NORM_EPS = 1e-6
N_MOD = 6
LAMBDA_INIT_L0 = 0.8 - 0.6 * math.exp(-0.3 * 0)

DQ_BLK = 0
DK_BLK = DQ_BLK + N_DIFF_HEADS
DV_BLK = DK_BLK + N_DIFF_HEADS
GQ_BLK = DV_BLK + N_DIFF_HEADS
GK_BLK = GQ_BLK + N_GQA_HEADS
GV_BLK = GK_BLK + N_GQA_KV_HEADS
IN_BLKS = GV_BLK + N_GQA_KV_HEADS

TOK_TILE = 256
OUTPROJ_TILE = 512
OUTPROJ_SPLIT = 2
KV_CHUNK_MAX = 2048
DISPATCH_WIN = 64
NEG_BIG = -1e30
LOG2_E = math.log2(math.e)
MIN_EXP = -150.0
EXP_STEPS = 8
MANTISSA_STEPS = 53

VMEM_LIMIT_MB = 56


def _params(n_axes, vmem_mb=VMEM_LIMIT_MB):
    return pltpu.CompilerParams(dimension_semantics=("arbitrary",) * n_axes,
                                vmem_limit_bytes=vmem_mb << 20)


def _sigmoid(x):
    return 1.0 / (1.0 + jnp.exp(-x))


def _rms(x, eps=NORM_EPS):
    return x * lax.rsqrt(jnp.mean(x * x, axis=-1, keepdims=True) + eps)


def _mod_kernel(c_ref, w_ref, b_ref, o_ref):
    c = c_ref[...]
    a = (c * _sigmoid(c)).astype(BF16)
    o_ref[...] = jnp.dot(a, w_ref[...].astype(BF16), preferred_element_type=F32) + b_ref[...]


def _mod_call(cond, w_mod, b_mod):
    d, m = w_mod.shape
    tn = next(t for t in (1024, 512, 256, LANES) if m % t == 0)
    return pl.pallas_call(
        _mod_kernel, name="mod",
        grid=(m // tn,),
        in_specs=[pl.BlockSpec((8, d), lambda j: (0, 0)),
                  pl.BlockSpec((d, tn), lambda j: (0, j)),
                  pl.BlockSpec((1, tn), lambda j: (0, j))],
        out_specs=pl.BlockSpec((8, tn), lambda j: (0, j)),
        out_shape=jax.ShapeDtypeStruct((8, m), F32),
        compiler_params=_params(1),
    )(cond, w_mod, b_mod)


def _rope(x, cos, sin_signed, half):
    lane = lax.broadcasted_iota(I32, x.shape, 1)
    first = (lane % (2 * half)) < half
    rot = jnp.where(first, pltpu.roll(x, LANES - half, 1), pltpu.roll(x, half, 1))
    return x * cos + rot * sin_signed


def _inproj_kernel(xs_ref, xp_ref, mod_ref, n1g_ref, w_ref, qn_ref, kn_ref,
                   cd_ref, sd_ref, cg_ref, sg_ref,
                   qkv_ref, ndk_ref, ndv_ref, ngk_ref, ngv_ref, *, n_lat_tiles):
    i = pl.program_id(0)
    is_ctx = i >= n_lat_tiles
    x = jnp.where(is_ctx, xp_ref[...], xs_ref[...])
    sh1 = mod_ref[0:1, :]
    sc1 = mod_ref[1:2, :]
    h = (_rms(x) * n1g_ref[...]) * (1.0 + sc1) + sh1
    hb = h.astype(BF16)
    cd, sd, cg, sg = cd_ref[...], sd_ref[...], cg_ref[...], sg_ref[...]
    qn, kn = qn_ref[...], kn_ref[...]
    cache_stores = []

    def seg(blk0, nblk):
        return jnp.dot(hb, w_ref[:, blk0 * LANES:(blk0 + nblk) * LANES], preferred_element_type=F32)

    def put(blk, val):
        qkv_ref[:, blk * LANES:(blk + 1) * LANES] = val.astype(BF16)

    p = seg(DQ_BLK, N_DIFF_HEADS)
    for j in range(N_DIFF_HEADS):
        v = p[:, j * LANES:(j + 1) * LANES]
        put(DQ_BLK + j, _rope(v, cd, sd, DIFF_QK_DIM // 4) * (DIFF_QK_DIM ** -0.5 * LOG2_E))
    p = seg(DK_BLK, N_DIFF_HEADS)
    for j in range(N_DIFF_HEADS):
        v = p[:, j * LANES:(j + 1) * LANES]
        cache_stores.append((ndk_ref, j, v))
        put(DK_BLK + j, _rope(v, cd, sd, DIFF_QK_DIM // 4))
    p = seg(DV_BLK, N_DIFF_HEADS)
    for j in range(N_DIFF_HEADS):
        v = p[:, j * LANES:(j + 1) * LANES]
        cache_stores.append((ndv_ref, j, v))
        put(DV_BLK + j, v)
    p = seg(GQ_BLK, N_GQA_HEADS)
    for j in range(N_GQA_HEADS):
        v = _rms(p[:, j * LANES:(j + 1) * LANES]) * qn
        put(GQ_BLK + j, _rope(v, cg, sg, HEAD_DIM // 4) * (HEAD_DIM ** -0.5 * LOG2_E))
    p = seg(GK_BLK, 2 * N_GQA_KV_HEADS)
    for j in range(N_GQA_KV_HEADS):
        v = _rms(p[:, j * LANES:(j + 1) * LANES]) * kn
        cache_stores.append((ngk_ref, j, v))
        put(GK_BLK + j, _rope(v, cg, sg, HEAD_DIM // 4))
    for j in range(N_GQA_KV_HEADS):
        v = p[:, (N_GQA_KV_HEADS + j) * LANES:(N_GQA_KV_HEADS + j + 1) * LANES]
        cache_stores.append((ngv_ref, j, v))
        put(GV_BLK + j, v)

    @pl.when(is_ctx)
    def _():
        for ref, j, v in cache_stores:
            if ref is ndk_ref:
                ref[j * LANES:(j + 1) * LANES, :] = v.T
            else:
                ref[:, j * LANES:(j + 1) * LANES] = v


def _inproj_call(xs, xp, mod, n1g, w_in_b, qn, kn, tables, n_lat):
    tl, d = xs.shape
    tc = xp.shape[0]
    tm = TOK_TILE
    nl, nc = tl // tm, tc // tm
    tpb = n_lat // tm
    nbl = tl // n_lat
    width = IN_BLKS * LANES

    def tab_spec():
        return pl.BlockSpec((tm, LANES), lambda i: (jnp.where(i < nl, i % tpb, tpb), 0))

    def cache_spec(w):
        return pl.BlockSpec((tm, w), lambda i: (jnp.maximum(i - nl, 0), 0))

    dw, gw = N_DIFF_HEADS * LANES, N_GQA_KV_HEADS * LANES
    return pl.pallas_call(
        functools.partial(_inproj_kernel, n_lat_tiles=nl), name="inproj",
        grid=(nl + nc,),
        in_specs=[pl.BlockSpec((tm, d), lambda i: (jnp.minimum(i, nl - 1), 0)),
                  pl.BlockSpec((tm, d), lambda i: (jnp.maximum(i - nl, 0), 0)),
                  pl.BlockSpec((None, N_MOD, d), lambda i: (jnp.where(i < nl, i // tpb, nbl), 0, 0)),
                  pl.BlockSpec((1, d), lambda i: (0, 0)),
                  pl.BlockSpec((d, width), lambda i: (0, 0), pipeline_mode=pl.Buffered(1)),
                  pl.BlockSpec((1, LANES), lambda i: (0, 0)),
                  pl.BlockSpec((1, LANES), lambda i: (0, 0)),
                  tab_spec(), tab_spec(), tab_spec(), tab_spec()],
        out_specs=[pl.BlockSpec((tm, width), lambda i: (i, 0)),
                   pl.BlockSpec((None, dw, tm), lambda i: (jnp.maximum(i - nl, 0), 0, 0)),
                   cache_spec(dw), cache_spec(gw), cache_spec(gw)],
        out_shape=[jax.ShapeDtypeStruct((tl + tc, width), BF16),
                   jax.ShapeDtypeStruct((nc, dw, tm), F32), jax.ShapeDtypeStruct((tc, dw), F32),
                   jax.ShapeDtypeStruct((tc, gw), F32), jax.ShapeDtypeStruct((tc, gw), F32)],
        compiler_params=_params(1),
    )(xs, xp, mod, n1g, w_in_b, qn, kn, *tables)


def _rope_tables(n, dim, pad_rows):
    t = np.arange(n)
    row = (t // GRID_W).astype(np.float64)
    col = (t % GRID_W).astype(np.float64)
    sec = dim // 2
    inv = ROPE_THETA ** (-np.arange(0, sec, 2, dtype=np.float64) / sec)
    ang = np.stack([row[:, None] * inv, col[:, None] * inv], axis=1)
    ang = np.concatenate([ang, ang], axis=-1).reshape(n, dim)
    sign = np.where((np.arange(dim) % sec) < sec // 2, -1.0, 1.0)
    cos, sin = np.cos(ang), np.sin(ang) * sign
    reps = LANES // dim
    cos, sin = np.tile(cos, (1, reps)), np.tile(sin, (1, reps))
    cos = np.concatenate([cos, np.ones((pad_rows, LANES))], axis=0)
    sin = np.concatenate([sin, np.zeros((pad_rows, LANES))], axis=0)
    return jnp.asarray(cos, F32), jnp.asarray(sin, F32)


def _flash(q, chunks):
    m_rows = q.shape[0]
    m = jnp.full((m_rows, 1), NEG_BIG, F32)
    acc = jnp.zeros((m_rows, HEAD_DIM), F32)
    l = jnp.zeros((m_rows, 1), F32)
    for scores, values in chunks:
        s = scores(q)
        m_new = jnp.maximum(m, jnp.max(s, axis=-1, keepdims=True))
        alpha = jnp.exp2(m - m_new)
        p = jnp.exp2(s - m_new)
        l = alpha * l + jnp.sum(p, axis=-1, keepdims=True)
        acc = alpha * acc + jnp.dot(p.astype(BF16), values(), preferred_element_type=F32)
        m = m_new
    return acc / l


def _scores_nt(k):
    return lambda q: lax.dot_general(q, k(), (((1,), (1,)), ((), ())), preferred_element_type=F32)


def _kv_chunks(k_ref, v_ref, ck_ref, cv_ref, col, cache_k_transposed):
    cs = slice(col * LANES, (col + 1) * LANES)
    chunks = []
    if ck_ref is not None:
        if cache_k_transposed:
            scores = lambda q: jnp.dot(q, ck_ref[cs, :].astype(BF16), preferred_element_type=F32)
        else:
            scores = _scores_nt(lambda: ck_ref[:, cs].astype(BF16))
        chunks.append((scores, lambda: cv_ref[:, cs].astype(BF16)))
    n = k_ref.shape[0]
    n_chunks = pl.cdiv(n, KV_CHUNK_MAX)
    assert n % n_chunks == 0
    step = n // n_chunks
    for c in range(n_chunks):
        rs = slice(c * step, (c + 1) * step)
        chunks.append((_scores_nt(lambda rs=rs: k_ref[rs, cs]), lambda rs=rs: v_ref[rs, cs]))
    return chunks


def _attn_diff_kernel(*refs, hps, has_cache):
    if has_cache:
        q_ref, k_ref, v_ref, ck_ref, cv_ref, lam_ref, g_ref, o_ref = refs
    else:
        q_ref, k_ref, v_ref, lam_ref, g_ref, o_ref = refs
        ck_ref = cv_ref = None
    lv = lam_ref[...]
    lam = (jnp.exp(jnp.sum(lv[0:1] * lv[1:2], axis=-1, keepdims=True))
           - jnp.exp(jnp.sum(lv[2:3] * lv[3:4], axis=-1, keepdims=True)) + LAMBDA_INIT_L0)
    for j in range(hps):
        q = q_ref[:, j * LANES:(j + 1) * LANES]
        lane = lax.broadcasted_iota(I32, q.shape, 1)
        zero = jnp.zeros_like(q)
        chunks = _kv_chunks(k_ref, v_ref, ck_ref, cv_ref, j, cache_k_transposed=True)
        o1 = _flash(jnp.where(lane < DIFF_QK_DIM, q, zero), chunks)
        o2 = _flash(jnp.where(lane >= DIFF_QK_DIM, q, zero), chunks)
        o = o1 - lam * o2
        o = (_rms(o) * g_ref[...]) * (1.0 - LAMBDA_INIT_L0)
        o_ref[:, j * LANES:(j + 1) * LANES] = o.astype(o_ref.dtype)


def _attn_diff_call(qkv, ck, cv, lamv, subg, *, row0, nb, n, tq, hps):
    w = hps * LANES
    hb_n = N_DIFF_HEADS // hps
    qpb = n // tq
    in_specs = [pl.BlockSpec((tq, w), lambda b, h, qi: (row0 // tq + b * qpb + qi, DQ_BLK // hps + h)),
                pl.BlockSpec((n, w), lambda b, h, qi: (row0 // n + b, DK_BLK // hps + h)),
                pl.BlockSpec((n, w), lambda b, h, qi: (row0 // n + b, DV_BLK // hps + h))]
    args = [qkv, qkv, qkv]
    if ck is not None:
        past = ck.shape[1]
        in_specs += [pl.BlockSpec((w, past), lambda b, h, qi: (b * hb_n + h, 0)),
                     pl.BlockSpec((past, w), lambda b, h, qi: (b, h))]
        args += [ck, cv]
    in_specs += [pl.BlockSpec(lamv.shape, lambda b, h, qi: (0, 0)),
                 pl.BlockSpec((1, LANES), lambda b, h, qi: (0, 0))]
    args += [lamv, subg]
    return pl.pallas_call(
        functools.partial(_attn_diff_kernel, hps=hps, has_cache=ck is not None),
        name="attn_diff_lat" if ck is not None else "attn_diff_ctx",
        grid=(nb, hb_n, qpb),
        in_specs=in_specs,
        out_specs=pl.BlockSpec((tq, w), lambda b, h, qi: (b * qpb + qi, h)),
        out_shape=jax.ShapeDtypeStruct((nb * n, N_DIFF_HEADS * LANES), BF16),
        compiler_params=_params(3),
    )(*args)


def _attn_gqa_kernel(*refs, kvps, has_cache):
    if has_cache:
        q_ref, k_ref, v_ref, ck_ref, cv_ref, o_ref = refs
    else:
        q_ref, k_ref, v_ref, o_ref = refs
        ck_ref = cv_ref = None
    tq = q_ref.shape[0]
    for j in range(kvps):
        c0 = j * GQA_GROUP
        q = jnp.concatenate([q_ref[:, (c0 + r) * LANES:(c0 + r + 1) * LANES] for r in range(GQA_GROUP)],
                            axis=0)
        chunks = _kv_chunks(k_ref, v_ref, ck_ref, cv_ref, j, cache_k_transposed=False)
        o = _flash(q, chunks).astype(o_ref.dtype)
        for r in range(GQA_GROUP):
            o_ref[:, (c0 + r) * LANES:(c0 + r + 1) * LANES] = o[r * tq:(r + 1) * tq]


def _attn_gqa_call(qkv, ck, cv, *, row0, nb, n, tq, kvps):
    w = kvps * GQA_GROUP * LANES
    kw = kvps * LANES
    qpb = n // tq
    in_specs = [pl.BlockSpec((tq, w), lambda b, g, qi: (row0 // tq + b * qpb + qi, GQ_BLK * LANES // w + g)),
                pl.BlockSpec((n, kw), lambda b, g, qi: (row0 // n + b, GK_BLK // kvps + g)),
                pl.BlockSpec((n, kw), lambda b, g, qi: (row0 // n + b, GV_BLK // kvps + g))]
    args = [qkv, qkv, qkv]
    if ck is not None:
        past = ck.shape[0] // nb
        in_specs += [pl.BlockSpec((past, kw), lambda b, g, qi: (b, g)),
                     pl.BlockSpec((past, kw), lambda b, g, qi: (b, g))]
        args += [ck, cv]
    return pl.pallas_call(
        functools.partial(_attn_gqa_kernel, kvps=kvps, has_cache=ck is not None),
        name="attn_gqa_lat" if ck is not None else "attn_gqa_ctx",
        grid=(nb, N_GQA_KV_HEADS // kvps, qpb),
        in_specs=in_specs,
        out_specs=pl.BlockSpec((tq, w), lambda b, g, qi: (b * qpb + qi, g)),
        out_shape=jax.ShapeDtypeStruct((nb * n, N_GQA_HEADS * LANES), BF16),
        compiler_params=_params(3),
    )(*args)


def _outproj_kernel(odl_ref, ogl_ref, odc_ref, ogc_ref, xs_ref, xp_ref, mod_ref, n2g_ref, w_ref, wr3_ref,
                    x1_ref, h2_ref, aff_ref, *, n_lat_tiles, ne):
    i = pl.program_id(0)
    is_ctx = i >= n_lat_tiles
    dw = odl_ref.shape[1]
    g1 = mod_ref[2:3, :]
    sh2 = mod_ref[3:4, :]
    sc2 = mod_ref[4:5, :]
    w3 = wr3_ref[...]
    tm = x1_ref.shape[0]
    half = tm // OUTPROJ_SPLIT
    projected = []
    for r in range(OUTPROJ_SPLIT):
        rs = slice(r * half, (r + 1) * half)
        od = jnp.where(is_ctx, odc_ref[rs, :], odl_ref[rs, :])
        og = jnp.where(is_ctx, ogc_ref[rs, :], ogl_ref[rs, :])
        projected.append(jnp.dot(od, w_ref[0:dw, :], preferred_element_type=F32)
                         + jnp.dot(og, w_ref[dw:, :], preferred_element_type=F32))
    for r in range(OUTPROJ_SPLIT):
        rs = slice(r * half, (r + 1) * half)
        x = jnp.where(is_ctx, xp_ref[rs, :], xs_ref[rs, :])
        x1 = x + g1 * projected[r]
        x1_ref[rs, :] = x1
        h2 = (_rms(x1) * n2g_ref[...]) * (1.0 + sc2) + sh2
        h_hi = h2.astype(BF16)
        h2_ref[rs, :] = h_hi
        h_mid = (h2 - h_hi.astype(F32)).astype(BF16)
        t = (jnp.dot(h_mid, w3, preferred_element_type=F32) + jnp.dot(h_hi, w3, preferred_element_type=F32))
        t = (pltpu.roll(t, LANES - 2 * ne, 1) + pltpu.roll(t, LANES - ne, 1)) + t
        lane = lax.broadcasted_iota(I32, t.shape, 1)
        logits = jnp.where(lane < ne, t, NEG_BIG)
        z = jnp.exp(logits - jnp.max(logits, axis=-1, keepdims=True))
        aff_ref[rs, :] = z / jnp.sum(z, axis=-1, keepdims=True)


def _split3_bf16(w):
    hi = w.astype(BF16)
    rem = w - hi.astype(F32)
    mid = rem.astype(BF16)
    lo = (rem - mid.astype(F32)).astype(BF16)
    pad = jnp.zeros((w.shape[0], LANES - 3 * w.shape[1]), BF16)
    return jnp.concatenate([hi, mid, lo, pad], axis=1)


def _outproj_call(odl, ogl, odc, ogc, xs, xp, mod, n2g, w_out_b, wr3, ne, n_lat):
    tl, d = xs.shape
    tc = xp.shape[0]
    tm = OUTPROJ_TILE
    nl, nc = tl // tm, tc // tm
    tpb = n_lat // tm
    nbl = tl // n_lat
    aw = odl.shape[1]

    def lat_spec(w):
        return pl.BlockSpec((tm, w), lambda i: (jnp.minimum(i, nl - 1), 0))

    def ctx_spec(w):
        return pl.BlockSpec((tm, w), lambda i: (jnp.maximum(i - nl, 0), 0))

    return pl.pallas_call(
        functools.partial(_outproj_kernel, n_lat_tiles=nl, ne=ne), name="outproj",
        grid=(nl + nc,),
        in_specs=[lat_spec(aw), lat_spec(aw), ctx_spec(aw), ctx_spec(aw), lat_spec(d), ctx_spec(d),
                  pl.BlockSpec((None, N_MOD, d), lambda i: (jnp.where(i < nl, i // tpb, nbl), 0, 0)),
                  pl.BlockSpec((1, d), lambda i: (0, 0)),
                  pl.BlockSpec(w_out_b.shape, lambda i: (0, 0), pipeline_mode=pl.Buffered(1)),
                  pl.BlockSpec((d, LANES), lambda i: (0, 0))],
        out_specs=[pl.BlockSpec((tm, d), lambda i: (i, 0)),
                   pl.BlockSpec((tm, d), lambda i: (i, 0)),
                   pl.BlockSpec((tm, LANES), lambda i: (i, 0))],
        out_shape=[jax.ShapeDtypeStruct((tl + tc, d), F32),
                   jax.ShapeDtypeStruct((tl + tc, d), BF16),
                   jax.ShapeDtypeStruct((tl + tc, LANES), F32)],
        compiler_params=_params(1),
    )(odl, ogl, odc, ogc, xs, xp, mod, n2g, w_out_b, wr3)


def _route_kernel(aff_ref, slot_ref, starts_ref, *, cap, n, offset_stride):
    ne = aff_ref.shape[0]
    n_sets = aff_ref.shape[1] // n
    tt = TOK_TILE
    aff = jnp.concatenate([aff_ref[:, s * n:(s + 1) * n] for s in range(n_sets)], axis=0)
    capf = float(cap)
    ne_all = n_sets * ne

    def count(mask):
        return jnp.sum(mask.astype(F32), axis=1, keepdims=True)

    def threshold(e):
        return jnp.where(e <= MIN_EXP, 0.0, jnp.exp2(e))

    def exp_step(_, carry):
        e_lo, e_hi = carry
        e_mid = jnp.floor(0.5 * (e_lo + e_hi))
        ok = count(aff >= threshold(e_mid)) >= capf
        return jnp.where(ok, e_mid, e_lo), jnp.where(ok, e_hi, e_mid)

    e_lo, e_hi = lax.fori_loop(0, EXP_STEPS, exp_step,
                               (jnp.full((ne_all, 1), MIN_EXP, F32), jnp.full((ne_all, 1), 1.0, F32)))

    def bisect(_, carry):
        lo, hi = carry
        mid = 0.5 * (lo + hi)
        ok = count(aff >= mid) >= capf
        return jnp.where(ok, mid, lo), jnp.where(ok, hi, mid)

    thr, _ = lax.fori_loop(0, MANTISSA_STEPS, bisect, (threshold(e_lo), threshold(e_hi)))
    gt = aff > thr
    tie = aff == thr
    need = capf - count(gt)

    ri = lax.broadcasted_iota(I32, (tt, tt), 0)
    ci = lax.broadcasted_iota(I32, (tt, tt), 1)
    upper = (ri < ci).astype(BF16)
    set_id = lax.broadcasted_iota(I32, (ne_all, 1), 0) // ne
    tie_pref = jnp.zeros((ne_all, 1), F32)
    sel_pref = (set_id * offset_stride).astype(F32)
    lane = lax.broadcasted_iota(I32, (ne_all, LANES), 1)
    starts = jnp.zeros((ne_all, LANES), F32)
    for t in range(n // tt):
        cs = slice(t * tt, (t + 1) * tt)
        tie_t = tie[:, cs]
        tie_rank = tie_pref + jnp.dot(tie_t.astype(BF16), upper, preferred_element_type=F32)
        sel_t = gt[:, cs] | (tie_t & (tie_rank < need))
        slot_t = sel_pref + jnp.dot(sel_t.astype(BF16), upper, preferred_element_type=F32)
        slot_t = jnp.where(sel_t, slot_t, -1.0).astype(I32)
        for s in range(n_sets):
            slot_ref[:, s * n + t * tt:s * n + (t + 1) * tt] = slot_t[s * ne:(s + 1) * ne]
        starts = jnp.where(lane == t, sel_pref, starts)
        tie_pref = tie_pref + count(tie_t)
        sel_pref = sel_pref + count(sel_t)
    starts = jnp.where(lane == n // tt, sel_pref, starts).astype(I32)
    for s in range(n_sets):
        starts_ref[s] = starts[s * ne:(s + 1) * ne]


def _route_call(aff_t, *, col0, n_sets, n, offset_stride, name):
    ne = aff_t.shape[0]
    cap = CAPACITY_FACTOR * n // N_EXPERTS
    w = n_sets * n
    return pl.pallas_call(
        functools.partial(_route_kernel, cap=cap, n=n, offset_stride=offset_stride), name=name,
        grid=(1,),
        in_specs=[pl.BlockSpec((ne, w), lambda s: (0, col0 // w))],
        out_specs=[pl.BlockSpec((ne, w), lambda s: (0, 0)),
                   pl.BlockSpec((n_sets, ne, LANES), lambda s: (0, 0, 0))],
        out_shape=[jax.ShapeDtypeStruct((ne, w), I32),
                   jax.ShapeDtypeStruct((n_sets, ne, LANES), I32)],
        compiler_params=_params(1),
    )(aff_t)


def _window_start(first_slot, rows):
    return pl.multiple_of(jnp.minimum((first_slot // 16) * 16, rows - DISPATCH_WIN), 16)


def _dispatch_kernel(tab_ref, h2_ref, slot_ref, gate_ref, x_ref, gc_ref, *, tiles_per_group, ne):
    g, eb, t = pl.program_id(0), pl.program_id(1), pl.program_id(2)
    epb, rows = x_ref.shape[0], x_ref.shape[1]
    tt = TOK_TILE
    win = DISPATCH_WIN

    @pl.when(t == 0)
    def _():
        x_ref[...] = jnp.zeros_like(x_ref)
        gc_ref[...] = jnp.zeros_like(gc_ref)

    sub = lax.broadcasted_iota(I32, (win, tt), 0)
    h_t = h2_ref[...]

    def scatter(i, start, hit):
        gwin = jnp.sum(jnp.where(hit, gate_ref[i:i + 1, :], 0.0), axis=1, keepdims=True)
        gc_ref[i, pl.ds(start, win), :] += jnp.broadcast_to(gwin, (win, LANES))

    starts, hits = [], []
    for i in range(epb):
        base = (g * ne + eb * epb + i) * (tiles_per_group + 1) + t
        start = _window_start(tab_ref[base], rows)
        starts.append(start)
        hits.append(slot_ref[i:i + 1, :] - start == sub)
    onehot = jnp.concatenate([h.astype(BF16) for h in hits], axis=0)
    moved = jnp.dot(onehot, h_t, preferred_element_type=F32).astype(BF16)
    for i in range(epb):
        x_ref[i, pl.ds(starts[i], win), :] += moved[i * win:(i + 1) * win]
        scatter(i, starts[i], hits[i])

    for i in range(epb):
        base = (g * ne + eb * epb + i) * (tiles_per_group + 1) + t
        last = tab_ref[base + 1]

        def extra(w, c, i=i, first=starts[i]):
            lo = first + w * win
            start = _window_start(lo, rows)
            srow = slot_ref[i:i + 1, :]
            hit = (srow - start == sub) & (srow >= lo)
            x_ref[i, pl.ds(start, win), :] += jnp.dot(hit.astype(BF16), h_t,
                                                      preferred_element_type=F32).astype(BF16)
            scatter(i, start, hit)
            return c

        lax.fori_loop(1, (last - starts[i] + win - 1) // win, extra, 0)


def _dispatch_call(tab, h2, slot_t, gate_t, *, ng, rows, tiles_per_group):
    t_all, d = h2.shape
    ne = slot_t.shape[0]
    epb = 8
    tt = TOK_TILE

    def tok_spec(w):
        return pl.BlockSpec((w, tt), lambda g, eb, t, tab: (eb, g * tiles_per_group + t))

    grid_spec = pltpu.PrefetchScalarGridSpec(
        num_scalar_prefetch=1,
        grid=(ng, ne // epb, tiles_per_group),
        in_specs=[pl.BlockSpec((tt, d), lambda g, eb, t, tab: (g * tiles_per_group + t, 0)),
                  tok_spec(epb), tok_spec(epb)],
        out_specs=[pl.BlockSpec((None, epb, rows, d), lambda g, eb, t, tab: (g, eb, 0, 0)),
                   pl.BlockSpec((None, epb, rows, LANES), lambda g, eb, t, tab: (g, eb, 0, 0))])
    return pl.pallas_call(
        functools.partial(_dispatch_kernel, tiles_per_group=tiles_per_group, ne=ne), name="dispatch",
        grid_spec=grid_spec,
        out_shape=[jax.ShapeDtypeStruct((ng, ne, rows, d), BF16),
                   jax.ShapeDtypeStruct((ng, ne, rows, LANES), F32)],
        compiler_params=_params(3),
    )(tab, h2, slot_t, gate_t)


def _moe_kernel(x_ref, gc_ref, wg_ref, wu_ref, wd_ref, y_ref, yacc_ref, *, nf):
    f = pl.program_id(1)
    ng = x_ref.shape[0]
    tf = wg_ref.shape[1]

    def partial_outputs():
        w_gu = jnp.concatenate([wg_ref[...].astype(BF16), wu_ref[...].astype(BF16)], axis=1)
        wd = wd_ref[...].astype(BF16)
        for g in range(ng):
            gu = jnp.dot(x_ref[g], w_gu, preferred_element_type=F32)
            gate, up = gu[:, :tf], gu[:, tf:]
            hid = (gate * _sigmoid(gate) * up).astype(BF16)
            yield g, jnp.dot(hid, wd, preferred_element_type=F32)

    def finish(g, total):
        y_ref[g] = (total * gc_ref[g][:, 0:1]).astype(y_ref.dtype)

    if nf == 1:
        for g, part in partial_outputs():
            finish(g, part)
        return

    @pl.when(f == 0)
    def _():
        for g, part in partial_outputs():
            yacc_ref[g] = part

    @pl.when(jnp.logical_and(f > 0, f < nf - 1))
    def _():
        for g, part in partial_outputs():
            yacc_ref[g] += part

    @pl.when(f == nf - 1)
    def _():
        for g, part in partial_outputs():
            finish(g, yacc_ref[g] + part)


def _moe_call(x, gc, w_gate, w_up, w_down):
    ng, ne, rows, d = x.shape
    fdim = w_gate.shape[2]
    tf = min(fdim, 256)
    return pl.pallas_call(
        functools.partial(_moe_kernel, nf=fdim // tf), name="moe",
        grid=(ne, fdim // tf),
        in_specs=[pl.BlockSpec((ng, None, rows, d), lambda e, f: (0, e, 0, 0)),
                  pl.BlockSpec((ng, None, rows, LANES), lambda e, f: (0, e, 0, 0)),
                  pl.BlockSpec((None, d, tf), lambda e, f: (e, 0, f)),
                  pl.BlockSpec((None, d, tf), lambda e, f: (e, 0, f)),
                  pl.BlockSpec((None, tf, d), lambda e, f: (e, f, 0))],
        out_specs=pl.BlockSpec((ng, None, rows, d), lambda e, f: (0, e, 0, 0)),
        out_shape=jax.ShapeDtypeStruct((ng, ne, rows, d), BF16),
        scratch_shapes=[pltpu.VMEM((ng, rows, d), F32)],
        compiler_params=_params(2),
    )(x, gc, w_gate, w_up, w_down)


def _combine_kernel(tab_ref, x1_ref, mod_ref, slot_ref, fng_ref, y_hbm, ys_ref, yp_ref,
                    ybuf, obuf, acc_ref, sem, osem, *, tiles_per_group, n_lat_tiles, rows):
    i = pl.program_id(0)
    n_tiles = pl.num_programs(0)
    ne = sem.shape[1]
    win = DISPATCH_WIN
    tt = TOK_TILE
    per = LANES // win

    def first_slot(tile, e):
        return tab_ref[((tile // tiles_per_group) * ne + e) * (tiles_per_group + 1) + tile % tiles_per_group]

    def window_copies(tile, buf):
        g = tile // tiles_per_group
        return [pltpu.make_async_copy(y_hbm.at[g, e, pl.ds(_window_start(first_slot(tile, e), rows), win), :],
                                      ybuf.at[buf, pl.ds(e * win, win), :], sem.at[buf, e])
                for e in range(ne)]

    buf = i % 2

    @pl.when(i == 0)
    def _():
        for cp in window_copies(i, buf):
            cp.start()

    @pl.when(i + 1 < n_tiles)
    def _():
        for cp in window_copies(i + 1, 1 - buf):
            cp.start()

    starts = [_window_start(first_slot(i, e), rows) for e in range(ne)]
    lane = lax.broadcasted_iota(I32, (tt, LANES), 1)
    blocks = []
    for k in range(ne // per):
        slot = slot_ref[:, k * per:k * per + 1]
        start = starts[k * per]
        for j in range(1, per):
            pick = lane >= j * win
            slot = jnp.where(pick, slot_ref[:, k * per + j:k * per + j + 1], slot)
            start = jnp.where(pick, starts[k * per + j], start)
        blocks.append((slot - start == lane % win).astype(BF16))
    onehot = jnp.concatenate(blocks, axis=1)

    for cp in window_copies(i, buf):
        cp.wait()
    acc_ref[...] = jnp.dot(onehot, ybuf[buf], preferred_element_type=F32)

    lane_w = lax.broadcasted_iota(I32, (tt, win), 1)
    g = i // tiles_per_group
    for e in range(ne):
        last = tab_ref[(g * ne + e) * (tiles_per_group + 1) + i % tiles_per_group + 1]

        def extra(w, c, e=e, first=starts[e]):
            lo = first + w * win
            start = _window_start(lo, rows)
            cp = pltpu.make_async_copy(y_hbm.at[g, e, pl.ds(start, win), :], obuf, osem.at[0])
            cp.start()
            cp.wait()
            slot = slot_ref[:, e:e + 1]
            hit = ((slot - start == lane_w) & (slot >= lo)).astype(BF16)
            acc_ref[...] += jnp.dot(hit, obuf[...], preferred_element_type=F32)
            return c

        lax.fori_loop(1, (last - starts[e] + win - 1) // win, extra, 0)

    g2 = mod_ref[5:6, :]
    x2 = x1_ref[...] + g2 * acc_ref[...]
    y = _rms(x2) * fng_ref[...]

    @pl.when(i < n_lat_tiles)
    def _():
        ys_ref[...] = y

    @pl.when(i >= n_lat_tiles)
    def _():
        yp_ref[...] = y


def _combine_call(tab, x1, mod, slot_n, fng, y, *, n_lat_tokens, n_ctx_tokens, tiles_per_group):
    t_all, d = x1.shape
    tm = TOK_TILE
    nl = n_lat_tokens // tm
    ne = slot_n.shape[1]
    rows = y.shape[2]
    assert LANES % DISPATCH_WIN == 0 and ne % (LANES // DISPATCH_WIN) == 0
    grid_spec = pltpu.PrefetchScalarGridSpec(
        num_scalar_prefetch=1,
        grid=(t_all // tm,),
        in_specs=[pl.BlockSpec((tm, d), lambda i, tab: (i, 0)),
                  pl.BlockSpec((None, N_MOD, d), lambda i, tab: (i // tiles_per_group, 0, 0)),
                  pl.BlockSpec((tm, ne), lambda i, tab: (i, 0)),
                  pl.BlockSpec((1, d), lambda i, tab: (0, 0)),
                  pl.BlockSpec(memory_space=pl.ANY)],
        out_specs=[pl.BlockSpec((tm, d), lambda i, tab: (jnp.minimum(i, nl - 1), 0)),
                   pl.BlockSpec((tm, d), lambda i, tab: (jnp.maximum(i - nl, 0), 0))],
        scratch_shapes=[pltpu.VMEM((2, ne * DISPATCH_WIN, d), BF16),
                        pltpu.VMEM((DISPATCH_WIN, d), BF16),
                        pltpu.VMEM((tm, d), F32),
                        pltpu.SemaphoreType.DMA((2, ne)),
                        pltpu.SemaphoreType.DMA((1,))])
    return pl.pallas_call(
        functools.partial(_combine_kernel, tiles_per_group=tiles_per_group, n_lat_tiles=nl, rows=rows),
        name="combine",
        grid_spec=grid_spec,
        out_shape=[jax.ShapeDtypeStruct((n_lat_tokens, d), F32),
                   jax.ShapeDtypeStruct((n_ctx_tokens, d), F32)],
        compiler_params=_params(1),
    )(tab, x1, mod, slot_n, fng, y)


def kernel(x_prompt, x_sample, c, cache_diff_k, cache_diff_v, cache_gqa_k, cache_gqa_v, c_ctx, norm1_g, norm2_g, w_mod, b_mod, w_in, w_out, diff_lambda_q1, diff_lambda_k1, diff_lambda_q2, diff_lambda_k2, diff_subln_g, gqa_q_norm_g, gqa_k_norm_g, w_router, w_expert_gate, w_expert_up, w_expert_down, final_norm_g):
    nbc, seq, d = x_prompt.shape
    nbl, n_lat, _ = x_sample.shape
    depth, past = cache_diff_k.shape[1], cache_diff_k.shape[2]
    tc, tl = nbc * seq, nbl * n_lat
    assert depth == 1, "single trunk layer"
    assert tc == n_lat, "the context tokens must fill exactly one routing group"
    assert seq == TOK_TILE and nbl + 1 <= 8
    ne = N_EXPERTS
    tpg = n_lat // TOK_TILE
    ng = nbl + 1
    rows = CAPACITY_FACTOR * n_lat // ne

    xs = x_sample.reshape(tl, d)
    xp = x_prompt.reshape(tc, d)

    cond = jnp.concatenate([c, c_ctx[None, :], jnp.zeros((8 - ng, d), F32)], axis=0)
    mod = _mod_call(cond, w_mod[0], b_mod).reshape(8, N_MOD, d)

    tables = (_rope_tables(n_lat, DIFF_QK_DIM, TOK_TILE) + _rope_tables(n_lat, HEAD_DIM, TOK_TILE))
    qkv, ndk, ndv, ngk, ngv = _inproj_call(
        xs, xp, mod, norm1_g, w_in[0].astype(BF16), gqa_q_norm_g, gqa_k_norm_g, tables, n_lat)

    lamv = jnp.concatenate([diff_lambda_q1, diff_lambda_k1, diff_lambda_q2, diff_lambda_k2], axis=0)
    ckd = jnp.transpose(cache_diff_k, (0, 1, 3, 4, 5, 2)).reshape(nbl * N_DIFF_HEADS * LANES, past)
    cvd = cache_diff_v.reshape(nbl * past, N_DIFF_HEADS * LANES)
    ckg = cache_gqa_k.reshape(nbl * past, N_GQA_KV_HEADS * LANES)
    cvg = cache_gqa_v.reshape(nbl * past, N_GQA_KV_HEADS * LANES)
    odl = _attn_diff_call(qkv, ckd, cvd, lamv, diff_subln_g, row0=0, nb=nbl, n=n_lat, tq=1024, hps=1)
    ogl = _attn_gqa_call(qkv, ckg, cvg, row0=0, nb=nbl, n=n_lat, tq=256, kvps=1)
    odc = _attn_diff_call(qkv, None, None, lamv, diff_subln_g, row0=tl, nb=nbc, n=seq, tq=seq,
                          hps=N_DIFF_HEADS)
    ogc = _attn_gqa_call(qkv, None, None, row0=tl, nb=nbc, n=seq, tq=seq, kvps=N_GQA_KV_HEADS)

    x1, h2, aff = _outproj_call(odl, ogl, odc, ogc, xs, xp, mod, norm2_g, w_out[0].astype(BF16),
                                _split3_bf16(w_router[0]), ne, n_lat)
    aff_t = aff[:, :ne].T

    slot_l, starts_l = _route_call(aff_t, col0=0, n_sets=nbl, n=n_lat, offset_stride=0, name="route_lat")
    slot_c, starts_c = _route_call(aff_t, col0=tl, n_sets=nbc, n=seq,
                                   offset_stride=CAPACITY_FACTOR * seq // ne, name="route_ctx")
    slot_t = jnp.concatenate([slot_l, slot_c], axis=1)
    tab_l = starts_l[:, :, :tpg + 1]
    tab_c = jnp.concatenate([starts_c[:, :, 0].T, starts_c[-1:, :, 1].T], axis=1)[None]
    tab = jnp.concatenate([tab_l, tab_c], axis=0).reshape(-1)

    x_slots, gate_slots = _dispatch_call(tab, h2, slot_t, aff_t, ng=ng, rows=rows, tiles_per_group=tpg)
    y = _moe_call(x_slots, gate_slots, w_expert_gate[0], w_expert_up[0], w_expert_down[0])

    ys, yp = _combine_call(tab, x1, mod, slot_t.T, final_norm_g[None, :], y,
                           n_lat_tokens=tl, n_ctx_tokens=tc, tiles_per_group=tpg)

    return (yp.reshape(nbc, seq, d), ys.reshape(nbl, n_lat, d),
            ndk.reshape(nbc, 1, N_DIFF_HEADS, 2, DIFF_QK_DIM, seq).transpose(0, 1, 5, 2, 3, 4),
            ndv.reshape(nbc, 1, seq, N_DIFF_HEADS, HEAD_DIM),
            ngk.reshape(nbc, 1, seq, N_GQA_KV_HEADS, HEAD_DIM),
            ngv.reshape(nbc, 1, seq, N_GQA_KV_HEADS, HEAD_DIM))
```

```python
import functools
import math

import jax
import jax.numpy as jnp
import numpy as np
from jax import lax
from jax.experimental import pallas as pl
from jax.experimental.pallas import tpu as pltpu

F32 = jnp.float32
BF16 = jnp.bfloat16
I32 = jnp.int32

LANES = 128
HEAD_DIM = 128
N_DIFF_HEADS = 8
DIFF_QK_DIM = HEAD_DIM // 2
N_GQA_HEADS = 8
N_GQA_KV_HEADS = 2
GQA_GROUP = N_GQA_HEADS // N_GQA_KV_HEADS
N_EXPERTS = 16
CAPACITY_FACTOR = 2
GRID_W = 64
ROPE_THETA = 10000.0
NORM_EPS = 1e-6
N_MOD = 6
LAMBDA_INIT_L0 = 0.8 - 0.6 * math.exp(-0.3 * 0)

DQ_BLK = 0
DK_BLK = DQ_BLK + N_DIFF_HEADS
DV_BLK = DK_BLK + N_DIFF_HEADS
GQ_BLK = DV_BLK + N_DIFF_HEADS
GK_BLK = GQ_BLK + N_GQA_HEADS
GV_BLK = GK_BLK + N_GQA_KV_HEADS
IN_BLKS = GV_BLK + N_GQA_KV_HEADS

TOK_TILE = 256
OUTPROJ_TILE = 512
OUTPROJ_SPLIT = 2
KV_CHUNK_MAX = 2048
DISPATCH_WIN = 64
NEG_BIG = -1e30
LOG2_E = math.log2(math.e)
MIN_EXP = -150.0
EXP_STEPS = 8
MANTISSA_STEPS = 53

VMEM_LIMIT_MB = 56


def _params(n_axes, vmem_mb=VMEM_LIMIT_MB):
    return pltpu.CompilerParams(dimension_semantics=("arbitrary",) * n_axes,
                                vmem_limit_bytes=vmem_mb << 20)


def _sigmoid(x):
    return 1.0 / (1.0 + jnp.exp(-x))


def _rms(x, eps=NORM_EPS):
    return x * lax.rsqrt(jnp.mean(x * x, axis=-1, keepdims=True) + eps)


def _mod_kernel(c_ref, w_ref, b_ref, o_ref):
    c = c_ref[...]
    a = (c * _sigmoid(c)).astype(BF16)
    o_ref[...] = jnp.dot(a, w_ref[...].astype(BF16), preferred_element_type=F32) + b_ref[...]


def _mod_call(cond, w_mod, b_mod):
    d, m = w_mod.shape
    tn = next(t for t in (1536, 1024, 512, 256, LANES) if m % t == 0)
    return pl.pallas_call(
        _mod_kernel, name="mod",
        grid=(m // tn,),
        in_specs=[pl.BlockSpec((8, d), lambda j: (0, 0)),
                  pl.BlockSpec((d, tn), lambda j: (0, j)),
                  pl.BlockSpec((1, tn), lambda j: (0, j))],
        out_specs=pl.BlockSpec((8, tn), lambda j: (0, j)),
        out_shape=jax.ShapeDtypeStruct((8, m), F32),
        compiler_params=_params(1),
    )(cond, w_mod, b_mod)


def _rope(x, cos, sin_signed, half):
    lane = lax.broadcasted_iota(I32, x.shape, 1)
    first = (lane % (2 * half)) < half
    rot = jnp.where(first, pltpu.roll(x, LANES - half, 1), pltpu.roll(x, half, 1))
    return x * cos + rot * sin_signed


def _inproj_kernel(xs_ref, xp_ref, mod_ref, n1g_ref, w_ref, qn_ref, kn_ref,
                   cd_ref, sd_ref, cg_ref, sg_ref,
                   qkv_ref, ndk_ref, ndv_ref, ngk_ref, ngv_ref, *, n_lat_tiles):
    i = pl.program_id(0)
    is_ctx = i >= n_lat_tiles
    x = jnp.where(is_ctx, xp_ref[...], xs_ref[...])
    sh1 = mod_ref[0:1, :]
    sc1 = mod_ref[1:2, :]
    h = (_rms(x) * n1g_ref[...]) * (1.0 + sc1) + sh1
    hb = h.astype(BF16)
    cd, sd, cg, sg = cd_ref[...], sd_ref[...], cg_ref[...], sg_ref[...]
    qn, kn = qn_ref[...], kn_ref[...]
    cache_stores = []

    def seg(blk0, nblk):
        return jnp.dot(hb, w_ref[:, blk0 * LANES:(blk0 + nblk) * LANES], preferred_element_type=F32)

    def put(blk, val):
        qkv_ref[:, blk * LANES:(blk + 1) * LANES] = val.astype(BF16)

    p = seg(DQ_BLK, N_DIFF_HEADS)
    for j in range(N_DIFF_HEADS):
        v = p[:, j * LANES:(j + 1) * LANES]
        put(DQ_BLK + j, _rope(v, cd, sd, DIFF_QK_DIM // 4) * (DIFF_QK_DIM ** -0.5 * LOG2_E))
    p = seg(DK_BLK, N_DIFF_HEADS)
    for j in range(N_DIFF_HEADS):
        v = p[:, j * LANES:(j + 1) * LANES]
        cache_stores.append((ndk_ref, j, v))
        put(DK_BLK + j, _rope(v, cd, sd, DIFF_QK_DIM // 4))
    p = seg(DV_BLK, N_DIFF_HEADS)
    for j in range(N_DIFF_HEADS):
        v = p[:, j * LANES:(j + 1) * LANES]
        cache_stores.append((ndv_ref, j, v))
        put(DV_BLK + j, v)
    p = seg(GQ_BLK, N_GQA_HEADS)
    for j in range(N_GQA_HEADS):
        v = _rms(p[:, j * LANES:(j + 1) * LANES]) * qn
        put(GQ_BLK + j, _rope(v, cg, sg, HEAD_DIM // 4) * (HEAD_DIM ** -0.5 * LOG2_E))
    p = seg(GK_BLK, 2 * N_GQA_KV_HEADS)
    for j in range(N_GQA_KV_HEADS):
        v = _rms(p[:, j * LANES:(j + 1) * LANES]) * kn
        cache_stores.append((ngk_ref, j, v))
        put(GK_BLK + j, _rope(v, cg, sg, HEAD_DIM // 4))
    for j in range(N_GQA_KV_HEADS):
        v = p[:, (N_GQA_KV_HEADS + j) * LANES:(N_GQA_KV_HEADS + j + 1) * LANES]
        cache_stores.append((ngv_ref, j, v))
        put(GV_BLK + j, v)

    @pl.when(is_ctx)
    def _():
        for ref, j, v in cache_stores:
            if ref is ndk_ref:
                ref[j * LANES:(j + 1) * LANES, :] = v.T
            else:
                ref[:, j * LANES:(j + 1) * LANES] = v


def _inproj_call(xs, xp, mod, n1g, w_in_b, qn, kn, tables, n_lat):
    tl, d = xs.shape
    tc = xp.shape[0]
    tm = TOK_TILE
    nl, nc = tl // tm, tc // tm
    tpb = n_lat // tm
    nbl = tl // n_lat
    width = IN_BLKS * LANES

    def tab_spec():
        return pl.BlockSpec((tm, LANES), lambda i: (jnp.where(i < nl, i % tpb, tpb), 0))

    def cache_spec(w):
        return pl.BlockSpec((tm, w), lambda i: (jnp.maximum(i - nl, 0), 0))

    dw, gw = N_DIFF_HEADS * LANES, N_GQA_KV_HEADS * LANES
    return pl.pallas_call(
        functools.partial(_inproj_kernel, n_lat_tiles=nl), name="inproj",
        grid=(nl + nc,),
        in_specs=[pl.BlockSpec((tm, d), lambda i: (jnp.minimum(i, nl - 1), 0)),
                  pl.BlockSpec((tm, d), lambda i: (jnp.maximum(i - nl, 0), 0)),
                  pl.BlockSpec((None, N_MOD, d), lambda i: (jnp.where(i < nl, i // tpb, nbl), 0, 0)),
                  pl.BlockSpec((1, d), lambda i: (0, 0)),
                  pl.BlockSpec((d, width), lambda i: (0, 0), pipeline_mode=pl.Buffered(1)),
                  pl.BlockSpec((1, LANES), lambda i: (0, 0)),
                  pl.BlockSpec((1, LANES), lambda i: (0, 0)),
                  tab_spec(), tab_spec(), tab_spec(), tab_spec()],
        out_specs=[pl.BlockSpec((tm, width), lambda i: (i, 0)),
                   pl.BlockSpec((None, dw, tm), lambda i: (jnp.maximum(i - nl, 0), 0, 0)),
                   cache_spec(dw), cache_spec(gw), cache_spec(gw)],
        out_shape=[jax.ShapeDtypeStruct((tl + tc, width), BF16),
                   jax.ShapeDtypeStruct((nc, dw, tm), F32), jax.ShapeDtypeStruct((tc, dw), F32),
                   jax.ShapeDtypeStruct((tc, gw), F32), jax.ShapeDtypeStruct((tc, gw), F32)],
        compiler_params=_params(1),
    )(xs, xp, mod, n1g, w_in_b, qn, kn, *tables)


def _rope_tables(n, dim, pad_rows):
    t = np.arange(n)
    row = (t // GRID_W).astype(np.float64)
    col = (t % GRID_W).astype(np.float64)
    sec = dim // 2
    inv = ROPE_THETA ** (-np.arange(0, sec, 2, dtype=np.float64) / sec)
    ang = np.stack([row[:, None] * inv, col[:, None] * inv], axis=1)
    ang = np.concatenate([ang, ang], axis=-1).reshape(n, dim)
    sign = np.where((np.arange(dim) % sec) < sec // 2, -1.0, 1.0)
    cos, sin = np.cos(ang), np.sin(ang) * sign
    reps = LANES // dim
    cos, sin = np.tile(cos, (1, reps)), np.tile(sin, (1, reps))
    cos = np.concatenate([cos, np.ones((pad_rows, LANES))], axis=0)
    sin = np.concatenate([sin, np.zeros((pad_rows, LANES))], axis=0)
    return jnp.asarray(cos, F32), jnp.asarray(sin, F32)


def _flash(q, chunks):
    m_rows = q.shape[0]
    m = jnp.full((m_rows, 1), NEG_BIG, F32)
    acc = jnp.zeros((m_rows, HEAD_DIM), F32)
    l = jnp.zeros((m_rows, 1), F32)
    for scores, values in chunks:
        s = scores(q)
        m_new = jnp.maximum(m, jnp.max(s, axis=-1, keepdims=True))
        alpha = jnp.exp2(m - m_new)
        p = jnp.exp2(s - m_new)
        l = alpha * l + jnp.sum(p, axis=-1, keepdims=True)
        acc = alpha * acc + jnp.dot(p.astype(BF16), values(), preferred_element_type=F32)
        m = m_new
    return acc / l


def _scores_nt(k):
    return lambda q: lax.dot_general(q, k(), (((1,), (1,)), ((), ())), preferred_element_type=F32)


def _kv_chunks(k_ref, v_ref, ck_ref, cv_ref, col, cache_k_transposed):
    cs = slice(col * LANES, (col + 1) * LANES)
    chunks = []
    if ck_ref is not None:
        if cache_k_transposed:
            scores = lambda q: jnp.dot(q, ck_ref[cs, :].astype(BF16), preferred_element_type=F32)
        else:
            scores = _scores_nt(lambda: ck_ref[:, cs].astype(BF16))
        chunks.append((scores, lambda: cv_ref[:, cs].astype(BF16)))
    n = k_ref.shape[0]
    n_chunks = pl.cdiv(n, KV_CHUNK_MAX)
    assert n % n_chunks == 0
    step = n // n_chunks
    for c in range(n_chunks):
        rs = slice(c * step, (c + 1) * step)
        chunks.append((_scores_nt(lambda rs=rs: k_ref[rs, cs]), lambda rs=rs: v_ref[rs, cs]))
    return chunks


def _attn_diff_kernel(*refs, hps, has_cache):
    if has_cache:
        q_ref, k_ref, v_ref, ck_ref, cv_ref, lam_ref, g_ref, o_ref = refs
    else:
        q_ref, k_ref, v_ref, lam_ref, g_ref, o_ref = refs
        ck_ref = cv_ref = None
    lv = lam_ref[...]
    lam = (jnp.exp(jnp.sum(lv[0:1] * lv[1:2], axis=-1, keepdims=True))
           - jnp.exp(jnp.sum(lv[2:3] * lv[3:4], axis=-1, keepdims=True)) + LAMBDA_INIT_L0)
    for j in range(hps):
        q = q_ref[:, j * LANES:(j + 1) * LANES]
        lane = lax.broadcasted_iota(I32, q.shape, 1)
        zero = jnp.zeros_like(q)
        chunks = _kv_chunks(k_ref, v_ref, ck_ref, cv_ref, j, cache_k_transposed=True)
        o1 = _flash(jnp.where(lane < DIFF_QK_DIM, q, zero), chunks)
        o2 = _flash(jnp.where(lane >= DIFF_QK_DIM, q, zero), chunks)
        o = o1 - lam * o2
        o = (_rms(o) * g_ref[...]) * (1.0 - LAMBDA_INIT_L0)
        o_ref[:, j * LANES:(j + 1) * LANES] = o.astype(o_ref.dtype)


def _attn_diff_call(qkv, ck, cv, lamv, subg, *, row0, nb, n, tq, hps):
    w = hps * LANES
    hb_n = N_DIFF_HEADS // hps
    qpb = n // tq
    in_specs = [pl.BlockSpec((tq, w), lambda b, h, qi: (row0 // tq + b * qpb + qi, DQ_BLK // hps + h)),
                pl.BlockSpec((n, w), lambda b, h, qi: (row0 // n + b, DK_BLK // hps + h)),
                pl.BlockSpec((n, w), lambda b, h, qi: (row0 // n + b, DV_BLK // hps + h))]
    args = [qkv, qkv, qkv]
    if ck is not None:
        past = ck.shape[1]
        in_specs += [pl.BlockSpec((w, past), lambda b, h, qi: (b * hb_n + h, 0)),
                     pl.BlockSpec((past, w), lambda b, h, qi: (b, h))]
        args += [ck, cv]
    in_specs += [pl.BlockSpec(lamv.shape, lambda b, h, qi: (0, 0)),
                 pl.BlockSpec((1, LANES), lambda b, h, qi: (0, 0))]
    args += [lamv, subg]
    return pl.pallas_call(
        functools.partial(_attn_diff_kernel, hps=hps, has_cache=ck is not None),
        name="attn_diff_lat" if ck is not None else "attn_diff_ctx",
        grid=(nb, hb_n, qpb),
        in_specs=in_specs,
        out_specs=pl.BlockSpec((tq, w), lambda b, h, qi: (b * qpb + qi, h)),
        out_shape=jax.ShapeDtypeStruct((nb * n, N_DIFF_HEADS * LANES), BF16),
        compiler_params=_params(3),
    )(*args)


def _attn_gqa_kernel(*refs, kvps, has_cache):
    if has_cache:
        q_ref, k_ref, v_ref, ck_ref, cv_ref, o_ref = refs
    else:
        q_ref, k_ref, v_ref, o_ref = refs
        ck_ref = cv_ref = None
    tq = q_ref.shape[0]
    for j in range(kvps):
        c0 = j * GQA_GROUP
        q = jnp.concatenate([q_ref[:, (c0 + r) * LANES:(c0 + r + 1) * LANES] for r in range(GQA_GROUP)],
                            axis=0)
        chunks = _kv_chunks(k_ref, v_ref, ck_ref, cv_ref, j, cache_k_transposed=False)
        o = _flash(q, chunks).astype(o_ref.dtype)
        for r in range(GQA_GROUP):
            o_ref[:, (c0 + r) * LANES:(c0 + r + 1) * LANES] = o[r * tq:(r + 1) * tq]


def _attn_gqa_call(qkv, ck, cv, *, row0, nb, n, tq, kvps):
    w = kvps * GQA_GROUP * LANES
    kw = kvps * LANES
    qpb = n // tq
    in_specs = [pl.BlockSpec((tq, w), lambda b, g, qi: (row0 // tq + b * qpb + qi, GQ_BLK * LANES // w + g)),
                pl.BlockSpec((n, kw), lambda b, g, qi: (row0 // n + b, GK_BLK // kvps + g)),
                pl.BlockSpec((n, kw), lambda b, g, qi: (row0 // n + b, GV_BLK // kvps + g))]
    args = [qkv, qkv, qkv]
    if ck is not None:
        past = ck.shape[0] // nb
        in_specs += [pl.BlockSpec((past, kw), lambda b, g, qi: (b, g)),
                     pl.BlockSpec((past, kw), lambda b, g, qi: (b, g))]
        args += [ck, cv]
    return pl.pallas_call(
        functools.partial(_attn_gqa_kernel, kvps=kvps, has_cache=ck is not None),
        name="attn_gqa_lat" if ck is not None else "attn_gqa_ctx",
        grid=(nb, N_GQA_KV_HEADS // kvps, qpb),
        in_specs=in_specs,
        out_specs=pl.BlockSpec((tq, w), lambda b, g, qi: (b * qpb + qi, g)),
        out_shape=jax.ShapeDtypeStruct((nb * n, N_GQA_HEADS * LANES), BF16),
        compiler_params=_params(3),
    )(*args)


def _outproj_kernel(odl_ref, ogl_ref, odc_ref, ogc_ref, xs_ref, xp_ref, mod_ref, n2g_ref, w_ref, wr3_ref,
                    x1_ref, h2_ref, aff_ref, *, n_lat_tiles, ne):
    i = pl.program_id(0)
    is_ctx = i >= n_lat_tiles
    dw = odl_ref.shape[1]
    g1 = mod_ref[2:3, :]
    sh2 = mod_ref[3:4, :]
    sc2 = mod_ref[4:5, :]
    w3 = wr3_ref[...]
    tm = x1_ref.shape[0]
    half = tm // OUTPROJ_SPLIT
    projected = []
    for r in range(OUTPROJ_SPLIT):
        rs = slice(r * half, (r + 1) * half)
        od = jnp.where(is_ctx, odc_ref[rs, :], odl_ref[rs, :])
        og = jnp.where(is_ctx, ogc_ref[rs, :], ogl_ref[rs, :])
        projected.append(jnp.dot(od, w_ref[0:dw, :], preferred_element_type=F32)
                         + jnp.dot(og, w_ref[dw:, :], preferred_element_type=F32))
    for r in range(OUTPROJ_SPLIT):
        rs = slice(r * half, (r + 1) * half)
        x = jnp.where(is_ctx, xp_ref[rs, :], xs_ref[rs, :])
        x1 = x + g1 * projected[r]
        x1_ref[rs, :] = x1
        h2 = (_rms(x1) * n2g_ref[...]) * (1.0 + sc2) + sh2
        h_hi = h2.astype(BF16)
        h2_ref[rs, :] = h_hi
        h_mid = (h2 - h_hi.astype(F32)).astype(BF16)
        t = (jnp.dot(h_mid, w3, preferred_element_type=F32) + jnp.dot(h_hi, w3, preferred_element_type=F32))
        t = (pltpu.roll(t, LANES - 2 * ne, 1) + pltpu.roll(t, LANES - ne, 1)) + t
        lane = lax.broadcasted_iota(I32, t.shape, 1)
        logits = jnp.where(lane < ne, t, NEG_BIG)
        z = jnp.exp(logits - jnp.max(logits, axis=-1, keepdims=True))
        aff_ref[rs, :] = z / jnp.sum(z, axis=-1, keepdims=True)


def _split3_bf16(w):
    hi = w.astype(BF16)
    rem = w - hi.astype(F32)
    mid = rem.astype(BF16)
    lo = (rem - mid.astype(F32)).astype(BF16)
    pad = jnp.zeros((w.shape[0], LANES - 3 * w.shape[1]), BF16)
    return jnp.concatenate([hi, mid, lo, pad], axis=1)


def _outproj_call(odl, ogl, odc, ogc, xs, xp, mod, n2g, w_out_b, wr3, ne, n_lat):
    tl, d = xs.shape
    tc = xp.shape[0]
    tm = OUTPROJ_TILE
    nl, nc = tl // tm, tc // tm
    tpb = n_lat // tm
    nbl = tl // n_lat
    aw = odl.shape[1]

    def lat_spec(w):
        return pl.BlockSpec((tm, w), lambda i: (jnp.minimum(i, nl - 1), 0))

    def ctx_spec(w):
        return pl.BlockSpec((tm, w), lambda i: (jnp.maximum(i - nl, 0), 0))

    return pl.pallas_call(
        functools.partial(_outproj_kernel, n_lat_tiles=nl, ne=ne), name="outproj",
        grid=(nl + nc,),
        in_specs=[lat_spec(aw), lat_spec(aw), ctx_spec(aw), ctx_spec(aw), lat_spec(d), ctx_spec(d),
                  pl.BlockSpec((None, N_MOD, d), lambda i: (jnp.where(i < nl, i // tpb, nbl), 0, 0)),
                  pl.BlockSpec((1, d), lambda i: (0, 0)),
                  pl.BlockSpec(w_out_b.shape, lambda i: (0, 0), pipeline_mode=pl.Buffered(1)),
                  pl.BlockSpec((d, LANES), lambda i: (0, 0))],
        out_specs=[pl.BlockSpec((tm, d), lambda i: (i, 0)),
                   pl.BlockSpec((tm, d), lambda i: (i, 0)),
                   pl.BlockSpec((tm, LANES), lambda i: (i, 0))],
        out_shape=[jax.ShapeDtypeStruct((tl + tc, d), F32),
                   jax.ShapeDtypeStruct((tl + tc, d), BF16),
                   jax.ShapeDtypeStruct((tl + tc, LANES), F32)],
        compiler_params=_params(1),
    )(odl, ogl, odc, ogc, xs, xp, mod, n2g, w_out_b, wr3)


def _route_kernel(aff_ref, slot_ref, starts_ref, *, cap, n, offset_stride):
    ne = aff_ref.shape[0]
    n_sets = aff_ref.shape[1] // n
    tt = TOK_TILE
    aff = jnp.concatenate([aff_ref[:, s * n:(s + 1) * n] for s in range(n_sets)], axis=0)
    capf = float(cap)
    ne_all = n_sets * ne

    def count(mask):
        return jnp.sum(mask.astype(F32), axis=1, keepdims=True)

    def threshold(e):
        return jnp.where(e <= MIN_EXP, 0.0, jnp.exp2(e))

    def exp_step(_, carry):
        e_lo, e_hi = carry
        e_mid = jnp.floor(0.5 * (e_lo + e_hi))
        ok = count(aff >= threshold(e_mid)) >= capf
        return jnp.where(ok, e_mid, e_lo), jnp.where(ok, e_hi, e_mid)

    e_lo, e_hi = lax.fori_loop(0, EXP_STEPS, exp_step,
                               (jnp.full((ne_all, 1), MIN_EXP, F32), jnp.full((ne_all, 1), 1.0, F32)))

    def bisect(_, carry):
        lo, hi = carry
        mid = 0.5 * (lo + hi)
        ok = count(aff >= mid) >= capf
        return jnp.where(ok, mid, lo), jnp.where(ok, hi, mid)

    thr, _ = lax.fori_loop(0, MANTISSA_STEPS, bisect, (threshold(e_lo), threshold(e_hi)))
    gt = aff > thr
    tie = aff == thr
    need = capf - count(gt)

    ri = lax.broadcasted_iota(I32, (tt, tt), 0)
    ci = lax.broadcasted_iota(I32, (tt, tt), 1)
    upper = (ri < ci).astype(BF16)
    set_id = lax.broadcasted_iota(I32, (ne_all, 1), 0) // ne
    tie_pref = jnp.zeros((ne_all, 1), F32)
    sel_pref = (set_id * offset_stride).astype(F32)
    lane = lax.broadcasted_iota(I32, (ne_all, LANES), 1)
    starts = jnp.zeros((ne_all, LANES), F32)
    for t in range(n // tt):
        cs = slice(t * tt, (t + 1) * tt)
        tie_t = tie[:, cs]
        tie_rank = tie_pref + jnp.dot(tie_t.astype(BF16), upper, preferred_element_type=F32)
        sel_t = gt[:, cs] | (tie_t & (tie_rank < need))
        slot_t = sel_pref + jnp.dot(sel_t.astype(BF16), upper, preferred_element_type=F32)
        slot_t = jnp.where(sel_t, slot_t, -1.0).astype(I32)
        for s in range(n_sets):
            slot_ref[:, s * n + t * tt:s * n + (t + 1) * tt] = slot_t[s * ne:(s + 1) * ne]
        starts = jnp.where(lane == t, sel_pref, starts)
        tie_pref = tie_pref + count(tie_t)
        sel_pref = sel_pref + count(sel_t)
    starts = jnp.where(lane == n // tt, sel_pref, starts).astype(I32)
    for s in range(n_sets):
        starts_ref[s] = starts[s * ne:(s + 1) * ne]


def _route_call(aff_t, *, col0, n_sets, n, offset_stride, name):
    ne = aff_t.shape[0]
    cap = CAPACITY_FACTOR * n // N_EXPERTS
    w = n_sets * n
    return pl.pallas_call(
        functools.partial(_route_kernel, cap=cap, n=n, offset_stride=offset_stride), name=name,
        grid=(1,),
        in_specs=[pl.BlockSpec((ne, w), lambda s: (0, col0 // w))],
        out_specs=[pl.BlockSpec((ne, w), lambda s: (0, 0)),
                   pl.BlockSpec((n_sets, ne, LANES), lambda s: (0, 0, 0))],
        out_shape=[jax.ShapeDtypeStruct((ne, w), I32),
                   jax.ShapeDtypeStruct((n_sets, ne, LANES), I32)],
        compiler_params=_params(1),
    )(aff_t)


def _window_start(first_slot, rows):
    return pl.multiple_of(jnp.minimum((first_slot // 16) * 16, rows - DISPATCH_WIN), 16)


def _dispatch_kernel(tab_ref, h2_ref, slot_ref, gate_ref, x_ref, gc_ref, *, tiles_per_group, ne):
    g, eb, t = pl.program_id(0), pl.program_id(1), pl.program_id(2)
    epb, rows = x_ref.shape[0], x_ref.shape[1]
    tt = TOK_TILE
    win = DISPATCH_WIN

    @pl.when(t == 0)
    def _():
        x_ref[...] = jnp.zeros_like(x_ref)
        gc_ref[...] = jnp.zeros_like(gc_ref)

    sub = lax.broadcasted_iota(I32, (win, tt), 0)
    h_t = h2_ref[...]

    def scatter(i, start, hit):
        gwin = jnp.sum(jnp.where(hit, gate_ref[i:i + 1, :], 0.0), axis=1, keepdims=True)
        gc_ref[i, pl.ds(start, win), :] += jnp.broadcast_to(gwin, (win, LANES))

    starts, hits = [], []
    for i in range(epb):
        base = (g * ne + eb * epb + i) * (tiles_per_group + 1) + t
        start = _window_start(tab_ref[base], rows)
        starts.append(start)
        hits.append(slot_ref[i:i + 1, :] - start == sub)
    onehot = jnp.concatenate([h.astype(BF16) for h in hits], axis=0)
    moved = jnp.dot(onehot, h_t, preferred_element_type=F32).astype(BF16)
    for i in range(epb):
        x_ref[i, pl.ds(starts[i], win), :] += moved[i * win:(i + 1) * win]
        scatter(i, starts[i], hits[i])

    spans = [tab_ref[(g * ne + eb * epb + i) * (tiles_per_group + 1) + t + 1] - starts[i] for i in range(epb)]

    @pl.when(functools.reduce(jnp.logical_or, [span > win for span in spans]))
    def _():
        for i in range(epb):
            def extra(w, c, i=i, first=starts[i]):
                lo = first + w * win
                start = _window_start(lo, rows)
                srow = slot_ref[i:i + 1, :]
                hit = (srow - start == sub) & (srow >= lo)
                x_ref[i, pl.ds(start, win), :] += jnp.dot(hit.astype(BF16), h_t,
                                                          preferred_element_type=F32).astype(BF16)
                scatter(i, start, hit)
                return c

            lax.fori_loop(1, (spans[i] + win - 1) // win, extra, 0)


def _dispatch_call(tab, h2, slot_t, gate_t, *, ng, rows, tiles_per_group):
    t_all, d = h2.shape
    ne = slot_t.shape[0]
    epb = 8
    tt = TOK_TILE

    def tok_spec(w):
        return pl.BlockSpec((w, tt), lambda g, eb, t, tab: (eb, g * tiles_per_group + t))

    grid_spec = pltpu.PrefetchScalarGridSpec(
        num_scalar_prefetch=1,
        grid=(ng, ne // epb, tiles_per_group),
        in_specs=[pl.BlockSpec((tt, d), lambda g, eb, t, tab: (g * tiles_per_group + t, 0)),
                  tok_spec(epb), tok_spec(epb)],
        out_specs=[pl.BlockSpec((None, epb, rows, d), lambda g, eb, t, tab: (g, eb, 0, 0)),
                   pl.BlockSpec((None, epb, rows, LANES), lambda g, eb, t, tab: (g, eb, 0, 0))])
    return pl.pallas_call(
        functools.partial(_dispatch_kernel, tiles_per_group=tiles_per_group, ne=ne), name="dispatch",
        grid_spec=grid_spec,
        out_shape=[jax.ShapeDtypeStruct((ng, ne, rows, d), BF16),
                   jax.ShapeDtypeStruct((ng, ne, rows, LANES), F32)],
        compiler_params=_params(3),
    )(tab, h2, slot_t, gate_t)


def _moe_kernel(x_ref, gc_ref, wg_ref, wu_ref, wd_ref, y_ref, yacc_ref, *, nf):
    f = pl.program_id(1)
    ng = x_ref.shape[0]
    tf = wg_ref.shape[1]

    def partial_outputs():
        w_gu = jnp.concatenate([wg_ref[...].astype(BF16), wu_ref[...].astype(BF16)], axis=1)
        wd = wd_ref[...].astype(BF16)
        for g in range(ng):
            gu = jnp.dot(x_ref[g], w_gu, preferred_element_type=F32)
            gate, up = gu[:, :tf], gu[:, tf:]
            hid = (gate * _sigmoid(gate) * up).astype(BF16)
            yield g, jnp.dot(hid, wd, preferred_element_type=F32)

    def finish(g, total):
        y_ref[g] = (total * gc_ref[g][:, 0:1]).astype(y_ref.dtype)

    if nf == 1:
        for g, part in partial_outputs():
            finish(g, part)
        return

    @pl.when(f == 0)
    def _():
        for g, part in partial_outputs():
            yacc_ref[g] = part

    @pl.when(jnp.logical_and(f > 0, f < nf - 1))
    def _():
        for g, part in partial_outputs():
            yacc_ref[g] += part

    @pl.when(f == nf - 1)
    def _():
        for g, part in partial_outputs():
            finish(g, yacc_ref[g] + part)


def _moe_call(x, gc, w_gate, w_up, w_down):
    ng, ne, rows, d = x.shape
    fdim = w_gate.shape[2]
    tf = min(fdim, 256)
    return pl.pallas_call(
        functools.partial(_moe_kernel, nf=fdim // tf), name="moe",
        grid=(ne, fdim // tf),
        in_specs=[pl.BlockSpec((ng, None, rows, d), lambda e, f: (0, e, 0, 0)),
                  pl.BlockSpec((ng, None, rows, LANES), lambda e, f: (0, e, 0, 0)),
                  pl.BlockSpec((None, d, tf), lambda e, f: (e, 0, f)),
                  pl.BlockSpec((None, d, tf), lambda e, f: (e, 0, f)),
                  pl.BlockSpec((None, tf, d), lambda e, f: (e, f, 0))],
        out_specs=pl.BlockSpec((ng, None, rows, d), lambda e, f: (0, e, 0, 0)),
        out_shape=jax.ShapeDtypeStruct((ng, ne, rows, d), BF16),
        scratch_shapes=[pltpu.VMEM((ng, rows, d), F32)],
        compiler_params=_params(2),
    )(x, gc, w_gate, w_up, w_down)


def _combine_kernel(tab_ref, x1_ref, mod_ref, slot_ref, fng_ref, y_hbm, ys_ref, yp_ref,
                    ybuf, obuf, acc_ref, sem, osem, *, tiles_per_group, n_lat_tiles, rows):
    i = pl.program_id(0)
    n_tiles = pl.num_programs(0)
    ne = sem.shape[1]
    win = DISPATCH_WIN
    tt = TOK_TILE
    per = LANES // win

    def first_slot(tile, e):
        return tab_ref[((tile // tiles_per_group) * ne + e) * (tiles_per_group + 1) + tile % tiles_per_group]

    def window_copies(tile, buf):
        g = tile // tiles_per_group
        return [pltpu.make_async_copy(y_hbm.at[g, e, pl.ds(_window_start(first_slot(tile, e), rows), win), :],
                                      ybuf.at[buf, pl.ds(e * win, win), :], sem.at[buf, e])
                for e in range(ne)]

    buf = i % 2

    @pl.when(i == 0)
    def _():
        for cp in window_copies(i, buf):
            cp.start()

    @pl.when(i + 1 < n_tiles)
    def _():
        for cp in window_copies(i + 1, 1 - buf):
            cp.start()

    starts = [_window_start(first_slot(i, e), rows) for e in range(ne)]
    lane = lax.broadcasted_iota(I32, (tt, LANES), 1)
    blocks = []
    for k in range(ne // per):
        slot = slot_ref[:, k * per:k * per + 1]
        start = starts[k * per]
        for j in range(1, per):
            pick = lane >= j * win
            slot = jnp.where(pick, slot_ref[:, k * per + j:k * per + j + 1], slot)
            start = jnp.where(pick, starts[k * per + j], start)
        blocks.append((slot - start == lane % win).astype(BF16))
    onehot = jnp.concatenate(blocks, axis=1)

    for cp in window_copies(i, buf):
        cp.wait()
    acc_ref[...] = jnp.dot(onehot, ybuf[buf], preferred_element_type=F32)

    g = i // tiles_per_group
    spans = [tab_ref[(g * ne + e) * (tiles_per_group + 1) + i % tiles_per_group + 1] - starts[e]
             for e in range(ne)]

    @pl.when(functools.reduce(jnp.logical_or, [span > win for span in spans]))
    def _():
        lane_w = lax.broadcasted_iota(I32, (tt, win), 1)
        for e in range(ne):
            def extra(w, c, e=e, first=starts[e]):
                lo = first + w * win
                start = _window_start(lo, rows)
                cp = pltpu.make_async_copy(y_hbm.at[g, e, pl.ds(start, win), :], obuf, osem.at[0])
                cp.start()
                cp.wait()
                slot = slot_ref[:, e:e + 1]
                hit = ((slot - start == lane_w) & (slot >= lo)).astype(BF16)
                acc_ref[...] += jnp.dot(hit, obuf[...], preferred_element_type=F32)
                return c

            lax.fori_loop(1, (spans[e] + win - 1) // win, extra, 0)

    g2 = mod_ref[5:6, :]
    x2 = x1_ref[...] + g2 * acc_ref[...]
    y = _rms(x2) * fng_ref[...]

    @pl.when(i < n_lat_tiles)
    def _():
        ys_ref[...] = y

    @pl.when(i >= n_lat_tiles)
    def _():
        yp_ref[...] = y


def _combine_call(tab, x1, mod, slot_n, fng, y, *, n_lat_tokens, n_ctx_tokens, tiles_per_group):
    t_all, d = x1.shape
    tm = TOK_TILE
    nl = n_lat_tokens // tm
    ne = slot_n.shape[1]
    rows = y.shape[2]
    assert LANES % DISPATCH_WIN == 0 and ne % (LANES // DISPATCH_WIN) == 0
    grid_spec = pltpu.PrefetchScalarGridSpec(
        num_scalar_prefetch=1,
        grid=(t_all // tm,),
        in_specs=[pl.BlockSpec((tm, d), lambda i, tab: (i, 0)),
                  pl.BlockSpec((None, N_MOD, d), lambda i, tab: (i // tiles_per_group, 0, 0)),
                  pl.BlockSpec((tm, ne), lambda i, tab: (i, 0)),
                  pl.BlockSpec((1, d), lambda i, tab: (0, 0)),
                  pl.BlockSpec(memory_space=pl.ANY)],
        out_specs=[pl.BlockSpec((tm, d), lambda i, tab: (jnp.minimum(i, nl - 1), 0)),
                   pl.BlockSpec((tm, d), lambda i, tab: (jnp.maximum(i - nl, 0), 0))],
        scratch_shapes=[pltpu.VMEM((2, ne * DISPATCH_WIN, d), BF16),
                        pltpu.VMEM((DISPATCH_WIN, d), BF16),
                        pltpu.VMEM((tm, d), F32),
                        pltpu.SemaphoreType.DMA((2, ne)),
                        pltpu.SemaphoreType.DMA((1,))])
    return pl.pallas_call(
        functools.partial(_combine_kernel, tiles_per_group=tiles_per_group, n_lat_tiles=nl, rows=rows),
        name="combine",
        grid_spec=grid_spec,
        out_shape=[jax.ShapeDtypeStruct((n_lat_tokens, d), F32),
                   jax.ShapeDtypeStruct((n_ctx_tokens, d), F32)],
        compiler_params=_params(1),
    )(tab, x1, mod, slot_n, fng, y)


def kernel(x_prompt, x_sample, c, cache_diff_k, cache_diff_v, cache_gqa_k, cache_gqa_v, c_ctx, norm1_g, norm2_g, w_mod, b_mod, w_in, w_out, diff_lambda_q1, diff_lambda_k1, diff_lambda_q2, diff_lambda_k2, diff_subln_g, gqa_q_norm_g, gqa_k_norm_g, w_router, w_expert_gate, w_expert_up, w_expert_down, final_norm_g):
    nbc, seq, d = x_prompt.shape
    nbl, n_lat, _ = x_sample.shape
    depth, past = cache_diff_k.shape[1], cache_diff_k.shape[2]
    tc, tl = nbc * seq, nbl * n_lat
    assert depth == 1, "single trunk layer"
    assert tc == n_lat, "the context tokens must fill exactly one routing group"
    assert seq == TOK_TILE and nbl + 1 <= 8
    ne = N_EXPERTS
    tpg = n_lat // TOK_TILE
    ng = nbl + 1
    rows = CAPACITY_FACTOR * n_lat // ne

    xs = x_sample.reshape(tl, d)
    xp = x_prompt.reshape(tc, d)

    cond = jnp.concatenate([c, c_ctx[None, :], jnp.zeros((8 - ng, d), F32)], axis=0)
    mod = _mod_call(cond, w_mod[0], b_mod).reshape(8, N_MOD, d)

    tables = (_rope_tables(n_lat, DIFF_QK_DIM, TOK_TILE) + _rope_tables(n_lat, HEAD_DIM, TOK_TILE))
    qkv, ndk, ndv, ngk, ngv = _inproj_call(
        xs, xp, mod, norm1_g, w_in[0].astype(BF16), gqa_q_norm_g, gqa_k_norm_g, tables, n_lat)

    lamv = jnp.concatenate([diff_lambda_q1, diff_lambda_k1, diff_lambda_q2, diff_lambda_k2], axis=0)
    ckd = jnp.transpose(cache_diff_k, (0, 1, 3, 4, 5, 2)).reshape(nbl * N_DIFF_HEADS * LANES, past)
    cvd = cache_diff_v.reshape(nbl * past, N_DIFF_HEADS * LANES)
    ckg = cache_gqa_k.reshape(nbl * past, N_GQA_KV_HEADS * LANES)
    cvg = cache_gqa_v.reshape(nbl * past, N_GQA_KV_HEADS * LANES)
    odl = _attn_diff_call(qkv, ckd, cvd, lamv, diff_subln_g, row0=0, nb=nbl, n=n_lat, tq=1024, hps=1)
    ogl = _attn_gqa_call(qkv, ckg, cvg, row0=0, nb=nbl, n=n_lat, tq=256, kvps=N_GQA_KV_HEADS)
    odc = _attn_diff_call(qkv, None, None, lamv, diff_subln_g, row0=tl, nb=nbc, n=seq, tq=seq,
                          hps=N_DIFF_HEADS)
    ogc = _attn_gqa_call(qkv, None, None, row0=tl, nb=nbc, n=seq, tq=seq, kvps=N_GQA_KV_HEADS)

    x1, h2, aff = _outproj_call(odl, ogl, odc, ogc, xs, xp, mod, norm2_g, w_out[0].astype(BF16),
                                _split3_bf16(w_router[0]), ne, n_lat)
    aff_t = aff[:, :ne].T

    slot_l, starts_l = _route_call(aff_t, col0=0, n_sets=nbl, n=n_lat, offset_stride=0, name="route_lat")
    slot_c, starts_c = _route_call(aff_t, col0=tl, n_sets=nbc, n=seq,
                                   offset_stride=CAPACITY_FACTOR * seq // ne, name="route_ctx")
    slot_t = jnp.concatenate([slot_l, slot_c], axis=1)
    tab_l = starts_l[:, :, :tpg + 1]
    tab_c = jnp.concatenate([starts_c[:, :, 0].T, starts_c[-1:, :, 1].T], axis=1)[None]
    tab = jnp.concatenate([tab_l, tab_c], axis=0).reshape(-1)

    x_slots, gate_slots = _dispatch_call(tab, h2, slot_t, aff_t, ng=ng, rows=rows, tiles_per_group=tpg)
    y = _moe_call(x_slots, gate_slots, w_expert_gate[0], w_expert_up[0], w_expert_down[0])

    ys, yp = _combine_call(tab, x1, mod, slot_t.T, final_norm_g[None, :], y,
                           n_lat_tokens=tl, n_ctx_tokens=tc, tiles_per_group=tpg)

    return (yp.reshape(nbc, seq, d), ys.reshape(nbl, n_lat, d),
            ndk.reshape(nbc, 1, N_DIFF_HEADS, 2, DIFF_QK_DIM, seq).transpose(0, 1, 5, 2, 3, 4),
            ndv.reshape(nbc, 1, seq, N_DIFF_HEADS, HEAD_DIM),
            ngk.reshape(nbc, 1, seq, N_GQA_KV_HEADS, HEAD_DIM),
            ngv.reshape(nbc, 1, seq, N_GQA_KV_HEADS, HEAD_DIM))
```

```python
import functools
import math

import jax
import jax.numpy as jnp
import numpy as np
from jax import lax
from jax.experimental import pallas as pl
from jax.experimental.pallas import tpu as pltpu

F32 = jnp.float32
BF16 = jnp.bfloat16
I32 = jnp.int32

LANES = 128
HEAD_DIM = 128
N_DIFF_HEADS = 8
DIFF_QK_DIM = HEAD_DIM // 2
N_GQA_HEADS = 8
N_GQA_KV_HEADS = 2
GQA_GROUP = N_GQA_HEADS // N_GQA_KV_HEADS
N_EXPERTS = 16
CAPACITY_FACTOR = 2
GRID_W = 64
ROPE_THETA = 10000.0
NORM_EPS = 1e-6
N_MOD = 6
LAMBDA_INIT_L0 = 0.8 - 0.6 * math.exp(-0.3 * 0)

DQ_BLK = 0
DK_BLK = DQ_BLK + N_DIFF_HEADS
DV_BLK = DK_BLK + N_DIFF_HEADS
GQ_BLK = DV_BLK + N_DIFF_HEADS
GK_BLK = GQ_BLK + N_GQA_HEADS
GV_BLK = GK_BLK + N_GQA_KV_HEADS
IN_BLKS = GV_BLK + N_GQA_KV_HEADS

TOK_TILE = 256
OUTPROJ_TILE = 512
OUTPROJ_SPLIT = 2
KV_CHUNK_MAX = 2048
DISPATCH_WIN = 64
DISPATCH_TILES_PER_STEP = 2
NEG_BIG = -1e30
LOG2_E = math.log2(math.e)
MIN_EXP = -150.0
EXP_STEPS = 8
MANTISSA_STEPS = 53

VMEM_LIMIT_MB = 56


def _params(n_axes, vmem_mb=VMEM_LIMIT_MB):
    return pltpu.CompilerParams(dimension_semantics=("arbitrary",) * n_axes,
                                vmem_limit_bytes=vmem_mb << 20)


def _sigmoid(x):
    return 1.0 / (1.0 + jnp.exp(-x))


def _rms(x, eps=NORM_EPS):
    return x * lax.rsqrt(jnp.mean(x * x, axis=-1, keepdims=True) + eps)


def _mod_kernel(c_ref, w_ref, b_ref, o_ref):
    c = c_ref[...]
    a = (c * _sigmoid(c)).astype(BF16)
    o_ref[...] = jnp.dot(a, w_ref[...].astype(BF16), preferred_element_type=F32) + b_ref[...]


def _mod_call(cond, w_mod, b_mod):
    d, m = w_mod.shape
    tn = next(t for t in (1536, 1024, 512, 256, LANES) if m % t == 0)
    return pl.pallas_call(
        _mod_kernel, name="mod",
        grid=(m // tn,),
        in_specs=[pl.BlockSpec((8, d), lambda j: (0, 0)),
                  pl.BlockSpec((d, tn), lambda j: (0, j)),
                  pl.BlockSpec((1, tn), lambda j: (0, j))],
        out_specs=pl.BlockSpec((8, tn), lambda j: (0, j)),
        out_shape=jax.ShapeDtypeStruct((8, m), F32),
        compiler_params=_params(1),
    )(cond, w_mod, b_mod)


def _rope(x, cos, sin_signed, half):
    lane = lax.broadcasted_iota(I32, x.shape, 1)
    first = (lane % (2 * half)) < half
    rot = jnp.where(first, pltpu.roll(x, LANES - half, 1), pltpu.roll(x, half, 1))
    return x * cos + rot * sin_signed


def _inproj_kernel(xs_ref, xp_ref, mod_ref, n1g_ref, w_ref, qn_ref, kn_ref,
                   cd_ref, sd_ref, cg_ref, sg_ref,
                   qkv_ref, ndk_ref, ndv_ref, ngk_ref, ngv_ref, *, n_lat_tiles):
    i = pl.program_id(0)
    is_ctx = i >= n_lat_tiles
    x = jnp.where(is_ctx, xp_ref[...], xs_ref[...])
    sh1 = mod_ref[0:1, :]
    sc1 = mod_ref[1:2, :]
    h = (_rms(x) * n1g_ref[...]) * (1.0 + sc1) + sh1
    hb = h.astype(BF16)
    cd, sd, cg, sg = cd_ref[...], sd_ref[...], cg_ref[...], sg_ref[...]
    qn, kn = qn_ref[...], kn_ref[...]
    cache_stores = []

    def seg(blk0, nblk):
        return jnp.dot(hb, w_ref[:, blk0 * LANES:(blk0 + nblk) * LANES], preferred_element_type=F32)

    def put(blk, val):
        qkv_ref[:, blk * LANES:(blk + 1) * LANES] = val.astype(BF16)

    p = seg(DQ_BLK, N_DIFF_HEADS)
    for j in range(N_DIFF_HEADS):
        v = p[:, j * LANES:(j + 1) * LANES]
        put(DQ_BLK + j, _rope(v, cd, sd, DIFF_QK_DIM // 4) * (DIFF_QK_DIM ** -0.5 * LOG2_E))
    p = seg(DK_BLK, N_DIFF_HEADS)
    for j in range(N_DIFF_HEADS):
        v = p[:, j * LANES:(j + 1) * LANES]
        cache_stores.append((ndk_ref, j, v))
        put(DK_BLK + j, _rope(v, cd, sd, DIFF_QK_DIM // 4))
    p = seg(DV_BLK, N_DIFF_HEADS)
    for j in range(N_DIFF_HEADS):
        v = p[:, j * LANES:(j + 1) * LANES]
        cache_stores.append((ndv_ref, j, v))
        put(DV_BLK + j, v)
    p = seg(GQ_BLK, N_GQA_HEADS)
    for j in range(N_GQA_HEADS):
        v = _rms(p[:, j * LANES:(j + 1) * LANES]) * qn
        put(GQ_BLK + j, _rope(v, cg, sg, HEAD_DIM // 4) * (HEAD_DIM ** -0.5 * LOG2_E))
    p = seg(GK_BLK, 2 * N_GQA_KV_HEADS)
    for j in range(N_GQA_KV_HEADS):
        v = _rms(p[:, j * LANES:(j + 1) * LANES]) * kn
        cache_stores.append((ngk_ref, j, v))
        put(GK_BLK + j, _rope(v, cg, sg, HEAD_DIM // 4))
    for j in range(N_GQA_KV_HEADS):
        v = p[:, (N_GQA_KV_HEADS + j) * LANES:(N_GQA_KV_HEADS + j + 1) * LANES]
        cache_stores.append((ngv_ref, j, v))
        put(GV_BLK + j, v)

    @pl.when(is_ctx)
    def _():
        for ref, j, v in cache_stores:
            if ref is ndk_ref:
                ref[j * LANES:(j + 1) * LANES, :] = v.T
            else:
                ref[:, j * LANES:(j + 1) * LANES] = v


def _inproj_call(xs, xp, mod, n1g, w_in_b, qn, kn, tables, n_lat):
    tl, d = xs.shape
    tc = xp.shape[0]
    tm = TOK_TILE
    nl, nc = tl // tm, tc // tm
    tpb = n_lat // tm
    nbl = tl // n_lat
    width = IN_BLKS * LANES

    def tab_spec():
        return pl.BlockSpec((tm, LANES), lambda i: (jnp.where(i < nl, i % tpb, tpb), 0))

    def cache_spec(w):
        return pl.BlockSpec((tm, w), lambda i: (jnp.maximum(i - nl, 0), 0))

    dw, gw = N_DIFF_HEADS * LANES, N_GQA_KV_HEADS * LANES
    return pl.pallas_call(
        functools.partial(_inproj_kernel, n_lat_tiles=nl), name="inproj",
        grid=(nl + nc,),
        in_specs=[pl.BlockSpec((tm, d), lambda i: (jnp.minimum(i, nl - 1), 0)),
                  pl.BlockSpec((tm, d), lambda i: (jnp.maximum(i - nl, 0), 0)),
                  pl.BlockSpec((None, N_MOD, d), lambda i: (jnp.where(i < nl, i // tpb, nbl), 0, 0)),
                  pl.BlockSpec((1, d), lambda i: (0, 0)),
                  pl.BlockSpec((d, width), lambda i: (0, 0), pipeline_mode=pl.Buffered(1)),
                  pl.BlockSpec((1, LANES), lambda i: (0, 0)),
                  pl.BlockSpec((1, LANES), lambda i: (0, 0)),
                  tab_spec(), tab_spec(), tab_spec(), tab_spec()],
        out_specs=[pl.BlockSpec((tm, width), lambda i: (i, 0)),
                   pl.BlockSpec((None, dw, tm), lambda i: (jnp.maximum(i - nl, 0), 0, 0)),
                   cache_spec(dw), cache_spec(gw), cache_spec(gw)],
        out_shape=[jax.ShapeDtypeStruct((tl + tc, width), BF16),
                   jax.ShapeDtypeStruct((nc, dw, tm), F32), jax.ShapeDtypeStruct((tc, dw), F32),
                   jax.ShapeDtypeStruct((tc, gw), F32), jax.ShapeDtypeStruct((tc, gw), F32)],
        compiler_params=_params(1),
    )(xs, xp, mod, n1g, w_in_b, qn, kn, *tables)


def _rope_tables(n, dim, pad_rows):
    t = np.arange(n)
    row = (t // GRID_W).astype(np.float64)
    col = (t % GRID_W).astype(np.float64)
    sec = dim // 2
    inv = ROPE_THETA ** (-np.arange(0, sec, 2, dtype=np.float64) / sec)
    ang = np.stack([row[:, None] * inv, col[:, None] * inv], axis=1)
    ang = np.concatenate([ang, ang], axis=-1).reshape(n, dim)
    sign = np.where((np.arange(dim) % sec) < sec // 2, -1.0, 1.0)
    cos, sin = np.cos(ang), np.sin(ang) * sign
    reps = LANES // dim
    cos, sin = np.tile(cos, (1, reps)), np.tile(sin, (1, reps))
    cos = np.concatenate([cos, np.ones((pad_rows, LANES))], axis=0)
    sin = np.concatenate([sin, np.zeros((pad_rows, LANES))], axis=0)
    return jnp.asarray(cos, F32), jnp.asarray(sin, F32)


def _flash(q, chunks):
    m_rows = q.shape[0]
    m = jnp.full((m_rows, 1), NEG_BIG, F32)
    acc = jnp.zeros((m_rows, HEAD_DIM), F32)
    l = jnp.zeros((m_rows, 1), F32)
    for scores, values in chunks:
        s = scores(q)
        m_new = jnp.maximum(m, jnp.max(s, axis=-1, keepdims=True))
        alpha = jnp.exp2(m - m_new)
        p = jnp.exp2(s - m_new)
        l = alpha * l + jnp.sum(p, axis=-1, keepdims=True)
        acc = alpha * acc + jnp.dot(p.astype(BF16), values(), preferred_element_type=F32)
        m = m_new
    return acc / l


def _scores_nt(k):
    return lambda q: lax.dot_general(q, k(), (((1,), (1,)), ((), ())), preferred_element_type=F32)


def _kv_chunks(k_ref, v_ref, ck_ref, cv_ref, col, cache_k_transposed):
    cs = slice(col * LANES, (col + 1) * LANES)
    chunks = []
    if ck_ref is not None:
        if cache_k_transposed:
            scores = lambda q: jnp.dot(q, ck_ref[cs, :].astype(BF16), preferred_element_type=F32)
        else:
            scores = _scores_nt(lambda: ck_ref[:, cs].astype(BF16))
        chunks.append((scores, lambda: cv_ref[:, cs].astype(BF16)))
    n = k_ref.shape[0]
    n_chunks = pl.cdiv(n, KV_CHUNK_MAX)
    assert n % n_chunks == 0
    step = n // n_chunks
    for c in range(n_chunks):
        rs = slice(c * step, (c + 1) * step)
        chunks.append((_scores_nt(lambda rs=rs: k_ref[rs, cs]), lambda rs=rs: v_ref[rs, cs]))
    return chunks


def _attn_diff_kernel(*refs, hps, has_cache):
    if has_cache:
        q_ref, k_ref, v_ref, ck_ref, cv_ref, lam_ref, g_ref, o_ref = refs
    else:
        q_ref, k_ref, v_ref, lam_ref, g_ref, o_ref = refs
        ck_ref = cv_ref = None
    lv = lam_ref[...]
    lam = (jnp.exp(jnp.sum(lv[0:1] * lv[1:2], axis=-1, keepdims=True))
           - jnp.exp(jnp.sum(lv[2:3] * lv[3:4], axis=-1, keepdims=True)) + LAMBDA_INIT_L0)
    for j in range(hps):
        q = q_ref[:, j * LANES:(j + 1) * LANES]
        lane = lax.broadcasted_iota(I32, q.shape, 1)
        zero = jnp.zeros_like(q)
        chunks = _kv_chunks(k_ref, v_ref, ck_ref, cv_ref, j, cache_k_transposed=True)
        o1 = _flash(jnp.where(lane < DIFF_QK_DIM, q, zero), chunks)
        o2 = _flash(jnp.where(lane >= DIFF_QK_DIM, q, zero), chunks)
        o = o1 - lam * o2
        o = (_rms(o) * g_ref[...]) * (1.0 - LAMBDA_INIT_L0)
        o_ref[:, j * LANES:(j + 1) * LANES] = o.astype(o_ref.dtype)


def _attn_diff_call(qkv, ck, cv, lamv, subg, *, row0, nb, n, tq, hps):
    w = hps * LANES
    hb_n = N_DIFF_HEADS // hps
    qpb = n // tq
    in_specs = [pl.BlockSpec((tq, w), lambda b, h, qi: (row0 // tq + b * qpb + qi, DQ_BLK // hps + h)),
                pl.BlockSpec((n, w), lambda b, h, qi: (row0 // n + b, DK_BLK // hps + h)),
                pl.BlockSpec((n, w), lambda b, h, qi: (row0 // n + b, DV_BLK // hps + h))]
    args = [qkv, qkv, qkv]
    if ck is not None:
        past = ck.shape[1]
        in_specs += [pl.BlockSpec((w, past), lambda b, h, qi: (b * hb_n + h, 0)),
                     pl.BlockSpec((past, w), lambda b, h, qi: (b, h))]
        args += [ck, cv]
    in_specs += [pl.BlockSpec(lamv.shape, lambda b, h, qi: (0, 0)),
                 pl.BlockSpec((1, LANES), lambda b, h, qi: (0, 0))]
    args += [lamv, subg]
    return pl.pallas_call(
        functools.partial(_attn_diff_kernel, hps=hps, has_cache=ck is not None),
        name="attn_diff_lat" if ck is not None else "attn_diff_ctx",
        grid=(nb, hb_n, qpb),
        in_specs=in_specs,
        out_specs=pl.BlockSpec((tq, w), lambda b, h, qi: (b * qpb + qi, h)),
        out_shape=jax.ShapeDtypeStruct((nb * n, N_DIFF_HEADS * LANES), BF16),
        compiler_params=_params(3),
    )(*args)


def _attn_gqa_kernel(*refs, kvps, has_cache):
    if has_cache:
        q_ref, k_ref, v_ref, ck_ref, cv_ref, o_ref = refs
    else:
        q_ref, k_ref, v_ref, o_ref = refs
        ck_ref = cv_ref = None
    tq = q_ref.shape[0]
    for j in range(kvps):
        c0 = j * GQA_GROUP
        q = jnp.concatenate([q_ref[:, (c0 + r) * LANES:(c0 + r + 1) * LANES] for r in range(GQA_GROUP)],
                            axis=0)
        chunks = _kv_chunks(k_ref, v_ref, ck_ref, cv_ref, j, cache_k_transposed=False)
        o = _flash(q, chunks).astype(o_ref.dtype)
        for r in range(GQA_GROUP):
            o_ref[:, (c0 + r) * LANES:(c0 + r + 1) * LANES] = o[r * tq:(r + 1) * tq]


def _attn_gqa_call(qkv, ck, cv, *, row0, nb, n, tq, kvps):
    w = kvps * GQA_GROUP * LANES
    kw = kvps * LANES
    qpb = n // tq
    in_specs = [pl.BlockSpec((tq, w), lambda b, g, qi: (row0 // tq + b * qpb + qi, GQ_BLK * LANES // w + g)),
                pl.BlockSpec((n, kw), lambda b, g, qi: (row0 // n + b, GK_BLK // kvps + g)),
                pl.BlockSpec((n, kw), lambda b, g, qi: (row0 // n + b, GV_BLK // kvps + g))]
    args = [qkv, qkv, qkv]
    if ck is not None:
        past = ck.shape[0] // nb
        in_specs += [pl.BlockSpec((past, kw), lambda b, g, qi: (b, g)),
                     pl.BlockSpec((past, kw), lambda b, g, qi: (b, g))]
        args += [ck, cv]
    return pl.pallas_call(
        functools.partial(_attn_gqa_kernel, kvps=kvps, has_cache=ck is not None),
        name="attn_gqa_lat" if ck is not None else "attn_gqa_ctx",
        grid=(nb, N_GQA_KV_HEADS // kvps, qpb),
        in_specs=in_specs,
        out_specs=pl.BlockSpec((tq, w), lambda b, g, qi: (b * qpb + qi, g)),
        out_shape=jax.ShapeDtypeStruct((nb * n, N_GQA_HEADS * LANES), BF16),
        compiler_params=_params(3),
    )(*args)


def _outproj_kernel(odl_ref, ogl_ref, odc_ref, ogc_ref, xs_ref, xp_ref, mod_ref, n2g_ref, w_ref, wr3_ref,
                    x1_ref, h2_ref, aff_ref, *, n_lat_tiles, ne):
    i = pl.program_id(0)
    is_ctx = i >= n_lat_tiles
    dw = odl_ref.shape[1]
    g1 = mod_ref[2:3, :]
    sh2 = mod_ref[3:4, :]
    sc2 = mod_ref[4:5, :]
    w3 = wr3_ref[...]
    tm = x1_ref.shape[0]
    half = tm // OUTPROJ_SPLIT
    projected = []
    for r in range(OUTPROJ_SPLIT):
        rs = slice(r * half, (r + 1) * half)
        od = jnp.where(is_ctx, odc_ref[rs, :], odl_ref[rs, :])
        og = jnp.where(is_ctx, ogc_ref[rs, :], ogl_ref[rs, :])
        projected.append(jnp.dot(od, w_ref[0:dw, :], preferred_element_type=F32)
                         + jnp.dot(og, w_ref[dw:, :], preferred_element_type=F32))
    for r in range(OUTPROJ_SPLIT):
        rs = slice(r * half, (r + 1) * half)
        x = jnp.where(is_ctx, xp_ref[rs, :], xs_ref[rs, :])
        x1 = x + g1 * projected[r]
        x1_ref[rs, :] = x1
        h2 = (_rms(x1) * n2g_ref[...]) * (1.0 + sc2) + sh2
        h_hi = h2.astype(BF16)
        h2_ref[rs, :] = h_hi
        h_mid = (h2 - h_hi.astype(F32)).astype(BF16)
        t = (jnp.dot(h_mid, w3, preferred_element_type=F32) + jnp.dot(h_hi, w3, preferred_element_type=F32))
        t = (pltpu.roll(t, LANES - 2 * ne, 1) + pltpu.roll(t, LANES - ne, 1)) + t
        lane = lax.broadcasted_iota(I32, t.shape, 1)
        logits = jnp.where(lane < ne, t, NEG_BIG)
        z = jnp.exp(logits - jnp.max(logits, axis=-1, keepdims=True))
        aff_ref[rs, :] = z / jnp.sum(z, axis=-1, keepdims=True)


def _split3_bf16(w):
    hi = w.astype(BF16)
    rem = w - hi.astype(F32)
    mid = rem.astype(BF16)
    lo = (rem - mid.astype(F32)).astype(BF16)
    pad = jnp.zeros((w.shape[0], LANES - 3 * w.shape[1]), BF16)
    return jnp.concatenate([hi, mid, lo, pad], axis=1)


def _outproj_call(odl, ogl, odc, ogc, xs, xp, mod, n2g, w_out_b, wr3, ne, n_lat):
    tl, d = xs.shape
    tc = xp.shape[0]
    tm = OUTPROJ_TILE
    nl, nc = tl // tm, tc // tm
    tpb = n_lat // tm
    nbl = tl // n_lat
    aw = odl.shape[1]

    def lat_spec(w):
        return pl.BlockSpec((tm, w), lambda i: (jnp.minimum(i, nl - 1), 0))

    def ctx_spec(w):
        return pl.BlockSpec((tm, w), lambda i: (jnp.maximum(i - nl, 0), 0))

    return pl.pallas_call(
        functools.partial(_outproj_kernel, n_lat_tiles=nl, ne=ne), name="outproj",
        grid=(nl + nc,),
        in_specs=[lat_spec(aw), lat_spec(aw), ctx_spec(aw), ctx_spec(aw), lat_spec(d), ctx_spec(d),
                  pl.BlockSpec((None, N_MOD, d), lambda i: (jnp.where(i < nl, i // tpb, nbl), 0, 0)),
                  pl.BlockSpec((1, d), lambda i: (0, 0)),
                  pl.BlockSpec(w_out_b.shape, lambda i: (0, 0), pipeline_mode=pl.Buffered(1)),
                  pl.BlockSpec((d, LANES), lambda i: (0, 0))],
        out_specs=[pl.BlockSpec((tm, d), lambda i: (i, 0)),
                   pl.BlockSpec((tm, d), lambda i: (i, 0)),
                   pl.BlockSpec((tm, LANES), lambda i: (i, 0))],
        out_shape=[jax.ShapeDtypeStruct((tl + tc, d), F32),
                   jax.ShapeDtypeStruct((tl + tc, d), BF16),
                   jax.ShapeDtypeStruct((tl + tc, LANES), F32)],
        compiler_params=_params(1),
    )(odl, ogl, odc, ogc, xs, xp, mod, n2g, w_out_b, wr3)


def _route_kernel(aff_ref, slot_ref, starts_ref, *, cap, n, offset_stride):
    ne = aff_ref.shape[0]
    n_sets = aff_ref.shape[1] // n
    tt = TOK_TILE
    aff = jnp.concatenate([aff_ref[:, s * n:(s + 1) * n] for s in range(n_sets)], axis=0)
    capf = float(cap)
    ne_all = n_sets * ne

    def count(mask):
        return jnp.sum(mask.astype(F32), axis=1, keepdims=True)

    def threshold(e):
        return jnp.where(e <= MIN_EXP, 0.0, jnp.exp2(e))

    def exp_step(_, carry):
        e_lo, e_hi = carry
        e_mid = jnp.floor(0.5 * (e_lo + e_hi))
        ok = count(aff >= threshold(e_mid)) >= capf
        return jnp.where(ok, e_mid, e_lo), jnp.where(ok, e_hi, e_mid)

    e_lo, e_hi = lax.fori_loop(0, EXP_STEPS, exp_step,
                               (jnp.full((ne_all, 1), MIN_EXP, F32), jnp.full((ne_all, 1), 1.0, F32)))

    def bisect(_, carry):
        lo, hi = carry
        mid = 0.5 * (lo + hi)
        ok = count(aff >= mid) >= capf
        return jnp.where(ok, mid, lo), jnp.where(ok, hi, mid)

    thr, _ = lax.fori_loop(0, MANTISSA_STEPS, bisect, (threshold(e_lo), threshold(e_hi)))
    gt = aff > thr
    tie = aff == thr
    need = capf - count(gt)

    ri = lax.broadcasted_iota(I32, (tt, tt), 0)
    ci = lax.broadcasted_iota(I32, (tt, tt), 1)
    upper = (ri < ci).astype(BF16)
    set_id = lax.broadcasted_iota(I32, (ne_all, 1), 0) // ne
    tie_pref = jnp.zeros((ne_all, 1), F32)
    sel_pref = (set_id * offset_stride).astype(F32)
    lane = lax.broadcasted_iota(I32, (ne_all, LANES), 1)
    starts = jnp.zeros((ne_all, LANES), F32)
    for t in range(n // tt):
        cs = slice(t * tt, (t + 1) * tt)
        tie_t = tie[:, cs]
        tie_rank = tie_pref + jnp.dot(tie_t.astype(BF16), upper, preferred_element_type=F32)
        sel_t = gt[:, cs] | (tie_t & (tie_rank < need))
        slot_t = sel_pref + jnp.dot(sel_t.astype(BF16), upper, preferred_element_type=F32)
        slot_t = jnp.where(sel_t, slot_t, -1.0).astype(I32)
        for s in range(n_sets):
            slot_ref[:, s * n + t * tt:s * n + (t + 1) * tt] = slot_t[s * ne:(s + 1) * ne]
        starts = jnp.where(lane == t, sel_pref, starts)
        tie_pref = tie_pref + count(tie_t)
        sel_pref = sel_pref + count(sel_t)
    starts = jnp.where(lane == n // tt, sel_pref, starts).astype(I32)
    for s in range(n_sets):
        starts_ref[s] = starts[s * ne:(s + 1) * ne]


def _route_call(aff_t, *, col0, n_sets, n, offset_stride, name):
    ne = aff_t.shape[0]
    cap = CAPACITY_FACTOR * n // N_EXPERTS
    w = n_sets * n
    return pl.pallas_call(
        functools.partial(_route_kernel, cap=cap, n=n, offset_stride=offset_stride), name=name,
        grid=(1,),
        in_specs=[pl.BlockSpec((ne, w), lambda s: (0, col0 // w))],
        out_specs=[pl.BlockSpec((ne, w), lambda s: (0, 0)),
                   pl.BlockSpec((n_sets, ne, LANES), lambda s: (0, 0, 0))],
        out_shape=[jax.ShapeDtypeStruct((ne, w), I32),
                   jax.ShapeDtypeStruct((n_sets, ne, LANES), I32)],
        compiler_params=_params(1),
    )(aff_t)


def _window_start(first_slot, rows):
    return pl.multiple_of(jnp.minimum((first_slot // 16) * 16, rows - DISPATCH_WIN), 16)


def _dispatch_kernel(tab_ref, h2_ref, slot_ref, gate_ref, x_ref, gc_ref, *, tiles_per_group, ne):
    g, eb, tb = pl.program_id(0), pl.program_id(1), pl.program_id(2)
    epb, rows = x_ref.shape[0], x_ref.shape[1]
    tt = TOK_TILE
    win = DISPATCH_WIN

    @pl.when(tb == 0)
    def _():
        x_ref[...] = jnp.zeros_like(x_ref)
        gc_ref[...] = jnp.zeros_like(gc_ref)

    sub = lax.broadcasted_iota(I32, (win, tt), 0)

    for s in range(h2_ref.shape[0] // tt):
        t = tb * (h2_ref.shape[0] // tt) + s
        cs = slice(s * tt, (s + 1) * tt)
        h_t = h2_ref[cs, :]

        def scatter(i, start, hit, cs=cs):
            gwin = jnp.sum(jnp.where(hit, gate_ref[i:i + 1, cs], 0.0), axis=1, keepdims=True)
            gc_ref[i, pl.ds(start, win), :] += jnp.broadcast_to(gwin, (win, LANES))

        starts, hits = [], []
        for i in range(epb):
            base = (g * ne + eb * epb + i) * (tiles_per_group + 1) + t
            start = _window_start(tab_ref[base], rows)
            starts.append(start)
            hits.append(slot_ref[i:i + 1, cs] - start == sub)
        onehot = jnp.concatenate([h.astype(BF16) for h in hits], axis=0)
        moved = jnp.dot(onehot, h_t, preferred_element_type=F32).astype(BF16)
        for i in range(epb):
            x_ref[i, pl.ds(starts[i], win), :] += moved[i * win:(i + 1) * win]
            scatter(i, starts[i], hits[i])

        spans = [tab_ref[(g * ne + eb * epb + i) * (tiles_per_group + 1) + t + 1] - starts[i]
                 for i in range(epb)]

        @pl.when(functools.reduce(jnp.logical_or, [span > win for span in spans]))
        def _(starts=starts, spans=spans, cs=cs, h_t=h_t, scatter=scatter):
            for i in range(epb):
                def extra(w, c, i=i, first=starts[i]):
                    lo = first + w * win
                    start = _window_start(lo, rows)
                    srow = slot_ref[i:i + 1, cs]
                    hit = (srow - start == sub) & (srow >= lo)
                    x_ref[i, pl.ds(start, win), :] += jnp.dot(hit.astype(BF16), h_t,
                                                              preferred_element_type=F32).astype(BF16)
                    scatter(i, start, hit)
                    return c

                lax.fori_loop(1, (spans[i] + win - 1) // win, extra, 0)


def _dispatch_call(tab, h2, slot_t, gate_t, *, ng, rows, tiles_per_group):
    t_all, d = h2.shape
    ne = slot_t.shape[0]
    epb = 8
    tps = DISPATCH_TILES_PER_STEP
    steps = tiles_per_group // tps
    tw = tps * TOK_TILE

    def tok_spec(w):
        return pl.BlockSpec((w, tw), lambda g, eb, t, tab: (eb, g * steps + t))

    grid_spec = pltpu.PrefetchScalarGridSpec(
        num_scalar_prefetch=1,
        grid=(ng, ne // epb, steps),
        in_specs=[pl.BlockSpec((tw, d), lambda g, eb, t, tab: (g * steps + t, 0)),
                  tok_spec(epb), tok_spec(epb)],
        out_specs=[pl.BlockSpec((None, epb, rows, d), lambda g, eb, t, tab: (g, eb, 0, 0)),
                   pl.BlockSpec((None, epb, rows, LANES), lambda g, eb, t, tab: (g, eb, 0, 0))])
    return pl.pallas_call(
        functools.partial(_dispatch_kernel, tiles_per_group=tiles_per_group, ne=ne), name="dispatch",
        grid_spec=grid_spec,
        out_shape=[jax.ShapeDtypeStruct((ng, ne, rows, d), BF16),
                   jax.ShapeDtypeStruct((ng, ne, rows, LANES), F32)],
        compiler_params=_params(3),
    )(tab, h2, slot_t, gate_t)


def _moe_kernel(x_ref, gc_ref, wg_ref, wu_ref, wd_ref, y_ref, yacc_ref, *, nf):
    f = pl.program_id(1)
    ng = x_ref.shape[0]
    tf = wg_ref.shape[1]

    def partial_outputs():
        w_gu = jnp.concatenate([wg_ref[...].astype(BF16), wu_ref[...].astype(BF16)], axis=1)
        wd = wd_ref[...].astype(BF16)
        for g in range(ng):
            gu = jnp.dot(x_ref[g], w_gu, preferred_element_type=F32)
            gate, up = gu[:, :tf], gu[:, tf:]
            hid = (gate * _sigmoid(gate) * up).astype(BF16)
            yield g, jnp.dot(hid, wd, preferred_element_type=F32)

    def finish(g, total):
        y_ref[g] = (total * gc_ref[g][:, 0:1]).astype(y_ref.dtype)

    if nf == 1:
        for g, part in partial_outputs():
            finish(g, part)
        return

    @pl.when(f == 0)
    def _():
        for g, part in partial_outputs():
            yacc_ref[g] = part

    @pl.when(jnp.logical_and(f > 0, f < nf - 1))
    def _():
        for g, part in partial_outputs():
            yacc_ref[g] += part

    @pl.when(f == nf - 1)
    def _():
        for g, part in partial_outputs():
            finish(g, yacc_ref[g] + part)


def _moe_call(x, gc, w_gate, w_up, w_down):
    ng, ne, rows, d = x.shape
    fdim = w_gate.shape[2]
    tf = min(fdim, 256)
    return pl.pallas_call(
        functools.partial(_moe_kernel, nf=fdim // tf), name="moe",
        grid=(ne, fdim // tf),
        in_specs=[pl.BlockSpec((ng, None, rows, d), lambda e, f: (0, e, 0, 0)),
                  pl.BlockSpec((ng, None, rows, LANES), lambda e, f: (0, e, 0, 0)),
                  pl.BlockSpec((None, d, tf), lambda e, f: (e, 0, f)),
                  pl.BlockSpec((None, d, tf), lambda e, f: (e, 0, f)),
                  pl.BlockSpec((None, tf, d), lambda e, f: (e, f, 0))],
        out_specs=pl.BlockSpec((ng, None, rows, d), lambda e, f: (0, e, 0, 0)),
        out_shape=jax.ShapeDtypeStruct((ng, ne, rows, d), BF16),
        scratch_shapes=[pltpu.VMEM((ng, rows, d), F32)],
        compiler_params=_params(2),
    )(x, gc, w_gate, w_up, w_down)


def _combine_kernel(tab_ref, x1_ref, mod_ref, slot_ref, fng_ref, y_hbm, ys_ref, yp_ref,
                    ybuf, obuf, acc_ref, sem, osem, *, tiles_per_group, n_lat_tiles, rows):
    i = pl.program_id(0)
    n_tiles = pl.num_programs(0)
    ne = sem.shape[1]
    win = DISPATCH_WIN
    tt = TOK_TILE
    per = LANES // win

    def first_slot(tile, e):
        return tab_ref[((tile // tiles_per_group) * ne + e) * (tiles_per_group + 1) + tile % tiles_per_group]

    def window_copies(tile, buf):
        g = tile // tiles_per_group
        return [pltpu.make_async_copy(y_hbm.at[g, e, pl.ds(_window_start(first_slot(tile, e), rows), win), :],
                                      ybuf.at[buf, pl.ds(e * win, win), :], sem.at[buf, e])
                for e in range(ne)]

    buf = i % 2

    @pl.when(i == 0)
    def _():
        for cp in window_copies(i, buf):
            cp.start()

    @pl.when(i + 1 < n_tiles)
    def _():
        for cp in window_copies(i + 1, 1 - buf):
            cp.start()

    starts = [_window_start(first_slot(i, e), rows) for e in range(ne)]
    lane = lax.broadcasted_iota(I32, (tt, LANES), 1)
    blocks = []
    for k in range(ne // per):
        slot = slot_ref[:, k * per:k * per + 1]
        start = starts[k * per]
        for j in range(1, per):
            pick = lane >= j * win
            slot = jnp.where(pick, slot_ref[:, k * per + j:k * per + j + 1], slot)
            start = jnp.where(pick, starts[k * per + j], start)
        blocks.append((slot - start == lane % win).astype(BF16))
    onehot = jnp.concatenate(blocks, axis=1)

    for cp in window_copies(i, buf):
        cp.wait()
    acc_ref[...] = jnp.dot(onehot, ybuf[buf], preferred_element_type=F32)

    g = i // tiles_per_group
    spans = [tab_ref[(g * ne + e) * (tiles_per_group + 1) + i % tiles_per_group + 1] - starts[e]
             for e in range(ne)]

    @pl.when(functools.reduce(jnp.logical_or, [span > win for span in spans]))
    def _():
        lane_w = lax.broadcasted_iota(I32, (tt, win), 1)
        for e in range(ne):
            def extra(w, c, e=e, first=starts[e]):
                lo = first + w * win
                start = _window_start(lo, rows)
                cp = pltpu.make_async_copy(y_hbm.at[g, e, pl.ds(start, win), :], obuf, osem.at[0])
                cp.start()
                cp.wait()
                slot = slot_ref[:, e:e + 1]
                hit = ((slot - start == lane_w) & (slot >= lo)).astype(BF16)
                acc_ref[...] += jnp.dot(hit, obuf[...], preferred_element_type=F32)
                return c

            lax.fori_loop(1, (spans[e] + win - 1) // win, extra, 0)

    g2 = mod_ref[5:6, :]
    x2 = x1_ref[...] + g2 * acc_ref[...]
    y = _rms(x2) * fng_ref[...]

    @pl.when(i < n_lat_tiles)
    def _():
        ys_ref[...] = y

    @pl.when(i >= n_lat_tiles)
    def _():
        yp_ref[...] = y


def _combine_call(tab, x1, mod, slot_n, fng, y, *, n_lat_tokens, n_ctx_tokens, tiles_per_group):
    t_all, d = x1.shape
    tm = TOK_TILE
    nl = n_lat_tokens // tm
    ne = slot_n.shape[1]
    rows = y.shape[2]
    assert LANES % DISPATCH_WIN == 0 and ne % (LANES // DISPATCH_WIN) == 0
    grid_spec = pltpu.PrefetchScalarGridSpec(
        num_scalar_prefetch=1,
        grid=(t_all // tm,),
        in_specs=[pl.BlockSpec((tm, d), lambda i, tab: (i, 0)),
                  pl.BlockSpec((None, N_MOD, d), lambda i, tab: (i // tiles_per_group, 0, 0)),
                  pl.BlockSpec((tm, ne), lambda i, tab: (i, 0)),
                  pl.BlockSpec((1, d), lambda i, tab: (0, 0)),
                  pl.BlockSpec(memory_space=pl.ANY)],
        out_specs=[pl.BlockSpec((tm, d), lambda i, tab: (jnp.minimum(i, nl - 1), 0)),
                   pl.BlockSpec((tm, d), lambda i, tab: (jnp.maximum(i - nl, 0), 0))],
        scratch_shapes=[pltpu.VMEM((2, ne * DISPATCH_WIN, d), BF16),
                        pltpu.VMEM((DISPATCH_WIN, d), BF16),
                        pltpu.VMEM((tm, d), F32),
                        pltpu.SemaphoreType.DMA((2, ne)),
                        pltpu.SemaphoreType.DMA((1,))])
    return pl.pallas_call(
        functools.partial(_combine_kernel, tiles_per_group=tiles_per_group, n_lat_tiles=nl, rows=rows),
        name="combine",
        grid_spec=grid_spec,
        out_shape=[jax.ShapeDtypeStruct((n_lat_tokens, d), F32),
                   jax.ShapeDtypeStruct((n_ctx_tokens, d), F32)],
        compiler_params=_params(1),
    )(tab, x1, mod, slot_n, fng, y)


def kernel(x_prompt, x_sample, c, cache_diff_k, cache_diff_v, cache_gqa_k, cache_gqa_v, c_ctx, norm1_g, norm2_g, w_mod, b_mod, w_in, w_out, diff_lambda_q1, diff_lambda_k1, diff_lambda_q2, diff_lambda_k2, diff_subln_g, gqa_q_norm_g, gqa_k_norm_g, w_router, w_expert_gate, w_expert_up, w_expert_down, final_norm_g):
    nbc, seq, d = x_prompt.shape
    nbl, n_lat, _ = x_sample.shape
    depth, past = cache_diff_k.shape[1], cache_diff_k.shape[2]
    tc, tl = nbc * seq, nbl * n_lat
    assert depth == 1, "single trunk layer"
    assert tc == n_lat, "the context tokens must fill exactly one routing group"
    assert seq == TOK_TILE and nbl + 1 <= 8
    ne = N_EXPERTS
    tpg = n_lat // TOK_TILE
    ng = nbl + 1
    rows = CAPACITY_FACTOR * n_lat // ne

    xs = x_sample.reshape(tl, d)
    xp = x_prompt.reshape(tc, d)

    cond = jnp.concatenate([c, c_ctx[None, :], jnp.zeros((8 - ng, d), F32)], axis=0)
    mod = _mod_call(cond, w_mod[0], b_mod).reshape(8, N_MOD, d)

    tables = (_rope_tables(n_lat, DIFF_QK_DIM, TOK_TILE) + _rope_tables(n_lat, HEAD_DIM, TOK_TILE))
    qkv, ndk, ndv, ngk, ngv = _inproj_call(
        xs, xp, mod, norm1_g, w_in[0].astype(BF16), gqa_q_norm_g, gqa_k_norm_g, tables, n_lat)

    lamv = jnp.concatenate([diff_lambda_q1, diff_lambda_k1, diff_lambda_q2, diff_lambda_k2], axis=0)
    ckd = jnp.transpose(cache_diff_k, (0, 1, 3, 4, 5, 2)).reshape(nbl * N_DIFF_HEADS * LANES, past)
    cvd = cache_diff_v.reshape(nbl * past, N_DIFF_HEADS * LANES)
    ckg = cache_gqa_k.reshape(nbl * past, N_GQA_KV_HEADS * LANES)
    cvg = cache_gqa_v.reshape(nbl * past, N_GQA_KV_HEADS * LANES)
    odl = _attn_diff_call(qkv, ckd, cvd, lamv, diff_subln_g, row0=0, nb=nbl, n=n_lat, tq=1024, hps=1)
    ogl = _attn_gqa_call(qkv, ckg, cvg, row0=0, nb=nbl, n=n_lat, tq=256, kvps=N_GQA_KV_HEADS)
    odc = _attn_diff_call(qkv, None, None, lamv, diff_subln_g, row0=tl, nb=nbc, n=seq, tq=seq,
                          hps=N_DIFF_HEADS)
    ogc = _attn_gqa_call(qkv, None, None, row0=tl, nb=nbc, n=seq, tq=seq, kvps=N_GQA_KV_HEADS)

    x1, h2, aff = _outproj_call(odl, ogl, odc, ogc, xs, xp, mod, norm2_g, w_out[0].astype(BF16),
                                _split3_bf16(w_router[0]), ne, n_lat)
    aff_t = aff[:, :ne].T

    slot_l, starts_l = _route_call(aff_t, col0=0, n_sets=nbl, n=n_lat, offset_stride=0, name="route_lat")
    slot_c, starts_c = _route_call(aff_t, col0=tl, n_sets=nbc, n=seq,
                                   offset_stride=CAPACITY_FACTOR * seq // ne, name="route_ctx")
    slot_t = jnp.concatenate([slot_l, slot_c], axis=1)
    tab_l = starts_l[:, :, :tpg + 1]
    tab_c = jnp.concatenate([starts_c[:, :, 0].T, starts_c[-1:, :, 1].T], axis=1)[None]
    tab = jnp.concatenate([tab_l, tab_c], axis=0).reshape(-1)

    x_slots, gate_slots = _dispatch_call(tab, h2, slot_t, aff_t, ng=ng, rows=rows, tiles_per_group=tpg)
    y = _moe_call(x_slots, gate_slots, w_expert_gate[0], w_expert_up[0], w_expert_down[0])

    ys, yp = _combine_call(tab, x1, mod, slot_t.T, final_norm_g[None, :], y,
                           n_lat_tokens=tl, n_ctx_tokens=tc, tiles_per_group=tpg)

    return (yp.reshape(nbc, seq, d), ys.reshape(nbl, n_lat, d),
            ndk.reshape(nbc, 1, N_DIFF_HEADS, 2, DIFF_QK_DIM, seq).transpose(0, 1, 5, 2, 3, 4),
            ndv.reshape(nbc, 1, seq, N_DIFF_HEADS, HEAD_DIM),
            ngk.reshape(nbc, 1, seq, N_GQA_KV_HEADS, HEAD_DIM),
            ngv.reshape(nbc, 1, seq, N_GQA_KV_HEADS, HEAD_DIM))
```

```python
import functools
import math

import jax
import jax.numpy as jnp
import numpy as np
from jax import lax
from jax.experimental import pallas as pl
from jax.experimental.pallas import tpu as pltpu

F32 = jnp.float32
BF16 = jnp.bfloat16
I32 = jnp.int32

LANES = 128
HEAD_DIM = 128
N_DIFF_HEADS = 8
DIFF_QK_DIM = HEAD_DIM // 2
N_GQA_HEADS = 8
N_GQA_KV_HEADS = 2
GQA_GROUP = N_GQA_HEADS // N_GQA_KV_HEADS
N_EXPERTS = 16
CAPACITY_FACTOR = 2
GRID_W = 64
ROPE_THETA = 10000.0
NORM_EPS = 1e-6
N_MOD = 6
LAMBDA_INIT_L0 = 0.8 - 0.6 * math.exp(-0.3 * 0)

DQ_BLK = 0
DK_BLK = DQ_BLK + N_DIFF_HEADS
DV_BLK = DK_BLK + N_DIFF_HEADS
GQ_BLK = DV_BLK + N_DIFF_HEADS
GK_BLK = GQ_BLK + N_GQA_HEADS
GV_BLK = GK_BLK + N_GQA_KV_HEADS
IN_BLKS = GV_BLK + N_GQA_KV_HEADS

TOK_TILE = 256
OUTPROJ_TILE = 512
OUTPROJ_SPLIT = 2
KV_CHUNK_MAX = 2048
DISPATCH_WIN = 64
DISPATCH_TILES_PER_STEP = 4
NEG_BIG = -1e30
LOG2_E = math.log2(math.e)
MIN_EXP = -150.0
EXP_STEPS = 8
MANTISSA_STEPS = 53

VMEM_LIMIT_MB = 56


def _params(n_axes, vmem_mb=VMEM_LIMIT_MB):
    return pltpu.CompilerParams(dimension_semantics=("arbitrary",) * n_axes,
                                vmem_limit_bytes=vmem_mb << 20)


def _sigmoid(x):
    return 1.0 / (1.0 + jnp.exp(-x))


def _rms(x, eps=NORM_EPS):
    return x * lax.rsqrt(jnp.mean(x * x, axis=-1, keepdims=True) + eps)


def _mod_kernel(c_ref, w_ref, b_ref, o_ref):
    c = c_ref[...]
    a = (c * _sigmoid(c)).astype(BF16)
    o_ref[...] = jnp.dot(a, w_ref[...].astype(BF16), preferred_element_type=F32) + b_ref[...]


def _mod_call(cond, w_mod, b_mod):
    d, m = w_mod.shape
    tn = next(t for t in (1536, 1024, 512, 256, LANES) if m % t == 0)
    return pl.pallas_call(
        _mod_kernel, name="mod",
        grid=(m // tn,),
        in_specs=[pl.BlockSpec((8, d), lambda j: (0, 0)),
                  pl.BlockSpec((d, tn), lambda j: (0, j)),
                  pl.BlockSpec((1, tn), lambda j: (0, j))],
        out_specs=pl.BlockSpec((8, tn), lambda j: (0, j)),
        out_shape=jax.ShapeDtypeStruct((8, m), F32),
        compiler_params=_params(1),
    )(cond, w_mod, b_mod)


def _rope(x, cos, sin_signed, half):
    lane = lax.broadcasted_iota(I32, x.shape, 1)
    first = (lane % (2 * half)) < half
    rot = jnp.where(first, pltpu.roll(x, LANES - half, 1), pltpu.roll(x, half, 1))
    return x * cos + rot * sin_signed


def _inproj_kernel(xs_ref, xp_ref, mod_ref, n1g_ref, w_ref, qn_ref, kn_ref,
                   cd_ref, sd_ref, cg_ref, sg_ref,
                   qkv_ref, ndk_ref, ndv_ref, ngk_ref, ngv_ref, *, n_lat_tiles):
    i = pl.program_id(0)
    is_ctx = i >= n_lat_tiles
    x = jnp.where(is_ctx, xp_ref[...], xs_ref[...])
    sh1 = mod_ref[0:1, :]
    sc1 = mod_ref[1:2, :]
    h = (_rms(x) * n1g_ref[...]) * (1.0 + sc1) + sh1
    hb = h.astype(BF16)
    cd, sd, cg, sg = cd_ref[...], sd_ref[...], cg_ref[...], sg_ref[...]
    qn, kn = qn_ref[...], kn_ref[...]
    cache_stores = []

    def seg(blk0, nblk):
        return jnp.dot(hb, w_ref[:, blk0 * LANES:(blk0 + nblk) * LANES], preferred_element_type=F32)

    def put(blk, val):
        qkv_ref[:, blk * LANES:(blk + 1) * LANES] = val.astype(BF16)

    p = seg(DQ_BLK, N_DIFF_HEADS)
    for j in range(N_DIFF_HEADS):
        v = p[:, j * LANES:(j + 1) * LANES]
        put(DQ_BLK + j, _rope(v, cd, sd, DIFF_QK_DIM // 4) * (DIFF_QK_DIM ** -0.5 * LOG2_E))
    p = seg(DK_BLK, N_DIFF_HEADS)
    for j in range(N_DIFF_HEADS):
        v = p[:, j * LANES:(j + 1) * LANES]
        cache_stores.append((ndk_ref, j, v))
        put(DK_BLK + j, _rope(v, cd, sd, DIFF_QK_DIM // 4))
    p = seg(DV_BLK, N_DIFF_HEADS)
    for j in range(N_DIFF_HEADS):
        v = p[:, j * LANES:(j + 1) * LANES]
        cache_stores.append((ndv_ref, j, v))
        put(DV_BLK + j, v)
    p = seg(GQ_BLK, N_GQA_HEADS)
    for j in range(N_GQA_HEADS):
        v = _rms(p[:, j * LANES:(j + 1) * LANES]) * qn
        put(GQ_BLK + j, _rope(v, cg, sg, HEAD_DIM // 4) * (HEAD_DIM ** -0.5 * LOG2_E))
    p = seg(GK_BLK, 2 * N_GQA_KV_HEADS)
    for j in range(N_GQA_KV_HEADS):
        v = _rms(p[:, j * LANES:(j + 1) * LANES]) * kn
        cache_stores.append((ngk_ref, j, v))
        put(GK_BLK + j, _rope(v, cg, sg, HEAD_DIM // 4))
    for j in range(N_GQA_KV_HEADS):
        v = p[:, (N_GQA_KV_HEADS + j) * LANES:(N_GQA_KV_HEADS + j + 1) * LANES]
        cache_stores.append((ngv_ref, j, v))
        put(GV_BLK + j, v)

    @pl.when(is_ctx)
    def _():
        for ref, j, v in cache_stores:
            if ref is ndk_ref:
                ref[j * LANES:(j + 1) * LANES, :] = v.T
            else:
                ref[:, j * LANES:(j + 1) * LANES] = v


def _inproj_call(xs, xp, mod, n1g, w_in_b, qn, kn, tables, n_lat):
    tl, d = xs.shape
    tc = xp.shape[0]
    tm = TOK_TILE
    nl, nc = tl // tm, tc // tm
    tpb = n_lat // tm
    nbl = tl // n_lat
    width = IN_BLKS * LANES

    def tab_spec():
        return pl.BlockSpec((tm, LANES), lambda i: (jnp.where(i < nl, i % tpb, tpb), 0))

    def cache_spec(w):
        return pl.BlockSpec((tm, w), lambda i: (jnp.maximum(i - nl, 0), 0))

    dw, gw = N_DIFF_HEADS * LANES, N_GQA_KV_HEADS * LANES
    return pl.pallas_call(
        functools.partial(_inproj_kernel, n_lat_tiles=nl), name="inproj",
        grid=(nl + nc,),
        in_specs=[pl.BlockSpec((tm, d), lambda i: (jnp.minimum(i, nl - 1), 0)),
                  pl.BlockSpec((tm, d), lambda i: (jnp.maximum(i - nl, 0), 0)),
                  pl.BlockSpec((None, N_MOD, d), lambda i: (jnp.where(i < nl, i // tpb, nbl), 0, 0)),
                  pl.BlockSpec((1, d), lambda i: (0, 0)),
                  pl.BlockSpec((d, width), lambda i: (0, 0), pipeline_mode=pl.Buffered(1)),
                  pl.BlockSpec((1, LANES), lambda i: (0, 0)),
                  pl.BlockSpec((1, LANES), lambda i: (0, 0)),
                  tab_spec(), tab_spec(), tab_spec(), tab_spec()],
        out_specs=[pl.BlockSpec((tm, width), lambda i: (i, 0)),
                   pl.BlockSpec((None, dw, tm), lambda i: (jnp.maximum(i - nl, 0), 0, 0)),
                   cache_spec(dw), cache_spec(gw), cache_spec(gw)],
        out_shape=[jax.ShapeDtypeStruct((tl + tc, width), BF16),
                   jax.ShapeDtypeStruct((nc, dw, tm), F32), jax.ShapeDtypeStruct((tc, dw), F32),
                   jax.ShapeDtypeStruct((tc, gw), F32), jax.ShapeDtypeStruct((tc, gw), F32)],
        compiler_params=_params(1),
    )(xs, xp, mod, n1g, w_in_b, qn, kn, *tables)


def _rope_tables(n, dim, pad_rows):
    t = np.arange(n)
    row = (t // GRID_W).astype(np.float64)
    col = (t % GRID_W).astype(np.float64)
    sec = dim // 2
    inv = ROPE_THETA ** (-np.arange(0, sec, 2, dtype=np.float64) / sec)
    ang = np.stack([row[:, None] * inv, col[:, None] * inv], axis=1)
    ang = np.concatenate([ang, ang], axis=-1).reshape(n, dim)
    sign = np.where((np.arange(dim) % sec) < sec // 2, -1.0, 1.0)
    cos, sin = np.cos(ang), np.sin(ang) * sign
    reps = LANES // dim
    cos, sin = np.tile(cos, (1, reps)), np.tile(sin, (1, reps))
    cos = np.concatenate([cos, np.ones((pad_rows, LANES))], axis=0)
    sin = np.concatenate([sin, np.zeros((pad_rows, LANES))], axis=0)
    return jnp.asarray(cos, F32), jnp.asarray(sin, F32)


def _flash(q, chunks):
    m_rows = q.shape[0]
    m = jnp.full((m_rows, 1), NEG_BIG, F32)
    acc = jnp.zeros((m_rows, HEAD_DIM), F32)
    l = jnp.zeros((m_rows, 1), F32)
    for scores, values in chunks:
        s = scores(q)
        m_new = jnp.maximum(m, jnp.max(s, axis=-1, keepdims=True))
        alpha = jnp.exp2(m - m_new)
        p = jnp.exp2(s - m_new)
        l = alpha * l + jnp.sum(p, axis=-1, keepdims=True)
        acc = alpha * acc + jnp.dot(p.astype(BF16), values(), preferred_element_type=F32)
        m = m_new
    return acc / l


def _scores_nt(k):
    return lambda q: lax.dot_general(q, k(), (((1,), (1,)), ((), ())), preferred_element_type=F32)


def _kv_chunks(k_ref, v_ref, ck_ref, cv_ref, col, cache_k_transposed):
    cs = slice(col * LANES, (col + 1) * LANES)
    chunks = []
    if ck_ref is not None:
        if cache_k_transposed:
            scores = lambda q: jnp.dot(q, ck_ref[cs, :].astype(BF16), preferred_element_type=F32)
        else:
            scores = _scores_nt(lambda: ck_ref[:, cs].astype(BF16))
        chunks.append((scores, lambda: cv_ref[:, cs].astype(BF16)))
    n = k_ref.shape[0]
    n_chunks = pl.cdiv(n, KV_CHUNK_MAX)
    assert n % n_chunks == 0
    step = n // n_chunks
    for c in range(n_chunks):
        rs = slice(c * step, (c + 1) * step)
        chunks.append((_scores_nt(lambda rs=rs: k_ref[rs, cs]), lambda rs=rs: v_ref[rs, cs]))
    return chunks


def _attn_diff_kernel(*refs, hps, has_cache):
    if has_cache:
        q_ref, k_ref, v_ref, ck_ref, cv_ref, lam_ref, g_ref, o_ref = refs
    else:
        q_ref, k_ref, v_ref, lam_ref, g_ref, o_ref = refs
        ck_ref = cv_ref = None
    lv = lam_ref[...]
    lam = (jnp.exp(jnp.sum(lv[0:1] * lv[1:2], axis=-1, keepdims=True))
           - jnp.exp(jnp.sum(lv[2:3] * lv[3:4], axis=-1, keepdims=True)) + LAMBDA_INIT_L0)
    for j in range(hps):
        q = q_ref[:, j * LANES:(j + 1) * LANES]
        lane = lax.broadcasted_iota(I32, q.shape, 1)
        zero = jnp.zeros_like(q)
        chunks = _kv_chunks(k_ref, v_ref, ck_ref, cv_ref, j, cache_k_transposed=True)
        o1 = _flash(jnp.where(lane < DIFF_QK_DIM, q, zero), chunks)
        o2 = _flash(jnp.where(lane >= DIFF_QK_DIM, q, zero), chunks)
        o = o1 - lam * o2
        o = (_rms(o) * g_ref[...]) * (1.0 - LAMBDA_INIT_L0)
        o_ref[:, j * LANES:(j + 1) * LANES] = o.astype(o_ref.dtype)


def _attn_diff_call(qkv, ck, cv, lamv, subg, *, row0, nb, n, tq, hps):
    w = hps * LANES
    hb_n = N_DIFF_HEADS // hps
    qpb = n // tq
    in_specs = [pl.BlockSpec((tq, w), lambda b, h, qi: (row0 // tq + b * qpb + qi, DQ_BLK // hps + h)),
                pl.BlockSpec((n, w), lambda b, h, qi: (row0 // n + b, DK_BLK // hps + h)),
                pl.BlockSpec((n, w), lambda b, h, qi: (row0 // n + b, DV_BLK // hps + h))]
    args = [qkv, qkv, qkv]
    if ck is not None:
        past = ck.shape[1]
        in_specs += [pl.BlockSpec((w, past), lambda b, h, qi: (b * hb_n + h, 0)),
                     pl.BlockSpec((past, w), lambda b, h, qi: (b, h))]
        args += [ck, cv]
    in_specs += [pl.BlockSpec(lamv.shape, lambda b, h, qi: (0, 0)),
                 pl.BlockSpec((1, LANES), lambda b, h, qi: (0, 0))]
    args += [lamv, subg]
    return pl.pallas_call(
        functools.partial(_attn_diff_kernel, hps=hps, has_cache=ck is not None),
        name="attn_diff_lat" if ck is not None else "attn_diff_ctx",
        grid=(nb, hb_n, qpb),
        in_specs=in_specs,
        out_specs=pl.BlockSpec((tq, w), lambda b, h, qi: (b * qpb + qi, h)),
        out_shape=jax.ShapeDtypeStruct((nb * n, N_DIFF_HEADS * LANES), BF16),
        compiler_params=_params(3),
    )(*args)


def _attn_gqa_kernel(*refs, kvps, has_cache):
    if has_cache:
        q_ref, k_ref, v_ref, ck_ref, cv_ref, o_ref = refs
    else:
        q_ref, k_ref, v_ref, o_ref = refs
        ck_ref = cv_ref = None
    tq = q_ref.shape[0]
    for j in range(kvps):
        c0 = j * GQA_GROUP
        q = jnp.concatenate([q_ref[:, (c0 + r) * LANES:(c0 + r + 1) * LANES] for r in range(GQA_GROUP)],
                            axis=0)
        chunks = _kv_chunks(k_ref, v_ref, ck_ref, cv_ref, j, cache_k_transposed=False)
        o = _flash(q, chunks).astype(o_ref.dtype)
        for r in range(GQA_GROUP):
            o_ref[:, (c0 + r) * LANES:(c0 + r + 1) * LANES] = o[r * tq:(r + 1) * tq]


def _attn_gqa_call(qkv, ck, cv, *, row0, nb, n, tq, kvps):
    w = kvps * GQA_GROUP * LANES
    kw = kvps * LANES
    qpb = n // tq
    in_specs = [pl.BlockSpec((tq, w), lambda b, g, qi: (row0 // tq + b * qpb + qi, GQ_BLK * LANES // w + g)),
                pl.BlockSpec((n, kw), lambda b, g, qi: (row0 // n + b, GK_BLK // kvps + g)),
                pl.BlockSpec((n, kw), lambda b, g, qi: (row0 // n + b, GV_BLK // kvps + g))]
    args = [qkv, qkv, qkv]
    if ck is not None:
        past = ck.shape[0] // nb
        in_specs += [pl.BlockSpec((past, kw), lambda b, g, qi: (b, g)),
                     pl.BlockSpec((past, kw), lambda b, g, qi: (b, g))]
        args += [ck, cv]
    return pl.pallas_call(
        functools.partial(_attn_gqa_kernel, kvps=kvps, has_cache=ck is not None),
        name="attn_gqa_lat" if ck is not None else "attn_gqa_ctx",
        grid=(nb, N_GQA_KV_HEADS // kvps, qpb),
        in_specs=in_specs,
        out_specs=pl.BlockSpec((tq, w), lambda b, g, qi: (b * qpb + qi, g)),
        out_shape=jax.ShapeDtypeStruct((nb * n, N_GQA_HEADS * LANES), BF16),
        compiler_params=_params(3),
    )(*args)


def _outproj_kernel(odl_ref, ogl_ref, odc_ref, ogc_ref, xs_ref, xp_ref, mod_ref, n2g_ref, w_ref, wr3_ref,
                    x1_ref, h2_ref, aff_ref, *, n_lat_tiles, ne):
    i = pl.program_id(0)
    is_ctx = i >= n_lat_tiles
    dw = odl_ref.shape[1]
    g1 = mod_ref[2:3, :]
    sh2 = mod_ref[3:4, :]
    sc2 = mod_ref[4:5, :]
    w3 = wr3_ref[...]
    tm = x1_ref.shape[0]
    half = tm // OUTPROJ_SPLIT
    projected = []
    for r in range(OUTPROJ_SPLIT):
        rs = slice(r * half, (r + 1) * half)
        od = jnp.where(is_ctx, odc_ref[rs, :], odl_ref[rs, :])
        og = jnp.where(is_ctx, ogc_ref[rs, :], ogl_ref[rs, :])
        projected.append(jnp.dot(od, w_ref[0:dw, :], preferred_element_type=F32)
                         + jnp.dot(og, w_ref[dw:, :], preferred_element_type=F32))
    for r in range(OUTPROJ_SPLIT):
        rs = slice(r * half, (r + 1) * half)
        x = jnp.where(is_ctx, xp_ref[rs, :], xs_ref[rs, :])
        x1 = x + g1 * projected[r]
        x1_ref[rs, :] = x1
        h2 = (_rms(x1) * n2g_ref[...]) * (1.0 + sc2) + sh2
        h_hi = h2.astype(BF16)
        h2_ref[rs, :] = h_hi
        h_mid = (h2 - h_hi.astype(F32)).astype(BF16)
        t = (jnp.dot(h_mid, w3, preferred_element_type=F32) + jnp.dot(h_hi, w3, preferred_element_type=F32))
        t = (pltpu.roll(t, LANES - 2 * ne, 1) + pltpu.roll(t, LANES - ne, 1)) + t
        lane = lax.broadcasted_iota(I32, t.shape, 1)
        logits = jnp.where(lane < ne, t, NEG_BIG)
        z = jnp.exp(logits - jnp.max(logits, axis=-1, keepdims=True))
        aff_ref[rs, :] = z / jnp.sum(z, axis=-1, keepdims=True)


def _split3_bf16(w):
    hi = w.astype(BF16)
    rem = w - hi.astype(F32)
    mid = rem.astype(BF16)
    lo = (rem - mid.astype(F32)).astype(BF16)
    pad = jnp.zeros((w.shape[0], LANES - 3 * w.shape[1]), BF16)
    return jnp.concatenate([hi, mid, lo, pad], axis=1)


def _outproj_call(odl, ogl, odc, ogc, xs, xp, mod, n2g, w_out_b, wr3, ne, n_lat):
    tl, d = xs.shape
    tc = xp.shape[0]
    tm = OUTPROJ_TILE
    nl, nc = tl // tm, tc // tm
    tpb = n_lat // tm
    nbl = tl // n_lat
    aw = odl.shape[1]

    def lat_spec(w):
        return pl.BlockSpec((tm, w), lambda i: (jnp.minimum(i, nl - 1), 0))

    def ctx_spec(w):
        return pl.BlockSpec((tm, w), lambda i: (jnp.maximum(i - nl, 0), 0))

    return pl.pallas_call(
        functools.partial(_outproj_kernel, n_lat_tiles=nl, ne=ne), name="outproj",
        grid=(nl + nc,),
        in_specs=[lat_spec(aw), lat_spec(aw), ctx_spec(aw), ctx_spec(aw), lat_spec(d), ctx_spec(d),
                  pl.BlockSpec((None, N_MOD, d), lambda i: (jnp.where(i < nl, i // tpb, nbl), 0, 0)),
                  pl.BlockSpec((1, d), lambda i: (0, 0)),
                  pl.BlockSpec(w_out_b.shape, lambda i: (0, 0), pipeline_mode=pl.Buffered(1)),
                  pl.BlockSpec((d, LANES), lambda i: (0, 0))],
        out_specs=[pl.BlockSpec((tm, d), lambda i: (i, 0)),
                   pl.BlockSpec((tm, d), lambda i: (i, 0)),
                   pl.BlockSpec((tm, LANES), lambda i: (i, 0))],
        out_shape=[jax.ShapeDtypeStruct((tl + tc, d), F32),
                   jax.ShapeDtypeStruct((tl + tc, d), BF16),
                   jax.ShapeDtypeStruct((tl + tc, LANES), F32)],
        compiler_params=_params(1),
    )(odl, ogl, odc, ogc, xs, xp, mod, n2g, w_out_b, wr3)


def _route_kernel(aff_ref, slot_ref, starts_ref, *, cap, n, offset_stride):
    ne = aff_ref.shape[0]
    n_sets = aff_ref.shape[1] // n
    tt = TOK_TILE
    aff = jnp.concatenate([aff_ref[:, s * n:(s + 1) * n] for s in range(n_sets)], axis=0)
    capf = float(cap)
    ne_all = n_sets * ne

    def count(mask):
        return jnp.sum(mask.astype(F32), axis=1, keepdims=True)

    def threshold(e):
        return jnp.where(e <= MIN_EXP, 0.0, jnp.exp2(e))

    def exp_step(_, carry):
        e_lo, e_hi = carry
        e_mid = jnp.floor(0.5 * (e_lo + e_hi))
        ok = count(aff >= threshold(e_mid)) >= capf
        return jnp.where(ok, e_mid, e_lo), jnp.where(ok, e_hi, e_mid)

    e_lo, e_hi = lax.fori_loop(0, EXP_STEPS, exp_step,
                               (jnp.full((ne_all, 1), MIN_EXP, F32), jnp.full((ne_all, 1), 1.0, F32)))

    def bisect(_, carry):
        lo, hi = carry
        mid = 0.5 * (lo + hi)
        ok = count(aff >= mid) >= capf
        return jnp.where(ok, mid, lo), jnp.where(ok, hi, mid)

    thr, _ = lax.fori_loop(0, MANTISSA_STEPS, bisect, (threshold(e_lo), threshold(e_hi)))
    gt = aff > thr
    tie = aff == thr
    need = capf - count(gt)

    ri = lax.broadcasted_iota(I32, (tt, tt), 0)
    ci = lax.broadcasted_iota(I32, (tt, tt), 1)
    upper = (ri < ci).astype(BF16)
    set_id = lax.broadcasted_iota(I32, (ne_all, 1), 0) // ne
    tie_pref = jnp.zeros((ne_all, 1), F32)
    sel_pref = (set_id * offset_stride).astype(F32)
    lane = lax.broadcasted_iota(I32, (ne_all, LANES), 1)
    starts = jnp.zeros((ne_all, LANES), F32)
    for t in range(n // tt):
        cs = slice(t * tt, (t + 1) * tt)
        tie_t = tie[:, cs]
        tie_rank = tie_pref + jnp.dot(tie_t.astype(BF16), upper, preferred_element_type=F32)
        sel_t = gt[:, cs] | (tie_t & (tie_rank < need))
        slot_t = sel_pref + jnp.dot(sel_t.astype(BF16), upper, preferred_element_type=F32)
        slot_t = jnp.where(sel_t, slot_t, -1.0).astype(I32)
        for s in range(n_sets):
            slot_ref[:, s * n + t * tt:s * n + (t + 1) * tt] = slot_t[s * ne:(s + 1) * ne]
        starts = jnp.where(lane == t, sel_pref, starts)
        tie_pref = tie_pref + count(tie_t)
        sel_pref = sel_pref + count(sel_t)
    starts = jnp.where(lane == n // tt, sel_pref, starts).astype(I32)
    for s in range(n_sets):
        starts_ref[s] = starts[s * ne:(s + 1) * ne]


def _route_call(aff_t, *, col0, n_sets, n, offset_stride, name):
    ne = aff_t.shape[0]
    cap = CAPACITY_FACTOR * n // N_EXPERTS
    w = n_sets * n
    return pl.pallas_call(
        functools.partial(_route_kernel, cap=cap, n=n, offset_stride=offset_stride), name=name,
        grid=(1,),
        in_specs=[pl.BlockSpec((ne, w), lambda s: (0, col0 // w))],
        out_specs=[pl.BlockSpec((ne, w), lambda s: (0, 0)),
                   pl.BlockSpec((n_sets, ne, LANES), lambda s: (0, 0, 0))],
        out_shape=[jax.ShapeDtypeStruct((ne, w), I32),
                   jax.ShapeDtypeStruct((n_sets, ne, LANES), I32)],
        compiler_params=_params(1),
    )(aff_t)


def _window_start(first_slot, rows):
    return pl.multiple_of(jnp.minimum((first_slot // 16) * 16, rows - DISPATCH_WIN), 16)


def _dispatch_kernel(tab_ref, h2_ref, slot_ref, gate_ref, x_ref, gc_ref, *, tiles_per_group, ne):
    g, eb, tb = pl.program_id(0), pl.program_id(1), pl.program_id(2)
    epb, rows = x_ref.shape[0], x_ref.shape[1]
    tt = TOK_TILE
    win = DISPATCH_WIN

    @pl.when(tb == 0)
    def _():
        x_ref[...] = jnp.zeros_like(x_ref)
        gc_ref[...] = jnp.zeros_like(gc_ref)

    sub = lax.broadcasted_iota(I32, (win, tt), 0)

    for s in range(h2_ref.shape[0] // tt):
        t = tb * (h2_ref.shape[0] // tt) + s
        cs = slice(s * tt, (s + 1) * tt)
        h_t = h2_ref[cs, :]

        def scatter(i, start, hit, cs=cs):
            gwin = jnp.sum(jnp.where(hit, gate_ref[i:i + 1, cs], 0.0), axis=1, keepdims=True)
            gc_ref[i, pl.ds(start, win), :] += jnp.broadcast_to(gwin, (win, LANES))

        starts, hits = [], []
        for i in range(epb):
            base = (g * ne + eb * epb + i) * (tiles_per_group + 1) + t
            start = _window_start(tab_ref[base], rows)
            starts.append(start)
            hits.append(slot_ref[i:i + 1, cs] - start == sub)
        onehot = jnp.concatenate([h.astype(BF16) for h in hits], axis=0)
        moved = jnp.dot(onehot, h_t, preferred_element_type=F32).astype(BF16)
        for i in range(epb):
            x_ref[i, pl.ds(starts[i], win), :] += moved[i * win:(i + 1) * win]
            scatter(i, starts[i], hits[i])

        spans = [tab_ref[(g * ne + eb * epb + i) * (tiles_per_group + 1) + t + 1] - starts[i]
                 for i in range(epb)]

        @pl.when(functools.reduce(jnp.logical_or, [span > win for span in spans]))
        def _(starts=starts, spans=spans, cs=cs, h_t=h_t, scatter=scatter):
            for i in range(epb):
                def extra(w, c, i=i, first=starts[i]):
                    lo = first + w * win
                    start = _window_start(lo, rows)
                    srow = slot_ref[i:i + 1, cs]
                    hit = (srow - start == sub) & (srow >= lo)
                    x_ref[i, pl.ds(start, win), :] += jnp.dot(hit.astype(BF16), h_t,
                                                              preferred_element_type=F32).astype(BF16)
                    scatter(i, start, hit)
                    return c

                lax.fori_loop(1, (spans[i] + win - 1) // win, extra, 0)


def _dispatch_call(tab, h2, slot_t, gate_t, *, ng, rows, tiles_per_group):
    t_all, d = h2.shape
    ne = slot_t.shape[0]
    epb = 8
    tps = DISPATCH_TILES_PER_STEP
    steps = tiles_per_group // tps
    tw = tps * TOK_TILE

    def tok_spec(w):
        return pl.BlockSpec((w, tw), lambda g, eb, t, tab: (eb, g * steps + t))

    grid_spec = pltpu.PrefetchScalarGridSpec(
        num_scalar_prefetch=1,
        grid=(ng, ne // epb, steps),
        in_specs=[pl.BlockSpec((tw, d), lambda g, eb, t, tab: (g * steps + t, 0)),
                  tok_spec(epb), tok_spec(epb)],
        out_specs=[pl.BlockSpec((None, epb, rows, d), lambda g, eb, t, tab: (g, eb, 0, 0)),
                   pl.BlockSpec((None, epb, rows, LANES), lambda g, eb, t, tab: (g, eb, 0, 0))])
    return pl.pallas_call(
        functools.partial(_dispatch_kernel, tiles_per_group=tiles_per_group, ne=ne), name="dispatch",
        grid_spec=grid_spec,
        out_shape=[jax.ShapeDtypeStruct((ng, ne, rows, d), BF16),
                   jax.ShapeDtypeStruct((ng, ne, rows, LANES), F32)],
        compiler_params=_params(3),
    )(tab, h2, slot_t, gate_t)


def _moe_kernel(x_ref, gc_ref, wg_ref, wu_ref, wd_ref, y_ref, yacc_ref, *, nf):
    f = pl.program_id(1)
    ng = x_ref.shape[0]
    tf = wg_ref.shape[1]

    def partial_outputs():
        w_gu = jnp.concatenate([wg_ref[...].astype(BF16), wu_ref[...].astype(BF16)], axis=1)
        wd = wd_ref[...].astype(BF16)
        for g in range(ng):
            gu = jnp.dot(x_ref[g], w_gu, preferred_element_type=F32)
            gate, up = gu[:, :tf], gu[:, tf:]
            hid = (gate * _sigmoid(gate) * up).astype(BF16)
            yield g, jnp.dot(hid, wd, preferred_element_type=F32)

    def finish(g, total):
        y_ref[g] = (total * gc_ref[g][:, 0:1]).astype(y_ref.dtype)

    if nf == 1:
        for g, part in partial_outputs():
            finish(g, part)
        return

    @pl.when(f == 0)
    def _():
        for g, part in partial_outputs():
            yacc_ref[g] = part

    @pl.when(jnp.logical_and(f > 0, f < nf - 1))
    def _():
        for g, part in partial_outputs():
            yacc_ref[g] += part

    @pl.when(f == nf - 1)
    def _():
        for g, part in partial_outputs():
            finish(g, yacc_ref[g] + part)


def _moe_call(x, gc, w_gate, w_up, w_down):
    ng, ne, rows, d = x.shape
    fdim = w_gate.shape[2]
    tf = min(fdim, 256)
    return pl.pallas_call(
        functools.partial(_moe_kernel, nf=fdim // tf), name="moe",
        grid=(ne, fdim // tf),
        in_specs=[pl.BlockSpec((ng, None, rows, d), lambda e, f: (0, e, 0, 0)),
                  pl.BlockSpec((ng, None, rows, LANES), lambda e, f: (0, e, 0, 0)),
                  pl.BlockSpec((None, d, tf), lambda e, f: (e, 0, f)),
                  pl.BlockSpec((None, d, tf), lambda e, f: (e, 0, f)),
                  pl.BlockSpec((None, tf, d), lambda e, f: (e, f, 0))],
        out_specs=pl.BlockSpec((ng, None, rows, d), lambda e, f: (0, e, 0, 0)),
        out_shape=jax.ShapeDtypeStruct((ng, ne, rows, d), BF16),
        scratch_shapes=[pltpu.VMEM((ng, rows, d), F32)],
        compiler_params=_params(2),
    )(x, gc, w_gate, w_up, w_down)


def _combine_kernel(tab_ref, x1_ref, mod_ref, slot_ref, fng_ref, y_hbm, ys_ref, yp_ref,
                    ybuf, obuf, acc_ref, sem, osem, *, tiles_per_group, n_lat_tiles, rows):
    i = pl.program_id(0)
    n_tiles = pl.num_programs(0)
    ne = sem.shape[1]
    win = DISPATCH_WIN
    tt = TOK_TILE
    per = LANES // win

    def first_slot(tile, e):
        return tab_ref[((tile // tiles_per_group) * ne + e) * (tiles_per_group + 1) + tile % tiles_per_group]

    def window_copies(tile, buf):
        g = tile // tiles_per_group
        return [pltpu.make_async_copy(y_hbm.at[g, e, pl.ds(_window_start(first_slot(tile, e), rows), win), :],
                                      ybuf.at[buf, pl.ds(e * win, win), :], sem.at[buf, e])
                for e in range(ne)]

    buf = i % 2

    @pl.when(i == 0)
    def _():
        for cp in window_copies(i, buf):
            cp.start()

    @pl.when(i + 1 < n_tiles)
    def _():
        for cp in window_copies(i + 1, 1 - buf):
            cp.start()

    starts = [_window_start(first_slot(i, e), rows) for e in range(ne)]
    lane = lax.broadcasted_iota(I32, (tt, LANES), 1)
    blocks = []
    for k in range(ne // per):
        slot = slot_ref[:, k * per:k * per + 1]
        start = starts[k * per]
        for j in range(1, per):
            pick = lane >= j * win
            slot = jnp.where(pick, slot_ref[:, k * per + j:k * per + j + 1], slot)
            start = jnp.where(pick, starts[k * per + j], start)
        blocks.append((slot - start == lane % win).astype(BF16))
    onehot = jnp.concatenate(blocks, axis=1)

    for cp in window_copies(i, buf):
        cp.wait()
    acc_ref[...] = jnp.dot(onehot, ybuf[buf], preferred_element_type=F32)

    g = i // tiles_per_group
    spans = [tab_ref[(g * ne + e) * (tiles_per_group + 1) + i % tiles_per_group + 1] - starts[e]
             for e in range(ne)]

    @pl.when(functools.reduce(jnp.logical_or, [span > win for span in spans]))
    def _():
        lane_w = lax.broadcasted_iota(I32, (tt, win), 1)
        for e in range(ne):
            def extra(w, c, e=e, first=starts[e]):
                lo = first + w * win
                start = _window_start(lo, rows)
                cp = pltpu.make_async_copy(y_hbm.at[g, e, pl.ds(start, win), :], obuf, osem.at[0])
                cp.start()
                cp.wait()
                slot = slot_ref[:, e:e + 1]
                hit = ((slot - start == lane_w) & (slot >= lo)).astype(BF16)
                acc_ref[...] += jnp.dot(hit, obuf[...], preferred_element_type=F32)
                return c

            lax.fori_loop(1, (spans[e] + win - 1) // win, extra, 0)

    g2 = mod_ref[5:6, :]
    x2 = x1_ref[...] + g2 * acc_ref[...]
    y = _rms(x2) * fng_ref[...]

    @pl.when(i < n_lat_tiles)
    def _():
        ys_ref[...] = y

    @pl.when(i >= n_lat_tiles)
    def _():
        yp_ref[...] = y


def _combine_call(tab, x1, mod, slot_n, fng, y, *, n_lat_tokens, n_ctx_tokens, tiles_per_group):
    t_all, d = x1.shape
    tm = TOK_TILE
    nl = n_lat_tokens // tm
    ne = slot_n.shape[1]
    rows = y.shape[2]
    assert LANES % DISPATCH_WIN == 0 and ne % (LANES // DISPATCH_WIN) == 0
    grid_spec = pltpu.PrefetchScalarGridSpec(
        num_scalar_prefetch=1,
        grid=(t_all // tm,),
        in_specs=[pl.BlockSpec((tm, d), lambda i, tab: (i, 0)),
                  pl.BlockSpec((None, N_MOD, d), lambda i, tab: (i // tiles_per_group, 0, 0)),
                  pl.BlockSpec((tm, ne), lambda i, tab: (i, 0)),
                  pl.BlockSpec((1, d), lambda i, tab: (0, 0)),
                  pl.BlockSpec(memory_space=pl.ANY)],
        out_specs=[pl.BlockSpec((tm, d), lambda i, tab: (jnp.minimum(i, nl - 1), 0)),
                   pl.BlockSpec((tm, d), lambda i, tab: (jnp.maximum(i - nl, 0), 0))],
        scratch_shapes=[pltpu.VMEM((2, ne * DISPATCH_WIN, d), BF16),
                        pltpu.VMEM((DISPATCH_WIN, d), BF16),
                        pltpu.VMEM((tm, d), F32),
                        pltpu.SemaphoreType.DMA((2, ne)),
                        pltpu.SemaphoreType.DMA((1,))])
    return pl.pallas_call(
        functools.partial(_combine_kernel, tiles_per_group=tiles_per_group, n_lat_tiles=nl, rows=rows),
        name="combine",
        grid_spec=grid_spec,
        out_shape=[jax.ShapeDtypeStruct((n_lat_tokens, d), F32),
                   jax.ShapeDtypeStruct((n_ctx_tokens, d), F32)],
        compiler_params=_params(1),
    )(tab, x1, mod, slot_n, fng, y)


def kernel(x_prompt, x_sample, c, cache_diff_k, cache_diff_v, cache_gqa_k, cache_gqa_v, c_ctx, norm1_g, norm2_g, w_mod, b_mod, w_in, w_out, diff_lambda_q1, diff_lambda_k1, diff_lambda_q2, diff_lambda_k2, diff_subln_g, gqa_q_norm_g, gqa_k_norm_g, w_router, w_expert_gate, w_expert_up, w_expert_down, final_norm_g):
    nbc, seq, d = x_prompt.shape
    nbl, n_lat, _ = x_sample.shape
    depth, past = cache_diff_k.shape[1], cache_diff_k.shape[2]
    tc, tl = nbc * seq, nbl * n_lat
    assert depth == 1, "single trunk layer"
    assert tc == n_lat, "the context tokens must fill exactly one routing group"
    assert seq == TOK_TILE and nbl + 1 <= 8
    ne = N_EXPERTS
    tpg = n_lat // TOK_TILE
    ng = nbl + 1
    rows = CAPACITY_FACTOR * n_lat // ne

    xs = x_sample.reshape(tl, d)
    xp = x_prompt.reshape(tc, d)

    cond = jnp.concatenate([c, c_ctx[None, :], jnp.zeros((8 - ng, d), F32)], axis=0)
    mod = _mod_call(cond, w_mod[0], b_mod).reshape(8, N_MOD, d)

    tables = (_rope_tables(n_lat, DIFF_QK_DIM, TOK_TILE) + _rope_tables(n_lat, HEAD_DIM, TOK_TILE))
    qkv, ndk, ndv, ngk, ngv = _inproj_call(
        xs, xp, mod, norm1_g, w_in[0].astype(BF16), gqa_q_norm_g, gqa_k_norm_g, tables, n_lat)

    lamv = jnp.concatenate([diff_lambda_q1, diff_lambda_k1, diff_lambda_q2, diff_lambda_k2], axis=0)
    ckd = jnp.transpose(cache_diff_k, (0, 1, 3, 4, 5, 2)).reshape(nbl * N_DIFF_HEADS * LANES, past)
    cvd = cache_diff_v.reshape(nbl * past, N_DIFF_HEADS * LANES)
    ckg = cache_gqa_k.reshape(nbl * past, N_GQA_KV_HEADS * LANES)
    cvg = cache_gqa_v.reshape(nbl * past, N_GQA_KV_HEADS * LANES)
    odl = _attn_diff_call(qkv, ckd, cvd, lamv, diff_subln_g, row0=0, nb=nbl, n=n_lat, tq=1024, hps=1)
    ogl = _attn_gqa_call(qkv, ckg, cvg, row0=0, nb=nbl, n=n_lat, tq=256, kvps=N_GQA_KV_HEADS)
    odc = _attn_diff_call(qkv, None, None, lamv, diff_subln_g, row0=tl, nb=nbc, n=seq, tq=seq,
                          hps=N_DIFF_HEADS)
    ogc = _attn_gqa_call(qkv, None, None, row0=tl, nb=nbc, n=seq, tq=seq, kvps=N_GQA_KV_HEADS)

    x1, h2, aff = _outproj_call(odl, ogl, odc, ogc, xs, xp, mod, norm2_g, w_out[0].astype(BF16),
                                _split3_bf16(w_router[0]), ne, n_lat)
    aff_t = aff[:, :ne].T

    slot_l, starts_l = _route_call(aff_t, col0=0, n_sets=nbl, n=n_lat, offset_stride=0, name="route_lat")
    slot_c, starts_c = _route_call(aff_t, col0=tl, n_sets=nbc, n=seq,
                                   offset_stride=CAPACITY_FACTOR * seq // ne, name="route_ctx")
    slot_t = jnp.concatenate([slot_l, slot_c], axis=1)
    tab_l = starts_l[:, :, :tpg + 1]
    tab_c = jnp.concatenate([starts_c[:, :, 0].T, starts_c[-1:, :, 1].T], axis=1)[None]
    tab = jnp.concatenate([tab_l, tab_c], axis=0).reshape(-1)

    x_slots, gate_slots = _dispatch_call(tab, h2, slot_t, aff_t, ng=ng, rows=rows, tiles_per_group=tpg)
    y = _moe_call(x_slots, gate_slots, w_expert_gate[0], w_expert_up[0], w_expert_down[0])

    ys, yp = _combine_call(tab, x1, mod, slot_t.T, final_norm_g[None, :], y,
                           n_lat_tokens=tl, n_ctx_tokens=tc, tiles_per_group=tpg)

    return (yp.reshape(nbc, seq, d), ys.reshape(nbl, n_lat, d),
            ndk.reshape(nbc, 1, N_DIFF_HEADS, 2, DIFF_QK_DIM, seq).transpose(0, 1, 5, 2, 3, 4),
            ndv.reshape(nbc, 1, seq, N_DIFF_HEADS, HEAD_DIM),
            ngk.reshape(nbc, 1, seq, N_GQA_KV_HEADS, HEAD_DIM),
            ngv.reshape(nbc, 1, seq, N_GQA_KV_HEADS, HEAD_DIM))
```

```python
import functools
import math

import jax
import jax.numpy as jnp
import numpy as np
from jax import lax
from jax.experimental import pallas as pl
from jax.experimental.pallas import tpu as pltpu

F32 = jnp.float32
BF16 = jnp.bfloat16
I32 = jnp.int32

LANES = 128
HEAD_DIM = 128
N_DIFF_HEADS = 8
DIFF_QK_DIM = HEAD_DIM // 2
N_GQA_HEADS = 8
N_GQA_KV_HEADS = 2
GQA_GROUP = N_GQA_HEADS // N_GQA_KV_HEADS
N_EXPERTS = 16
CAPACITY_FACTOR = 2
GRID_W = 64
ROPE_THETA = 10000.0
NORM_EPS = 1e-6
N_MOD = 6
LAMBDA_INIT_L0 = 0.8 - 0.6 * math.exp(-0.3 * 0)

DQ_BLK = 0
DK_BLK = DQ_BLK + N_DIFF_HEADS
DV_BLK = DK_BLK + N_DIFF_HEADS
GQ_BLK = DV_BLK + N_DIFF_HEADS
GK_BLK = GQ_BLK + N_GQA_HEADS
GV_BLK = GK_BLK + N_GQA_KV_HEADS
IN_BLKS = GV_BLK + N_GQA_KV_HEADS

TOK_TILE = 256
OUTPROJ_TILE = 512
OUTPROJ_SPLIT = 2
KV_CHUNK_MAX = 2048
DISPATCH_WIN = 64
DISPATCH_TILES_PER_STEP = 4
COMBINE_TILES_PER_STEP = 2
NEG_BIG = -1e30
LOG2_E = math.log2(math.e)
MIN_EXP = -150.0
EXP_STEPS = 8
MANTISSA_STEPS = 53

VMEM_LIMIT_MB = 56


def _params(n_axes, vmem_mb=VMEM_LIMIT_MB):
    return pltpu.CompilerParams(dimension_semantics=("arbitrary",) * n_axes,
                                vmem_limit_bytes=vmem_mb << 20)


def _sigmoid(x):
    return 1.0 / (1.0 + jnp.exp(-x))


def _rms(x, eps=NORM_EPS):
    return x * lax.rsqrt(jnp.mean(x * x, axis=-1, keepdims=True) + eps)


def _mod_kernel(c_ref, w_ref, b_ref, o_ref):
    c = c_ref[...]
    a = (c * _sigmoid(c)).astype(BF16)
    o_ref[...] = jnp.dot(a, w_ref[...].astype(BF16), preferred_element_type=F32) + b_ref[...]


def _mod_call(cond, w_mod, b_mod):
    d, m = w_mod.shape
    tn = next(t for t in (1536, 1024, 512, 256, LANES) if m % t == 0)
    return pl.pallas_call(
        _mod_kernel, name="mod",
        grid=(m // tn,),
        in_specs=[pl.BlockSpec((8, d), lambda j: (0, 0)),
                  pl.BlockSpec((d, tn), lambda j: (0, j)),
                  pl.BlockSpec((1, tn), lambda j: (0, j))],
        out_specs=pl.BlockSpec((8, tn), lambda j: (0, j)),
        out_shape=jax.ShapeDtypeStruct((8, m), F32),
        compiler_params=_params(1),
    )(cond, w_mod, b_mod)


def _rope(x, cos, sin_signed, half):
    lane = lax.broadcasted_iota(I32, x.shape, 1)
    first = (lane % (2 * half)) < half
    rot = jnp.where(first, pltpu.roll(x, LANES - half, 1), pltpu.roll(x, half, 1))
    return x * cos + rot * sin_signed


def _inproj_kernel(xs_ref, xp_ref, mod_ref, n1g_ref, w_ref, qn_ref, kn_ref,
                   cd_ref, sd_ref, cg_ref, sg_ref,
                   qkv_ref, ndk_ref, ndv_ref, ngk_ref, ngv_ref, *, n_lat_tiles):
    i = pl.program_id(0)
    is_ctx = i >= n_lat_tiles
    x = jnp.where(is_ctx, xp_ref[...], xs_ref[...])
    sh1 = mod_ref[0:1, :]
    sc1 = mod_ref[1:2, :]
    h = (_rms(x) * n1g_ref[...]) * (1.0 + sc1) + sh1
    hb = h.astype(BF16)
    cd, sd, cg, sg = cd_ref[...], sd_ref[...], cg_ref[...], sg_ref[...]
    qn, kn = qn_ref[...], kn_ref[...]
    cache_stores = []

    def seg(blk0, nblk):
        return jnp.dot(hb, w_ref[:, blk0 * LANES:(blk0 + nblk) * LANES], preferred_element_type=F32)

    def put(blk, val):
        qkv_ref[:, blk * LANES:(blk + 1) * LANES] = val.astype(BF16)

    p = seg(DQ_BLK, N_DIFF_HEADS)
    for j in range(N_DIFF_HEADS):
        v = p[:, j * LANES:(j + 1) * LANES]
        put(DQ_BLK + j, _rope(v, cd, sd, DIFF_QK_DIM // 4) * (DIFF_QK_DIM ** -0.5 * LOG2_E))
    p = seg(DK_BLK, N_DIFF_HEADS)
    for j in range(N_DIFF_HEADS):
        v = p[:, j * LANES:(j + 1) * LANES]
        cache_stores.append((ndk_ref, j, v))
        put(DK_BLK + j, _rope(v, cd, sd, DIFF_QK_DIM // 4))
    p = seg(DV_BLK, N_DIFF_HEADS)
    for j in range(N_DIFF_HEADS):
        v = p[:, j * LANES:(j + 1) * LANES]
        cache_stores.append((ndv_ref, j, v))
        put(DV_BLK + j, v)
    p = seg(GQ_BLK, N_GQA_HEADS)
    for j in range(N_GQA_HEADS):
        v = _rms(p[:, j * LANES:(j + 1) * LANES]) * qn
        put(GQ_BLK + j, _rope(v, cg, sg, HEAD_DIM // 4) * (HEAD_DIM ** -0.5 * LOG2_E))
    p = seg(GK_BLK, 2 * N_GQA_KV_HEADS)
    for j in range(N_GQA_KV_HEADS):
        v = _rms(p[:, j * LANES:(j + 1) * LANES]) * kn
        cache_stores.append((ngk_ref, j, v))
        put(GK_BLK + j, _rope(v, cg, sg, HEAD_DIM // 4))
    for j in range(N_GQA_KV_HEADS):
        v = p[:, (N_GQA_KV_HEADS + j) * LANES:(N_GQA_KV_HEADS + j + 1) * LANES]
        cache_stores.append((ngv_ref, j, v))
        put(GV_BLK + j, v)

    @pl.when(is_ctx)
    def _():
        for ref, j, v in cache_stores:
            if ref is ndk_ref:
                ref[j * LANES:(j + 1) * LANES, :] = v.T
            else:
                ref[:, j * LANES:(j + 1) * LANES] = v


def _inproj_call(xs, xp, mod, n1g, w_in_b, qn, kn, tables, n_lat):
    tl, d = xs.shape
    tc = xp.shape[0]
    tm = TOK_TILE
    nl, nc = tl // tm, tc // tm
    tpb = n_lat // tm
    nbl = tl // n_lat
    width = IN_BLKS * LANES

    def tab_spec():
        return pl.BlockSpec((tm, LANES), lambda i: (jnp.where(i < nl, i % tpb, tpb), 0))

    def cache_spec(w):
        return pl.BlockSpec((tm, w), lambda i: (jnp.maximum(i - nl, 0), 0))

    dw, gw = N_DIFF_HEADS * LANES, N_GQA_KV_HEADS * LANES
    return pl.pallas_call(
        functools.partial(_inproj_kernel, n_lat_tiles=nl), name="inproj",
        grid=(nl + nc,),
        in_specs=[pl.BlockSpec((tm, d), lambda i: (jnp.minimum(i, nl - 1), 0)),
                  pl.BlockSpec((tm, d), lambda i: (jnp.maximum(i - nl, 0), 0)),
                  pl.BlockSpec((None, N_MOD, d), lambda i: (jnp.where(i < nl, i // tpb, nbl), 0, 0)),
                  pl.BlockSpec((1, d), lambda i: (0, 0)),
                  pl.BlockSpec((d, width), lambda i: (0, 0), pipeline_mode=pl.Buffered(1)),
                  pl.BlockSpec((1, LANES), lambda i: (0, 0)),
                  pl.BlockSpec((1, LANES), lambda i: (0, 0)),
                  tab_spec(), tab_spec(), tab_spec(), tab_spec()],
        out_specs=[pl.BlockSpec((tm, width), lambda i: (i, 0)),
                   pl.BlockSpec((None, dw, tm), lambda i: (jnp.maximum(i - nl, 0), 0, 0)),
                   cache_spec(dw), cache_spec(gw), cache_spec(gw)],
        out_shape=[jax.ShapeDtypeStruct((tl + tc, width), BF16),
                   jax.ShapeDtypeStruct((nc, dw, tm), F32), jax.ShapeDtypeStruct((tc, dw), F32),
                   jax.ShapeDtypeStruct((tc, gw), F32), jax.ShapeDtypeStruct((tc, gw), F32)],
        compiler_params=_params(1),
    )(xs, xp, mod, n1g, w_in_b, qn, kn, *tables)


def _rope_tables(n, dim, pad_rows):
    t = np.arange(n)
    row = (t // GRID_W).astype(np.float64)
    col = (t % GRID_W).astype(np.float64)
    sec = dim // 2
    inv = ROPE_THETA ** (-np.arange(0, sec, 2, dtype=np.float64) / sec)
    ang = np.stack([row[:, None] * inv, col[:, None] * inv], axis=1)
    ang = np.concatenate([ang, ang], axis=-1).reshape(n, dim)
    sign = np.where((np.arange(dim) % sec) < sec // 2, -1.0, 1.0)
    cos, sin = np.cos(ang), np.sin(ang) * sign
    reps = LANES // dim
    cos, sin = np.tile(cos, (1, reps)), np.tile(sin, (1, reps))
    cos = np.concatenate([cos, np.ones((pad_rows, LANES))], axis=0)
    sin = np.concatenate([sin, np.zeros((pad_rows, LANES))], axis=0)
    return jnp.asarray(cos, F32), jnp.asarray(sin, F32)


def _flash(q, chunks):
    m_rows = q.shape[0]
    m = jnp.full((m_rows, 1), NEG_BIG, F32)
    acc = jnp.zeros((m_rows, HEAD_DIM), F32)
    l = jnp.zeros((m_rows, 1), F32)
    for scores, values in chunks:
        s = scores(q)
        m_new = jnp.maximum(m, jnp.max(s, axis=-1, keepdims=True))
        alpha = jnp.exp2(m - m_new)
        p = jnp.exp2(s - m_new)
        l = alpha * l + jnp.sum(p, axis=-1, keepdims=True)
        acc = alpha * acc + jnp.dot(p.astype(BF16), values(), preferred_element_type=F32)
        m = m_new
    return acc / l


def _scores_nt(k):
    return lambda q: lax.dot_general(q, k(), (((1,), (1,)), ((), ())), preferred_element_type=F32)


def _kv_chunks(k_ref, v_ref, ck_ref, cv_ref, col, cache_k_transposed):
    cs = slice(col * LANES, (col + 1) * LANES)
    chunks = []
    if ck_ref is not None:
        if cache_k_transposed:
            scores = lambda q: jnp.dot(q, ck_ref[cs, :].astype(BF16), preferred_element_type=F32)
        else:
            scores = _scores_nt(lambda: ck_ref[:, cs].astype(BF16))
        chunks.append((scores, lambda: cv_ref[:, cs].astype(BF16)))
    n = k_ref.shape[0]
    n_chunks = pl.cdiv(n, KV_CHUNK_MAX)
    assert n % n_chunks == 0
    step = n // n_chunks
    for c in range(n_chunks):
        rs = slice(c * step, (c + 1) * step)
        chunks.append((_scores_nt(lambda rs=rs: k_ref[rs, cs]), lambda rs=rs: v_ref[rs, cs]))
    return chunks


def _attn_diff_kernel(*refs, hps, has_cache):
    if has_cache:
        q_ref, k_ref, v_ref, ck_ref, cv_ref, lam_ref, g_ref, o_ref = refs
    else:
        q_ref, k_ref, v_ref, lam_ref, g_ref, o_ref = refs
        ck_ref = cv_ref = None
    lv = lam_ref[...]
    lam = (jnp.exp(jnp.sum(lv[0:1] * lv[1:2], axis=-1, keepdims=True))
           - jnp.exp(jnp.sum(lv[2:3] * lv[3:4], axis=-1, keepdims=True)) + LAMBDA_INIT_L0)
    for j in range(hps):
        q = q_ref[:, j * LANES:(j + 1) * LANES]
        lane = lax.broadcasted_iota(I32, q.shape, 1)
        zero = jnp.zeros_like(q)
        chunks = _kv_chunks(k_ref, v_ref, ck_ref, cv_ref, j, cache_k_transposed=True)
        o1 = _flash(jnp.where(lane < DIFF_QK_DIM, q, zero), chunks)
        o2 = _flash(jnp.where(lane >= DIFF_QK_DIM, q, zero), chunks)
        o = o1 - lam * o2
        o = (_rms(o) * g_ref[...]) * (1.0 - LAMBDA_INIT_L0)
        o_ref[:, j * LANES:(j + 1) * LANES] = o.astype(o_ref.dtype)


def _attn_diff_call(qkv, ck, cv, lamv, subg, *, row0, nb, n, tq, hps):
    w = hps * LANES
    hb_n = N_DIFF_HEADS // hps
    qpb = n // tq
    in_specs = [pl.BlockSpec((tq, w), lambda b, h, qi: (row0 // tq + b * qpb + qi, DQ_BLK // hps + h)),
                pl.BlockSpec((n, w), lambda b, h, qi: (row0 // n + b, DK_BLK // hps + h)),
                pl.BlockSpec((n, w), lambda b, h, qi: (row0 // n + b, DV_BLK // hps + h))]
    args = [qkv, qkv, qkv]
    if ck is not None:
        past = ck.shape[1]
        in_specs += [pl.BlockSpec((w, past), lambda b, h, qi: (b * hb_n + h, 0)),
                     pl.BlockSpec((past, w), lambda b, h, qi: (b, h))]
        args += [ck, cv]
    in_specs += [pl.BlockSpec(lamv.shape, lambda b, h, qi: (0, 0)),
                 pl.BlockSpec((1, LANES), lambda b, h, qi: (0, 0))]
    args += [lamv, subg]
    return pl.pallas_call(
        functools.partial(_attn_diff_kernel, hps=hps, has_cache=ck is not None),
        name="attn_diff_lat" if ck is not None else "attn_diff_ctx",
        grid=(nb, hb_n, qpb),
        in_specs=in_specs,
        out_specs=pl.BlockSpec((tq, w), lambda b, h, qi: (b * qpb + qi, h)),
        out_shape=jax.ShapeDtypeStruct((nb * n, N_DIFF_HEADS * LANES), BF16),
        compiler_params=_params(3),
    )(*args)


def _attn_gqa_kernel(*refs, kvps, has_cache):
    if has_cache:
        q_ref, k_ref, v_ref, ck_ref, cv_ref, o_ref = refs
    else:
        q_ref, k_ref, v_ref, o_ref = refs
        ck_ref = cv_ref = None
    tq = q_ref.shape[0]
    for j in range(kvps):
        c0 = j * GQA_GROUP
        q = jnp.concatenate([q_ref[:, (c0 + r) * LANES:(c0 + r + 1) * LANES] for r in range(GQA_GROUP)],
                            axis=0)
        chunks = _kv_chunks(k_ref, v_ref, ck_ref, cv_ref, j, cache_k_transposed=False)
        o = _flash(q, chunks).astype(o_ref.dtype)
        for r in range(GQA_GROUP):
            o_ref[:, (c0 + r) * LANES:(c0 + r + 1) * LANES] = o[r * tq:(r + 1) * tq]


def _attn_gqa_call(qkv, ck, cv, *, row0, nb, n, tq, kvps):
    w = kvps * GQA_GROUP * LANES
    kw = kvps * LANES
    qpb = n // tq
    in_specs = [pl.BlockSpec((tq, w), lambda b, g, qi: (row0 // tq + b * qpb + qi, GQ_BLK * LANES // w + g)),
                pl.BlockSpec((n, kw), lambda b, g, qi: (row0 // n + b, GK_BLK // kvps + g)),
                pl.BlockSpec((n, kw), lambda b, g, qi: (row0 // n + b, GV_BLK // kvps + g))]
    args = [qkv, qkv, qkv]
    if ck is not None:
        past = ck.shape[0] // nb
        in_specs += [pl.BlockSpec((past, kw), lambda b, g, qi: (b, g)),
                     pl.BlockSpec((past, kw), lambda b, g, qi: (b, g))]
        args += [ck, cv]
    return pl.pallas_call(
        functools.partial(_attn_gqa_kernel, kvps=kvps, has_cache=ck is not None),
        name="attn_gqa_lat" if ck is not None else "attn_gqa_ctx",
        grid=(nb, N_GQA_KV_HEADS // kvps, qpb),
        in_specs=in_specs,
        out_specs=pl.BlockSpec((tq, w), lambda b, g, qi: (b * qpb + qi, g)),
        out_shape=jax.ShapeDtypeStruct((nb * n, N_GQA_HEADS * LANES), BF16),
        compiler_params=_params(3),
    )(*args)


def _outproj_kernel(odl_ref, ogl_ref, odc_ref, ogc_ref, xs_ref, xp_ref, mod_ref, n2g_ref, w_ref, wr3_ref,
                    x1_ref, h2_ref, aff_ref, *, n_lat_tiles, ne):
    i = pl.program_id(0)
    is_ctx = i >= n_lat_tiles
    dw = odl_ref.shape[1]
    g1 = mod_ref[2:3, :]
    sh2 = mod_ref[3:4, :]
    sc2 = mod_ref[4:5, :]
    w3 = wr3_ref[...]
    tm = x1_ref.shape[0]
    half = tm // OUTPROJ_SPLIT
    projected = []
    for r in range(OUTPROJ_SPLIT):
        rs = slice(r * half, (r + 1) * half)
        od = jnp.where(is_ctx, odc_ref[rs, :], odl_ref[rs, :])
        og = jnp.where(is_ctx, ogc_ref[rs, :], ogl_ref[rs, :])
        projected.append(jnp.dot(od, w_ref[0:dw, :], preferred_element_type=F32)
                         + jnp.dot(og, w_ref[dw:, :], preferred_element_type=F32))
    for r in range(OUTPROJ_SPLIT):
        rs = slice(r * half, (r + 1) * half)
        x = jnp.where(is_ctx, xp_ref[rs, :], xs_ref[rs, :])
        x1 = x + g1 * projected[r]
        x1_ref[rs, :] = x1
        h2 = (_rms(x1) * n2g_ref[...]) * (1.0 + sc2) + sh2
        h_hi = h2.astype(BF16)
        h2_ref[rs, :] = h_hi
        h_mid = (h2 - h_hi.astype(F32)).astype(BF16)
        t = (jnp.dot(h_mid, w3, preferred_element_type=F32) + jnp.dot(h_hi, w3, preferred_element_type=F32))
        t = (pltpu.roll(t, LANES - 2 * ne, 1) + pltpu.roll(t, LANES - ne, 1)) + t
        lane = lax.broadcasted_iota(I32, t.shape, 1)
        logits = jnp.where(lane < ne, t, NEG_BIG)
        z = jnp.exp(logits - jnp.max(logits, axis=-1, keepdims=True))
        aff_ref[rs, :] = z / jnp.sum(z, axis=-1, keepdims=True)


def _split3_bf16(w):
    hi = w.astype(BF16)
    rem = w - hi.astype(F32)
    mid = rem.astype(BF16)
    lo = (rem - mid.astype(F32)).astype(BF16)
    pad = jnp.zeros((w.shape[0], LANES - 3 * w.shape[1]), BF16)
    return jnp.concatenate([hi, mid, lo, pad], axis=1)


def _outproj_call(odl, ogl, odc, ogc, xs, xp, mod, n2g, w_out_b, wr3, ne, n_lat):
    tl, d = xs.shape
    tc = xp.shape[0]
    tm = OUTPROJ_TILE
    nl, nc = tl // tm, tc // tm
    tpb = n_lat // tm
    nbl = tl // n_lat
    aw = odl.shape[1]

    def lat_spec(w):
        return pl.BlockSpec((tm, w), lambda i: (jnp.minimum(i, nl - 1), 0))

    def ctx_spec(w):
        return pl.BlockSpec((tm, w), lambda i: (jnp.maximum(i - nl, 0), 0))

    return pl.pallas_call(
        functools.partial(_outproj_kernel, n_lat_tiles=nl, ne=ne), name="outproj",
        grid=(nl + nc,),
        in_specs=[lat_spec(aw), lat_spec(aw), ctx_spec(aw), ctx_spec(aw), lat_spec(d), ctx_spec(d),
                  pl.BlockSpec((None, N_MOD, d), lambda i: (jnp.where(i < nl, i // tpb, nbl), 0, 0)),
                  pl.BlockSpec((1, d), lambda i: (0, 0)),
                  pl.BlockSpec(w_out_b.shape, lambda i: (0, 0), pipeline_mode=pl.Buffered(1)),
                  pl.BlockSpec((d, LANES), lambda i: (0, 0))],
        out_specs=[pl.BlockSpec((tm, d), lambda i: (i, 0)),
                   pl.BlockSpec((tm, d), lambda i: (i, 0)),
                   pl.BlockSpec((tm, LANES), lambda i: (i, 0))],
        out_shape=[jax.ShapeDtypeStruct((tl + tc, d), F32),
                   jax.ShapeDtypeStruct((tl + tc, d), BF16),
                   jax.ShapeDtypeStruct((tl + tc, LANES), F32)],
        compiler_params=_params(1),
    )(odl, ogl, odc, ogc, xs, xp, mod, n2g, w_out_b, wr3)


def _route_kernel(aff_ref, slot_ref, starts_ref, *, cap, n, offset_stride):
    ne = aff_ref.shape[0]
    n_sets = aff_ref.shape[1] // n
    tt = TOK_TILE
    aff = jnp.concatenate([aff_ref[:, s * n:(s + 1) * n] for s in range(n_sets)], axis=0)
    capf = float(cap)
    ne_all = n_sets * ne

    def count(mask):
        return jnp.sum(mask.astype(F32), axis=1, keepdims=True)

    def threshold(e):
        return jnp.where(e <= MIN_EXP, 0.0, jnp.exp2(e))

    def exp_step(_, carry):
        e_lo, e_hi = carry
        e_mid = jnp.floor(0.5 * (e_lo + e_hi))
        ok = count(aff >= threshold(e_mid)) >= capf
        return jnp.where(ok, e_mid, e_lo), jnp.where(ok, e_hi, e_mid)

    e_lo, e_hi = lax.fori_loop(0, EXP_STEPS, exp_step,
                               (jnp.full((ne_all, 1), MIN_EXP, F32), jnp.full((ne_all, 1), 1.0, F32)))

    def bisect(_, carry):
        lo, hi = carry
        mid = 0.5 * (lo + hi)
        ok = count(aff >= mid) >= capf
        return jnp.where(ok, mid, lo), jnp.where(ok, hi, mid)

    thr, _ = lax.fori_loop(0, MANTISSA_STEPS, bisect, (threshold(e_lo), threshold(e_hi)))
    gt = aff > thr
    tie = aff == thr
    need = capf - count(gt)

    ri = lax.broadcasted_iota(I32, (tt, tt), 0)
    ci = lax.broadcasted_iota(I32, (tt, tt), 1)
    upper = (ri < ci).astype(BF16)
    set_id = lax.broadcasted_iota(I32, (ne_all, 1), 0) // ne
    tie_pref = jnp.zeros((ne_all, 1), F32)
    sel_pref = (set_id * offset_stride).astype(F32)
    lane = lax.broadcasted_iota(I32, (ne_all, LANES), 1)
    starts = jnp.zeros((ne_all, LANES), F32)
    for t in range(n // tt):
        cs = slice(t * tt, (t + 1) * tt)
        tie_t = tie[:, cs]
        tie_rank = tie_pref + jnp.dot(tie_t.astype(BF16), upper, preferred_element_type=F32)
        sel_t = gt[:, cs] | (tie_t & (tie_rank < need))
        slot_t = sel_pref + jnp.dot(sel_t.astype(BF16), upper, preferred_element_type=F32)
        slot_t = jnp.where(sel_t, slot_t, -1.0).astype(I32)
        for s in range(n_sets):
            slot_ref[:, s * n + t * tt:s * n + (t + 1) * tt] = slot_t[s * ne:(s + 1) * ne]
        starts = jnp.where(lane == t, sel_pref, starts)
        tie_pref = tie_pref + count(tie_t)
        sel_pref = sel_pref + count(sel_t)
    starts = jnp.where(lane == n // tt, sel_pref, starts).astype(I32)
    for s in range(n_sets):
        starts_ref[s] = starts[s * ne:(s + 1) * ne]


def _route_call(aff_t, *, col0, n_sets, n, offset_stride, name):
    ne = aff_t.shape[0]
    cap = CAPACITY_FACTOR * n // N_EXPERTS
    w = n_sets * n
    return pl.pallas_call(
        functools.partial(_route_kernel, cap=cap, n=n, offset_stride=offset_stride), name=name,
        grid=(1,),
        in_specs=[pl.BlockSpec((ne, w), lambda s: (0, col0 // w))],
        out_specs=[pl.BlockSpec((ne, w), lambda s: (0, 0)),
                   pl.BlockSpec((n_sets, ne, LANES), lambda s: (0, 0, 0))],
        out_shape=[jax.ShapeDtypeStruct((ne, w), I32),
                   jax.ShapeDtypeStruct((n_sets, ne, LANES), I32)],
        compiler_params=_params(1),
    )(aff_t)


def _window_start(first_slot, rows):
    return pl.multiple_of(jnp.minimum((first_slot // 16) * 16, rows - DISPATCH_WIN), 16)


def _dispatch_kernel(tab_ref, h2_ref, slot_ref, gate_ref, x_ref, gc_ref, *, tiles_per_group, ne):
    g, eb, tb = pl.program_id(0), pl.program_id(1), pl.program_id(2)
    epb, rows = x_ref.shape[0], x_ref.shape[1]
    tt = TOK_TILE
    win = DISPATCH_WIN

    @pl.when(tb == 0)
    def _():
        x_ref[...] = jnp.zeros_like(x_ref)
        gc_ref[...] = jnp.zeros_like(gc_ref)

    sub = lax.broadcasted_iota(I32, (win, tt), 0)

    for s in range(h2_ref.shape[0] // tt):
        t = tb * (h2_ref.shape[0] // tt) + s
        cs = slice(s * tt, (s + 1) * tt)
        h_t = h2_ref[cs, :]

        def scatter(i, start, hit, cs=cs):
            gwin = jnp.sum(jnp.where(hit, gate_ref[i:i + 1, cs], 0.0), axis=1, keepdims=True)
            gc_ref[i, pl.ds(start, win), :] += jnp.broadcast_to(gwin, (win, LANES))

        starts, hits = [], []
        for i in range(epb):
            base = (g * ne + eb * epb + i) * (tiles_per_group + 1) + t
            start = _window_start(tab_ref[base], rows)
            starts.append(start)
            hits.append(slot_ref[i:i + 1, cs] - start == sub)
        onehot = jnp.concatenate([h.astype(BF16) for h in hits], axis=0)
        moved = jnp.dot(onehot, h_t, preferred_element_type=F32).astype(BF16)
        for i in range(epb):
            x_ref[i, pl.ds(starts[i], win), :] += moved[i * win:(i + 1) * win]
            scatter(i, starts[i], hits[i])

        spans = [tab_ref[(g * ne + eb * epb + i) * (tiles_per_group + 1) + t + 1] - starts[i]
                 for i in range(epb)]

        @pl.when(functools.reduce(jnp.logical_or, [span > win for span in spans]))
        def _(starts=starts, spans=spans, cs=cs, h_t=h_t, scatter=scatter):
            for i in range(epb):
                def extra(w, c, i=i, first=starts[i]):
                    lo = first + w * win
                    start = _window_start(lo, rows)
                    srow = slot_ref[i:i + 1, cs]
                    hit = (srow - start == sub) & (srow >= lo)
                    x_ref[i, pl.ds(start, win), :] += jnp.dot(hit.astype(BF16), h_t,
                                                              preferred_element_type=F32).astype(BF16)
                    scatter(i, start, hit)
                    return c

                lax.fori_loop(1, (spans[i] + win - 1) // win, extra, 0)


def _dispatch_call(tab, h2, slot_t, gate_t, *, ng, rows, tiles_per_group):
    t_all, d = h2.shape
    ne = slot_t.shape[0]
    epb = 8
    tps = DISPATCH_TILES_PER_STEP
    steps = tiles_per_group // tps
    tw = tps * TOK_TILE

    def tok_spec(w):
        return pl.BlockSpec((w, tw), lambda g, eb, t, tab: (eb, g * steps + t))

    grid_spec = pltpu.PrefetchScalarGridSpec(
        num_scalar_prefetch=1,
        grid=(ng, ne // epb, steps),
        in_specs=[pl.BlockSpec((tw, d), lambda g, eb, t, tab: (g * steps + t, 0)),
                  tok_spec(epb), tok_spec(epb)],
        out_specs=[pl.BlockSpec((None, epb, rows, d), lambda g, eb, t, tab: (g, eb, 0, 0)),
                   pl.BlockSpec((None, epb, rows, LANES), lambda g, eb, t, tab: (g, eb, 0, 0))])
    return pl.pallas_call(
        functools.partial(_dispatch_kernel, tiles_per_group=tiles_per_group, ne=ne), name="dispatch",
        grid_spec=grid_spec,
        out_shape=[jax.ShapeDtypeStruct((ng, ne, rows, d), BF16),
                   jax.ShapeDtypeStruct((ng, ne, rows, LANES), F32)],
        compiler_params=_params(3),
    )(tab, h2, slot_t, gate_t)


def _moe_kernel(x_ref, gc_ref, wg_ref, wu_ref, wd_ref, y_ref, yacc_ref, *, nf):
    f = pl.program_id(1)
    ng = x_ref.shape[0]
    tf = wg_ref.shape[1]

    def partial_outputs():
        w_gu = jnp.concatenate([wg_ref[...].astype(BF16), wu_ref[...].astype(BF16)], axis=1)
        wd = wd_ref[...].astype(BF16)
        for g in range(ng):
            gu = jnp.dot(x_ref[g], w_gu, preferred_element_type=F32)
            gate, up = gu[:, :tf], gu[:, tf:]
            hid = (gate * _sigmoid(gate) * up).astype(BF16)
            yield g, jnp.dot(hid, wd, preferred_element_type=F32)

    def finish(g, total):
        y_ref[g] = (total * gc_ref[g][:, 0:1]).astype(y_ref.dtype)

    if nf == 1:
        for g, part in partial_outputs():
            finish(g, part)
        return

    @pl.when(f == 0)
    def _():
        for g, part in partial_outputs():
            yacc_ref[g] = part

    @pl.when(jnp.logical_and(f > 0, f < nf - 1))
    def _():
        for g, part in partial_outputs():
            yacc_ref[g] += part

    @pl.when(f == nf - 1)
    def _():
        for g, part in partial_outputs():
            finish(g, yacc_ref[g] + part)


def _moe_call(x, gc, w_gate, w_up, w_down):
    ng, ne, rows, d = x.shape
    fdim = w_gate.shape[2]
    tf = min(fdim, 256)
    return pl.pallas_call(
        functools.partial(_moe_kernel, nf=fdim // tf), name="moe",
        grid=(ne, fdim // tf),
        in_specs=[pl.BlockSpec((ng, None, rows, d), lambda e, f: (0, e, 0, 0)),
                  pl.BlockSpec((ng, None, rows, LANES), lambda e, f: (0, e, 0, 0)),
                  pl.BlockSpec((None, d, tf), lambda e, f: (e, 0, f)),
                  pl.BlockSpec((None, d, tf), lambda e, f: (e, 0, f)),
                  pl.BlockSpec((None, tf, d), lambda e, f: (e, f, 0))],
        out_specs=pl.BlockSpec((ng, None, rows, d), lambda e, f: (0, e, 0, 0)),
        out_shape=jax.ShapeDtypeStruct((ng, ne, rows, d), BF16),
        scratch_shapes=[pltpu.VMEM((ng, rows, d), F32)],
        compiler_params=_params(2),
    )(x, gc, w_gate, w_up, w_down)


def _combine_kernel(tab_ref, x1_ref, mod_ref, slot_ref, fng_ref, y_hbm, ys_ref, yp_ref,
                    ybuf, obuf, acc_ref, sem, osem, *, tiles_per_group, n_lat_tiles, rows):
    i = pl.program_id(0)
    n_steps = pl.num_programs(0)
    tps, ne = sem.shape[1], sem.shape[2]
    win = DISPATCH_WIN
    tt = TOK_TILE
    per = LANES // win

    def first_slot(tile, e):
        return tab_ref[((tile // tiles_per_group) * ne + e) * (tiles_per_group + 1) + tile % tiles_per_group]

    def window_copies(step, buf):
        copies = []
        for s in range(tps):
            tile = step * tps + s
            g = tile // tiles_per_group
            copies += [pltpu.make_async_copy(
                y_hbm.at[g, e, pl.ds(_window_start(first_slot(tile, e), rows), win), :],
                ybuf.at[buf, s, pl.ds(e * win, win), :], sem.at[buf, s, e]) for e in range(ne)]
        return copies

    buf = i % 2

    @pl.when(i == 0)
    def _():
        for cp in window_copies(i, buf):
            cp.start()

    @pl.when(i + 1 < n_steps)
    def _():
        for cp in window_copies(i + 1, 1 - buf):
            cp.start()

    lane = lax.broadcasted_iota(I32, (tt, LANES), 1)
    tile_starts, onehots = [], []
    for s in range(tps):
        rs = slice(s * tt, (s + 1) * tt)
        starts = [_window_start(first_slot(i * tps + s, e), rows) for e in range(ne)]
        blocks = []
        for k in range(ne // per):
            slot = slot_ref[rs, k * per:k * per + 1]
            start = starts[k * per]
            for j in range(1, per):
                pick = lane >= j * win
                slot = jnp.where(pick, slot_ref[rs, k * per + j:k * per + j + 1], slot)
                start = jnp.where(pick, starts[k * per + j], start)
            blocks.append((slot - start == lane % win).astype(BF16))
        tile_starts.append(starts)
        onehots.append(jnp.concatenate(blocks, axis=1))

    for cp in window_copies(i, buf):
        cp.wait()
    for s in range(tps):
        acc_ref[s * tt:(s + 1) * tt, :] = jnp.dot(onehots[s], ybuf[buf, s], preferred_element_type=F32)

    for s in range(tps):
        rs = slice(s * tt, (s + 1) * tt)
        tile = i * tps + s
        g = tile // tiles_per_group
        starts = tile_starts[s]
        spans = [tab_ref[(g * ne + e) * (tiles_per_group + 1) + tile % tiles_per_group + 1] - starts[e]
                 for e in range(ne)]

        @pl.when(functools.reduce(jnp.logical_or, [span > win for span in spans]))
        def _(rs=rs, g=g, starts=starts, spans=spans):
            lane_w = lax.broadcasted_iota(I32, (tt, win), 1)
            for e in range(ne):
                def extra(w, c, e=e, first=starts[e]):
                    lo = first + w * win
                    start = _window_start(lo, rows)
                    cp = pltpu.make_async_copy(y_hbm.at[g, e, pl.ds(start, win), :], obuf, osem.at[0])
                    cp.start()
                    cp.wait()
                    slot = slot_ref[rs, e:e + 1]
                    hit = ((slot - start == lane_w) & (slot >= lo)).astype(BF16)
                    acc_ref[rs, :] += jnp.dot(hit, obuf[...], preferred_element_type=F32)
                    return c

                lax.fori_loop(1, (spans[e] + win - 1) // win, extra, 0)

    g2 = mod_ref[5:6, :]
    x2 = x1_ref[...] + g2 * acc_ref[...]
    y = _rms(x2) * fng_ref[...]

    @pl.when(i < n_lat_tiles)
    def _():
        ys_ref[...] = y

    @pl.when(i >= n_lat_tiles)
    def _():
        yp_ref[...] = y


def _combine_call(tab, x1, mod, slot_n, fng, y, *, n_lat_tokens, n_ctx_tokens, tiles_per_group):
    t_all, d = x1.shape
    tps = COMBINE_TILES_PER_STEP
    tm = tps * TOK_TILE
    nl = n_lat_tokens // tm
    ne = slot_n.shape[1]
    rows = y.shape[2]
    assert LANES % DISPATCH_WIN == 0 and ne % (LANES // DISPATCH_WIN) == 0 and tiles_per_group % tps == 0
    grid_spec = pltpu.PrefetchScalarGridSpec(
        num_scalar_prefetch=1,
        grid=(t_all // tm,),
        in_specs=[pl.BlockSpec((tm, d), lambda i, tab: (i, 0)),
                  pl.BlockSpec((None, N_MOD, d), lambda i, tab: (i * tps // tiles_per_group, 0, 0)),
                  pl.BlockSpec((tm, ne), lambda i, tab: (i, 0)),
                  pl.BlockSpec((1, d), lambda i, tab: (0, 0)),
                  pl.BlockSpec(memory_space=pl.ANY)],
        out_specs=[pl.BlockSpec((tm, d), lambda i, tab: (jnp.minimum(i, nl - 1), 0)),
                   pl.BlockSpec((tm, d), lambda i, tab: (jnp.maximum(i - nl, 0), 0))],
        scratch_shapes=[pltpu.VMEM((2, tps, ne * DISPATCH_WIN, d), BF16),
                        pltpu.VMEM((DISPATCH_WIN, d), BF16),
                        pltpu.VMEM((tm, d), F32),
                        pltpu.SemaphoreType.DMA((2, tps, ne)),
                        pltpu.SemaphoreType.DMA((1,))])
    return pl.pallas_call(
        functools.partial(_combine_kernel, tiles_per_group=tiles_per_group, n_lat_tiles=nl, rows=rows),
        name="combine",
        grid_spec=grid_spec,
        out_shape=[jax.ShapeDtypeStruct((n_lat_tokens, d), F32),
                   jax.ShapeDtypeStruct((n_ctx_tokens, d), F32)],
        compiler_params=_params(1),
    )(tab, x1, mod, slot_n, fng, y)


def kernel(x_prompt, x_sample, c, cache_diff_k, cache_diff_v, cache_gqa_k, cache_gqa_v, c_ctx, norm1_g, norm2_g, w_mod, b_mod, w_in, w_out, diff_lambda_q1, diff_lambda_k1, diff_lambda_q2, diff_lambda_k2, diff_subln_g, gqa_q_norm_g, gqa_k_norm_g, w_router, w_expert_gate, w_expert_up, w_expert_down, final_norm_g):
    nbc, seq, d = x_prompt.shape
    nbl, n_lat, _ = x_sample.shape
    depth, past = cache_diff_k.shape[1], cache_diff_k.shape[2]
    tc, tl = nbc * seq, nbl * n_lat
    assert depth == 1, "single trunk layer"
    assert tc == n_lat, "the context tokens must fill exactly one routing group"
    assert seq == TOK_TILE and nbl + 1 <= 8
    ne = N_EXPERTS
    tpg = n_lat // TOK_TILE
    ng = nbl + 1
    rows = CAPACITY_FACTOR * n_lat // ne

    xs = x_sample.reshape(tl, d)
    xp = x_prompt.reshape(tc, d)

    cond = jnp.concatenate([c, c_ctx[None, :], jnp.zeros((8 - ng, d), F32)], axis=0)
    mod = _mod_call(cond, w_mod[0], b_mod).reshape(8, N_MOD, d)

    tables = (_rope_tables(n_lat, DIFF_QK_DIM, TOK_TILE) + _rope_tables(n_lat, HEAD_DIM, TOK_TILE))
    qkv, ndk, ndv, ngk, ngv = _inproj_call(
        xs, xp, mod, norm1_g, w_in[0].astype(BF16), gqa_q_norm_g, gqa_k_norm_g, tables, n_lat)

    lamv = jnp.concatenate([diff_lambda_q1, diff_lambda_k1, diff_lambda_q2, diff_lambda_k2], axis=0)
    ckd = jnp.transpose(cache_diff_k, (0, 1, 3, 4, 5, 2)).reshape(nbl * N_DIFF_HEADS * LANES, past)
    cvd = cache_diff_v.reshape(nbl * past, N_DIFF_HEADS * LANES)
    ckg = cache_gqa_k.reshape(nbl * past, N_GQA_KV_HEADS * LANES)
    cvg = cache_gqa_v.reshape(nbl * past, N_GQA_KV_HEADS * LANES)
    odl = _attn_diff_call(qkv, ckd, cvd, lamv, diff_subln_g, row0=0, nb=nbl, n=n_lat, tq=1024, hps=1)
    ogl = _attn_gqa_call(qkv, ckg, cvg, row0=0, nb=nbl, n=n_lat, tq=256, kvps=N_GQA_KV_HEADS)
    odc = _attn_diff_call(qkv, None, None, lamv, diff_subln_g, row0=tl, nb=nbc, n=seq, tq=seq,
                          hps=N_DIFF_HEADS)
    ogc = _attn_gqa_call(qkv, None, None, row0=tl, nb=nbc, n=seq, tq=seq, kvps=N_GQA_KV_HEADS)

    x1, h2, aff = _outproj_call(odl, ogl, odc, ogc, xs, xp, mod, norm2_g, w_out[0].astype(BF16),
                                _split3_bf16(w_router[0]), ne, n_lat)
    aff_t = aff[:, :ne].T

    slot_l, starts_l = _route_call(aff_t, col0=0, n_sets=nbl, n=n_lat, offset_stride=0, name="route_lat")
    slot_c, starts_c = _route_call(aff_t, col0=tl, n_sets=nbc, n=seq,
                                   offset_stride=CAPACITY_FACTOR * seq // ne, name="route_ctx")
    slot_t = jnp.concatenate([slot_l, slot_c], axis=1)
    tab_l = starts_l[:, :, :tpg + 1]
    tab_c = jnp.concatenate([starts_c[:, :, 0].T, starts_c[-1:, :, 1].T], axis=1)[None]
    tab = jnp.concatenate([tab_l, tab_c], axis=0).reshape(-1)

    x_slots, gate_slots = _dispatch_call(tab, h2, slot_t, aff_t, ng=ng, rows=rows, tiles_per_group=tpg)
    y = _moe_call(x_slots, gate_slots, w_expert_gate[0], w_expert_up[0], w_expert_down[0])

    ys, yp = _combine_call(tab, x1, mod, slot_t.T, final_norm_g[None, :], y,
                           n_lat_tokens=tl, n_ctx_tokens=tc, tiles_per_group=tpg)

    return (yp.reshape(nbc, seq, d), ys.reshape(nbl, n_lat, d),
            ndk.reshape(nbc, 1, N_DIFF_HEADS, 2, DIFF_QK_DIM, seq).transpose(0, 1, 5, 2, 3, 4),
            ndv.reshape(nbc, 1, seq, N_DIFF_HEADS, HEAD_DIM),
            ngk.reshape(nbc, 1, seq, N_GQA_KV_HEADS, HEAD_DIM),
            ngv.reshape(nbc, 1, seq, N_GQA_KV_HEADS, HEAD_DIM))
```

```python
import functools
import math

import jax
import jax.numpy as jnp
import numpy as np
from jax import lax
from jax.experimental import pallas as pl
from jax.experimental.pallas import tpu as pltpu

F32 = jnp.float32
BF16 = jnp.bfloat16
I32 = jnp.int32

LANES = 128
HEAD_DIM = 128
N_DIFF_HEADS = 8
DIFF_QK_DIM = HEAD_DIM // 2
N_GQA_HEADS = 8
N_GQA_KV_HEADS = 2
GQA_GROUP = N_GQA_HEADS // N_GQA_KV_HEADS
N_EXPERTS = 16
CAPACITY_FACTOR = 2
GRID_W = 64
ROPE_THETA = 10000.0
NORM_EPS = 1e-6
N_MOD = 6
LAMBDA_INIT_L0 = 0.8 - 0.6 * math.exp(-0.3 * 0)

DQ_BLK = 0
DK_BLK = DQ_BLK + N_DIFF_HEADS
DV_BLK = DK_BLK + N_DIFF_HEADS
GQ_BLK = DV_BLK + N_DIFF_HEADS
GK_BLK = GQ_BLK + N_GQA_HEADS
GV_BLK = GK_BLK + N_GQA_KV_HEADS
IN_BLKS = GV_BLK + N_GQA_KV_HEADS

TOK_TILE = 256
OUTPROJ_TILE = 512
OUTPROJ_SPLIT = 2
KV_CHUNK_MAX = 2048
DISPATCH_WIN = 64
DISPATCH_TILES_PER_STEP = 4
COMBINE_TILES_PER_STEP = 2
NEG_BIG = -1e30
LOG2_E = math.log2(math.e)
MIN_EXP = -150.0
EXP_STEPS = 8
MANTISSA_STEPS = 53

VMEM_LIMIT_MB = 56
MOD_BUFFERS = 4


def _params(n_axes, vmem_mb=VMEM_LIMIT_MB):
    return pltpu.CompilerParams(dimension_semantics=("arbitrary",) * n_axes,
                                vmem_limit_bytes=vmem_mb << 20)


def _sigmoid(x):
    return 1.0 / (1.0 + jnp.exp(-x))


def _rms(x, eps=NORM_EPS):
    return x * lax.rsqrt(jnp.mean(x * x, axis=-1, keepdims=True) + eps)


def _mod_kernel(c_ref, b_ref, w_hbm, o_ref, wbuf, sem, *, tn):
    n_tiles = o_ref.shape[1] // tn
    n_buf = wbuf.shape[0]

    def tile_copy(j):
        return pltpu.make_async_copy(w_hbm.at[:, pl.ds(j * tn, tn)], wbuf.at[j % n_buf], sem.at[j % n_buf])

    for j in range(min(n_buf, n_tiles)):
        tile_copy(j).start()
    c = c_ref[...]
    a = (c * _sigmoid(c)).astype(BF16)
    for j in range(n_tiles):
        cs = slice(j * tn, (j + 1) * tn)
        tile_copy(j).wait()
        o_ref[:, cs] = (jnp.dot(a, wbuf[j % n_buf].astype(BF16), preferred_element_type=F32) + b_ref[:, cs])
        if j + n_buf < n_tiles:
            tile_copy(j + n_buf).start()


def _mod_call(cond, w_mod, b_mod):
    d, m = w_mod.shape
    tn = next(t for t in (1024, 512, 256, LANES) if m % t == 0)
    vmem = pl.BlockSpec(memory_space=pltpu.VMEM)
    return pl.pallas_call(
        functools.partial(_mod_kernel, tn=tn), name="mod",
        in_specs=[vmem, vmem, pl.BlockSpec(memory_space=pl.ANY)],
        out_specs=vmem,
        out_shape=jax.ShapeDtypeStruct((8, m), F32),
        scratch_shapes=[pltpu.VMEM((MOD_BUFFERS, d, tn), F32), pltpu.SemaphoreType.DMA((MOD_BUFFERS,))],
        compiler_params=pltpu.CompilerParams(vmem_limit_bytes=VMEM_LIMIT_MB << 20),
    )(cond, b_mod, w_mod)


def _rope(x, cos, sin_signed, half):
    lane = lax.broadcasted_iota(I32, x.shape, 1)
    first = (lane % (2 * half)) < half
    rot = jnp.where(first, pltpu.roll(x, LANES - half, 1), pltpu.roll(x, half, 1))
    return x * cos + rot * sin_signed


def _inproj_kernel(xs_ref, xp_ref, mod_ref, n1g_ref, w_ref, qn_ref, kn_ref,
                   cd_ref, sd_ref, cg_ref, sg_ref,
                   qkv_ref, ndk_ref, ndv_ref, ngk_ref, ngv_ref, *, n_lat_tiles):
    i = pl.program_id(0)
    is_ctx = i >= n_lat_tiles
    x = jnp.where(is_ctx, xp_ref[...], xs_ref[...])
    sh1 = mod_ref[0:1, :]
    sc1 = mod_ref[1:2, :]
    h = (_rms(x) * n1g_ref[...]) * (1.0 + sc1) + sh1
    hb = h.astype(BF16)
    cd, sd, cg, sg = cd_ref[...], sd_ref[...], cg_ref[...], sg_ref[...]
    qn, kn = qn_ref[...], kn_ref[...]
    cache_stores = []

    def seg(blk0, nblk):
        return jnp.dot(hb, w_ref[:, blk0 * LANES:(blk0 + nblk) * LANES], preferred_element_type=F32)

    def put(blk, val):
        qkv_ref[:, blk * LANES:(blk + 1) * LANES] = val.astype(BF16)

    p = seg(DQ_BLK, N_DIFF_HEADS)
    for j in range(N_DIFF_HEADS):
        v = p[:, j * LANES:(j + 1) * LANES]
        put(DQ_BLK + j, _rope(v, cd, sd, DIFF_QK_DIM // 4) * (DIFF_QK_DIM ** -0.5 * LOG2_E))
    p = seg(DK_BLK, N_DIFF_HEADS)
    for j in range(N_DIFF_HEADS):
        v = p[:, j * LANES:(j + 1) * LANES]
        cache_stores.append((ndk_ref, j, v))
        put(DK_BLK + j, _rope(v, cd, sd, DIFF_QK_DIM // 4))
    p = seg(DV_BLK, N_DIFF_HEADS)
    for j in range(N_DIFF_HEADS):
        v = p[:, j * LANES:(j + 1) * LANES]
        cache_stores.append((ndv_ref, j, v))
        put(DV_BLK + j, v)
    p = seg(GQ_BLK, N_GQA_HEADS)
    for j in range(N_GQA_HEADS):
        v = _rms(p[:, j * LANES:(j + 1) * LANES]) * qn
        put(GQ_BLK + j, _rope(v, cg, sg, HEAD_DIM // 4) * (HEAD_DIM ** -0.5 * LOG2_E))
    p = seg(GK_BLK, 2 * N_GQA_KV_HEADS)
    for j in range(N_GQA_KV_HEADS):
        v = _rms(p[:, j * LANES:(j + 1) * LANES]) * kn
        cache_stores.append((ngk_ref, j, v))
        put(GK_BLK + j, _rope(v, cg, sg, HEAD_DIM // 4))
    for j in range(N_GQA_KV_HEADS):
        v = p[:, (N_GQA_KV_HEADS + j) * LANES:(N_GQA_KV_HEADS + j + 1) * LANES]
        cache_stores.append((ngv_ref, j, v))
        put(GV_BLK + j, v)

    @pl.when(is_ctx)
    def _():
        for ref, j, v in cache_stores:
            if ref is ndk_ref:
                ref[j * LANES:(j + 1) * LANES, :] = v.T
            else:
                ref[:, j * LANES:(j + 1) * LANES] = v


def _inproj_call(xs, xp, mod, n1g, w_in_b, qn, kn, tables, n_lat):
    tl, d = xs.shape
    tc = xp.shape[0]
    tm = TOK_TILE
    nl, nc = tl // tm, tc // tm
    tpb = n_lat // tm
    nbl = tl // n_lat
    width = IN_BLKS * LANES

    def tab_spec():
        return pl.BlockSpec((tm, LANES), lambda i: (jnp.where(i < nl, i % tpb, tpb), 0))

    def cache_spec(w):
        return pl.BlockSpec((tm, w), lambda i: (jnp.maximum(i - nl, 0), 0))

    dw, gw = N_DIFF_HEADS * LANES, N_GQA_KV_HEADS * LANES
    return pl.pallas_call(
        functools.partial(_inproj_kernel, n_lat_tiles=nl), name="inproj",
        grid=(nl + nc,),
        in_specs=[pl.BlockSpec((tm, d), lambda i: (jnp.minimum(i, nl - 1), 0)),
                  pl.BlockSpec((tm, d), lambda i: (jnp.maximum(i - nl, 0), 0)),
                  pl.BlockSpec((None, N_MOD, d), lambda i: (jnp.where(i < nl, i // tpb, nbl), 0, 0)),
                  pl.BlockSpec((1, d), lambda i: (0, 0)),
                  pl.BlockSpec((d, width), lambda i: (0, 0), pipeline_mode=pl.Buffered(1)),
                  pl.BlockSpec((1, LANES), lambda i: (0, 0)),
                  pl.BlockSpec((1, LANES), lambda i: (0, 0)),
                  tab_spec(), tab_spec(), tab_spec(), tab_spec()],
        out_specs=[pl.BlockSpec((tm, width), lambda i: (i, 0)),
                   pl.BlockSpec((None, dw, tm), lambda i: (jnp.maximum(i - nl, 0), 0, 0)),
                   cache_spec(dw), cache_spec(gw), cache_spec(gw)],
        out_shape=[jax.ShapeDtypeStruct((tl + tc, width), BF16),
                   jax.ShapeDtypeStruct((nc, dw, tm), F32), jax.ShapeDtypeStruct((tc, dw), F32),
                   jax.ShapeDtypeStruct((tc, gw), F32), jax.ShapeDtypeStruct((tc, gw), F32)],
        compiler_params=_params(1),
    )(xs, xp, mod, n1g, w_in_b, qn, kn, *tables)


def _rope_tables(n, dim, pad_rows):
    t = np.arange(n)
    row = (t // GRID_W).astype(np.float64)
    col = (t % GRID_W).astype(np.float64)
    sec = dim // 2
    inv = ROPE_THETA ** (-np.arange(0, sec, 2, dtype=np.float64) / sec)
    ang = np.stack([row[:, None] * inv, col[:, None] * inv], axis=1)
    ang = np.concatenate([ang, ang], axis=-1).reshape(n, dim)
    sign = np.where((np.arange(dim) % sec) < sec // 2, -1.0, 1.0)
    cos, sin = np.cos(ang), np.sin(ang) * sign
    reps = LANES // dim
    cos, sin = np.tile(cos, (1, reps)), np.tile(sin, (1, reps))
    cos = np.concatenate([cos, np.ones((pad_rows, LANES))], axis=0)
    sin = np.concatenate([sin, np.zeros((pad_rows, LANES))], axis=0)
    return jnp.asarray(cos, F32), jnp.asarray(sin, F32)


def _flash(q, chunks):
    m_rows = q.shape[0]
    m = jnp.full((m_rows, 1), NEG_BIG, F32)
    acc = jnp.zeros((m_rows, HEAD_DIM), F32)
    l = jnp.zeros((m_rows, 1), F32)
    for scores, values in chunks:
        s = scores(q)
        m_new = jnp.maximum(m, jnp.max(s, axis=-1, keepdims=True))
        alpha = jnp.exp2(m - m_new)
        p = jnp.exp2(s - m_new)
        l = alpha * l + jnp.sum(p, axis=-1, keepdims=True)
        acc = alpha * acc + jnp.dot(p.astype(BF16), values(), preferred_element_type=F32)
        m = m_new
    return acc / l


def _scores_nt(k):
    return lambda q: lax.dot_general(q, k(), (((1,), (1,)), ((), ())), preferred_element_type=F32)


def _kv_chunks(k_ref, v_ref, ck_ref, cv_ref, col, cache_k_transposed):
    cs = slice(col * LANES, (col + 1) * LANES)
    chunks = []
    if ck_ref is not None:
        if cache_k_transposed:
            scores = lambda q: jnp.dot(q, ck_ref[cs, :].astype(BF16), preferred_element_type=F32)
        else:
            scores = _scores_nt(lambda: ck_ref[:, cs].astype(BF16))
        chunks.append((scores, lambda: cv_ref[:, cs].astype(BF16)))
    n = k_ref.shape[0]
    n_chunks = pl.cdiv(n, KV_CHUNK_MAX)
    assert n % n_chunks == 0
    step = n // n_chunks
    for c in range(n_chunks):
        rs = slice(c * step, (c + 1) * step)
        chunks.append((_scores_nt(lambda rs=rs: k_ref[rs, cs]), lambda rs=rs: v_ref[rs, cs]))
    return chunks


def _attn_diff_kernel(*refs, hps, has_cache):
    if has_cache:
        q_ref, k_ref, v_ref, ck_ref, cv_ref, lam_ref, g_ref, o_ref = refs
    else:
        q_ref, k_ref, v_ref, lam_ref, g_ref, o_ref = refs
        ck_ref = cv_ref = None
    lv = lam_ref[...]
    lam = (jnp.exp(jnp.sum(lv[0:1] * lv[1:2], axis=-1, keepdims=True))
           - jnp.exp(jnp.sum(lv[2:3] * lv[3:4], axis=-1, keepdims=True)) + LAMBDA_INIT_L0)
    for j in range(hps):
        q = q_ref[:, j * LANES:(j + 1) * LANES]
        lane = lax.broadcasted_iota(I32, q.shape, 1)
        zero = jnp.zeros_like(q)
        chunks = _kv_chunks(k_ref, v_ref, ck_ref, cv_ref, j, cache_k_transposed=True)
        o1 = _flash(jnp.where(lane < DIFF_QK_DIM, q, zero), chunks)
        o2 = _flash(jnp.where(lane >= DIFF_QK_DIM, q, zero), chunks)
        o = o1 - lam * o2
        o = (_rms(o) * g_ref[...]) * (1.0 - LAMBDA_INIT_L0)
        o_ref[:, j * LANES:(j + 1) * LANES] = o.astype(o_ref.dtype)


def _attn_diff_call(qkv, ck, cv, lamv, subg, *, row0, nb, n, tq, hps):
    w = hps * LANES
    hb_n = N_DIFF_HEADS // hps
    qpb = n // tq
    in_specs = [pl.BlockSpec((tq, w), lambda b, h, qi: (row0 // tq + b * qpb + qi, DQ_BLK // hps + h)),
                pl.BlockSpec((n, w), lambda b, h, qi: (row0 // n + b, DK_BLK // hps + h)),
                pl.BlockSpec((n, w), lambda b, h, qi: (row0 // n + b, DV_BLK // hps + h))]
    args = [qkv, qkv, qkv]
    if ck is not None:
        past = ck.shape[1]
        in_specs += [pl.BlockSpec((w, past), lambda b, h, qi: (b * hb_n + h, 0)),
                     pl.BlockSpec((past, w), lambda b, h, qi: (b, h))]
        args += [ck, cv]
    in_specs += [pl.BlockSpec(lamv.shape, lambda b, h, qi: (0, 0)),
                 pl.BlockSpec((1, LANES), lambda b, h, qi: (0, 0))]
    args += [lamv, subg]
    return pl.pallas_call(
        functools.partial(_attn_diff_kernel, hps=hps, has_cache=ck is not None),
        name="attn_diff_lat" if ck is not None else "attn_diff_ctx",
        grid=(nb, hb_n, qpb),
        in_specs=in_specs,
        out_specs=pl.BlockSpec((tq, w), lambda b, h, qi: (b * qpb + qi, h)),
        out_shape=jax.ShapeDtypeStruct((nb * n, N_DIFF_HEADS * LANES), BF16),
        compiler_params=_params(3),
    )(*args)


def _attn_gqa_kernel(*refs, kvps, has_cache):
    if has_cache:
        q_ref, k_ref, v_ref, ck_ref, cv_ref, o_ref = refs
    else:
        q_ref, k_ref, v_ref, o_ref = refs
        ck_ref = cv_ref = None
    tq = q_ref.shape[0]
    for j in range(kvps):
        c0 = j * GQA_GROUP
        q = jnp.concatenate([q_ref[:, (c0 + r) * LANES:(c0 + r + 1) * LANES] for r in range(GQA_GROUP)],
                            axis=0)
        chunks = _kv_chunks(k_ref, v_ref, ck_ref, cv_ref, j, cache_k_transposed=False)
        o = _flash(q, chunks).astype(o_ref.dtype)
        for r in range(GQA_GROUP):
            o_ref[:, (c0 + r) * LANES:(c0 + r + 1) * LANES] = o[r * tq:(r + 1) * tq]


def _attn_gqa_call(qkv, ck, cv, *, row0, nb, n, tq, kvps):
    w = kvps * GQA_GROUP * LANES
    kw = kvps * LANES
    qpb = n // tq
    in_specs = [pl.BlockSpec((tq, w), lambda b, g, qi: (row0 // tq + b * qpb + qi, GQ_BLK * LANES // w + g)),
                pl.BlockSpec((n, kw), lambda b, g, qi: (row0 // n + b, GK_BLK // kvps + g)),
                pl.BlockSpec((n, kw), lambda b, g, qi: (row0 // n + b, GV_BLK // kvps + g))]
    args = [qkv, qkv, qkv]
    if ck is not None:
        past = ck.shape[0] // nb
        in_specs += [pl.BlockSpec((past, kw), lambda b, g, qi: (b, g)),
                     pl.BlockSpec((past, kw), lambda b, g, qi: (b, g))]
        args += [ck, cv]
    return pl.pallas_call(
        functools.partial(_attn_gqa_kernel, kvps=kvps, has_cache=ck is not None),
        name="attn_gqa_lat" if ck is not None else "attn_gqa_ctx",
        grid=(nb, N_GQA_KV_HEADS // kvps, qpb),
        in_specs=in_specs,
        out_specs=pl.BlockSpec((tq, w), lambda b, g, qi: (b * qpb + qi, g)),
        out_shape=jax.ShapeDtypeStruct((nb * n, N_GQA_HEADS * LANES), BF16),
        compiler_params=_params(3),
    )(*args)


def _outproj_kernel(odl_ref, ogl_ref, odc_ref, ogc_ref, xs_ref, xp_ref, mod_ref, n2g_ref, w_ref, wr3_ref,
                    x1_ref, h2_ref, aff_ref, *, n_lat_tiles, ne):
    i = pl.program_id(0)
    is_ctx = i >= n_lat_tiles
    dw = odl_ref.shape[1]
    g1 = mod_ref[2:3, :]
    sh2 = mod_ref[3:4, :]
    sc2 = mod_ref[4:5, :]
    w3 = wr3_ref[...]
    tm = x1_ref.shape[0]
    half = tm // OUTPROJ_SPLIT
    projected = []
    for r in range(OUTPROJ_SPLIT):
        rs = slice(r * half, (r + 1) * half)
        od = jnp.where(is_ctx, odc_ref[rs, :], odl_ref[rs, :])
        og = jnp.where(is_ctx, ogc_ref[rs, :], ogl_ref[rs, :])
        projected.append(jnp.dot(od, w_ref[0:dw, :], preferred_element_type=F32)
                         + jnp.dot(og, w_ref[dw:, :], preferred_element_type=F32))
    for r in range(OUTPROJ_SPLIT):
        rs = slice(r * half, (r + 1) * half)
        x = jnp.where(is_ctx, xp_ref[rs, :], xs_ref[rs, :])
        x1 = x + g1 * projected[r]
        x1_ref[rs, :] = x1
        h2 = (_rms(x1) * n2g_ref[...]) * (1.0 + sc2) + sh2
        h_hi = h2.astype(BF16)
        h2_ref[rs, :] = h_hi
        h_mid = (h2 - h_hi.astype(F32)).astype(BF16)
        t = (jnp.dot(h_mid, w3, preferred_element_type=F32) + jnp.dot(h_hi, w3, preferred_element_type=F32))
        t = (pltpu.roll(t, LANES - 2 * ne, 1) + pltpu.roll(t, LANES - ne, 1)) + t
        lane = lax.broadcasted_iota(I32, t.shape, 1)
        logits = jnp.where(lane < ne, t, NEG_BIG)
        z = jnp.exp(logits - jnp.max(logits, axis=-1, keepdims=True))
        aff_ref[rs, :] = z / jnp.sum(z, axis=-1, keepdims=True)


def _split3_bf16(w):
    hi = w.astype(BF16)
    rem = w - hi.astype(F32)
    mid = rem.astype(BF16)
    lo = (rem - mid.astype(F32)).astype(BF16)
    pad = jnp.zeros((w.shape[0], LANES - 3 * w.shape[1]), BF16)
    return jnp.concatenate([hi, mid, lo, pad], axis=1)


def _outproj_call(odl, ogl, odc, ogc, xs, xp, mod, n2g, w_out_b, wr3, ne, n_lat):
    tl, d = xs.shape
    tc = xp.shape[0]
    tm = OUTPROJ_TILE
    nl, nc = tl // tm, tc // tm
    tpb = n_lat // tm
    nbl = tl // n_lat
    aw = odl.shape[1]

    def lat_spec(w):
        return pl.BlockSpec((tm, w), lambda i: (jnp.minimum(i, nl - 1), 0))

    def ctx_spec(w):
        return pl.BlockSpec((tm, w), lambda i: (jnp.maximum(i - nl, 0), 0))

    return pl.pallas_call(
        functools.partial(_outproj_kernel, n_lat_tiles=nl, ne=ne), name="outproj",
        grid=(nl + nc,),
        in_specs=[lat_spec(aw), lat_spec(aw), ctx_spec(aw), ctx_spec(aw), lat_spec(d), ctx_spec(d),
                  pl.BlockSpec((None, N_MOD, d), lambda i: (jnp.where(i < nl, i // tpb, nbl), 0, 0)),
                  pl.BlockSpec((1, d), lambda i: (0, 0)),
                  pl.BlockSpec(w_out_b.shape, lambda i: (0, 0), pipeline_mode=pl.Buffered(1)),
                  pl.BlockSpec((d, LANES), lambda i: (0, 0))],
        out_specs=[pl.BlockSpec((tm, d), lambda i: (i, 0)),
                   pl.BlockSpec((tm, d), lambda i: (i, 0)),
                   pl.BlockSpec((tm, LANES), lambda i: (i, 0))],
        out_shape=[jax.ShapeDtypeStruct((tl + tc, d), F32),
                   jax.ShapeDtypeStruct((tl + tc, d), BF16),
                   jax.ShapeDtypeStruct((tl + tc, LANES), F32)],
        compiler_params=_params(1),
    )(odl, ogl, odc, ogc, xs, xp, mod, n2g, w_out_b, wr3)


def _route_kernel(aff_ref, slot_ref, starts_ref, *, cap, n, offset_stride):
    ne = aff_ref.shape[0]
    n_sets = aff_ref.shape[1] // n
    tt = TOK_TILE
    aff = jnp.concatenate([aff_ref[:, s * n:(s + 1) * n] for s in range(n_sets)], axis=0)
    capf = float(cap)
    ne_all = n_sets * ne

    def count(mask):
        return jnp.sum(mask.astype(F32), axis=1, keepdims=True)

    def threshold(e):
        return jnp.where(e <= MIN_EXP, 0.0, jnp.exp2(e))

    def exp_step(_, carry):
        e_lo, e_hi = carry
        e_mid = jnp.floor(0.5 * (e_lo + e_hi))
        ok = count(aff >= threshold(e_mid)) >= capf
        return jnp.where(ok, e_mid, e_lo), jnp.where(ok, e_hi, e_mid)

    e_lo, e_hi = lax.fori_loop(0, EXP_STEPS, exp_step,
                               (jnp.full((ne_all, 1), MIN_EXP, F32), jnp.full((ne_all, 1), 1.0, F32)))

    def bisect(_, carry):
        lo, hi = carry
        mid = 0.5 * (lo + hi)
        ok = count(aff >= mid) >= capf
        return jnp.where(ok, mid, lo), jnp.where(ok, hi, mid)

    thr, _ = lax.fori_loop(0, MANTISSA_STEPS, bisect, (threshold(e_lo), threshold(e_hi)))
    gt = aff > thr
    tie = aff == thr
    need = capf - count(gt)

    ri = lax.broadcasted_iota(I32, (tt, tt), 0)
    ci = lax.broadcasted_iota(I32, (tt, tt), 1)
    upper = (ri < ci).astype(BF16)
    set_id = lax.broadcasted_iota(I32, (ne_all, 1), 0) // ne
    tie_pref = jnp.zeros((ne_all, 1), F32)
    sel_pref = (set_id * offset_stride).astype(F32)
    lane = lax.broadcasted_iota(I32, (ne_all, LANES), 1)
    starts = jnp.zeros((ne_all, LANES), F32)
    for t in range(n // tt):
        cs = slice(t * tt, (t + 1) * tt)
        tie_t = tie[:, cs]
        tie_rank = tie_pref + jnp.dot(tie_t.astype(BF16), upper, preferred_element_type=F32)
        sel_t = gt[:, cs] | (tie_t & (tie_rank < need))
        slot_t = sel_pref + jnp.dot(sel_t.astype(BF16), upper, preferred_element_type=F32)
        slot_t = jnp.where(sel_t, slot_t, -1.0).astype(I32)
        for s in range(n_sets):
            slot_ref[:, s * n + t * tt:s * n + (t + 1) * tt] = slot_t[s * ne:(s + 1) * ne]
        starts = jnp.where(lane == t, sel_pref, starts)
        tie_pref = tie_pref + count(tie_t)
        sel_pref = sel_pref + count(sel_t)
    starts = jnp.where(lane == n // tt, sel_pref, starts).astype(I32)
    for s in range(n_sets):
        starts_ref[s] = starts[s * ne:(s + 1) * ne]


def _route_call(aff_t, *, col0, n_sets, n, offset_stride, name):
    ne = aff_t.shape[0]
    cap = CAPACITY_FACTOR * n // N_EXPERTS
    w = n_sets * n
    return pl.pallas_call(
        functools.partial(_route_kernel, cap=cap, n=n, offset_stride=offset_stride), name=name,
        grid=(1,),
        in_specs=[pl.BlockSpec((ne, w), lambda s: (0, col0 // w))],
        out_specs=[pl.BlockSpec((ne, w), lambda s: (0, 0)),
                   pl.BlockSpec((n_sets, ne, LANES), lambda s: (0, 0, 0))],
        out_shape=[jax.ShapeDtypeStruct((ne, w), I32),
                   jax.ShapeDtypeStruct((n_sets, ne, LANES), I32)],
        compiler_params=_params(1),
    )(aff_t)


def _window_start(first_slot, rows):
    return pl.multiple_of(jnp.minimum((first_slot // 16) * 16, rows - DISPATCH_WIN), 16)


def _dispatch_kernel(tab_ref, h2_ref, slot_ref, gate_ref, x_ref, gc_ref, *, tiles_per_group, ne):
    g, eb, tb = pl.program_id(0), pl.program_id(1), pl.program_id(2)
    epb, rows = x_ref.shape[0], x_ref.shape[1]
    tt = TOK_TILE
    win = DISPATCH_WIN

    @pl.when(tb == 0)
    def _():
        x_ref[...] = jnp.zeros_like(x_ref)
        gc_ref[...] = jnp.zeros_like(gc_ref)

    sub = lax.broadcasted_iota(I32, (win, tt), 0)

    for s in range(h2_ref.shape[0] // tt):
        t = tb * (h2_ref.shape[0] // tt) + s
        cs = slice(s * tt, (s + 1) * tt)
        h_t = h2_ref[cs, :]

        def scatter(i, start, hit, cs=cs):
            gwin = jnp.sum(jnp.where(hit, gate_ref[i:i + 1, cs], 0.0), axis=1, keepdims=True)
            gc_ref[i, pl.ds(start, win), :] += jnp.broadcast_to(gwin, (win, LANES))

        starts, hits = [], []
        for i in range(epb):
            base = (g * ne + eb * epb + i) * (tiles_per_group + 1) + t
            start = _window_start(tab_ref[base], rows)
            starts.append(start)
            hits.append(slot_ref[i:i + 1, cs] - start == sub)
        onehot = jnp.concatenate([h.astype(BF16) for h in hits], axis=0)
        moved = jnp.dot(onehot, h_t, preferred_element_type=F32).astype(BF16)
        for i in range(epb):
            x_ref[i, pl.ds(starts[i], win), :] += moved[i * win:(i + 1) * win]
            scatter(i, starts[i], hits[i])

        spans = [tab_ref[(g * ne + eb * epb + i) * (tiles_per_group + 1) + t + 1] - starts[i]
                 for i in range(epb)]

        @pl.when(functools.reduce(jnp.logical_or, [span > win for span in spans]))
        def _(starts=starts, spans=spans, cs=cs, h_t=h_t, scatter=scatter):
            for i in range(epb):
                def extra(w, c, i=i, first=starts[i]):
                    lo = first + w * win
                    start = _window_start(lo, rows)
                    srow = slot_ref[i:i + 1, cs]
                    hit = (srow - start == sub) & (srow >= lo)
                    x_ref[i, pl.ds(start, win), :] += jnp.dot(hit.astype(BF16), h_t,
                                                              preferred_element_type=F32).astype(BF16)
                    scatter(i, start, hit)
                    return c

                lax.fori_loop(1, (spans[i] + win - 1) // win, extra, 0)


def _dispatch_call(tab, h2, slot_t, gate_t, *, ng, rows, tiles_per_group):
    t_all, d = h2.shape
    ne = slot_t.shape[0]
    epb = 8
    tps = DISPATCH_TILES_PER_STEP
    steps = tiles_per_group // tps
    tw = tps * TOK_TILE

    def tok_spec(w):
        return pl.BlockSpec((w, tw), lambda g, eb, t, tab: (eb, g * steps + t))

    grid_spec = pltpu.PrefetchScalarGridSpec(
        num_scalar_prefetch=1,
        grid=(ng, ne // epb, steps),
        in_specs=[pl.BlockSpec((tw, d), lambda g, eb, t, tab: (g * steps + t, 0)),
                  tok_spec(epb), tok_spec(epb)],
        out_specs=[pl.BlockSpec((None, epb, rows, d), lambda g, eb, t, tab: (g, eb, 0, 0)),
                   pl.BlockSpec((None, epb, rows, LANES), lambda g, eb, t, tab: (g, eb, 0, 0))])
    return pl.pallas_call(
        functools.partial(_dispatch_kernel, tiles_per_group=tiles_per_group, ne=ne), name="dispatch",
        grid_spec=grid_spec,
        out_shape=[jax.ShapeDtypeStruct((ng, ne, rows, d), BF16),
                   jax.ShapeDtypeStruct((ng, ne, rows, LANES), F32)],
        compiler_params=_params(3),
    )(tab, h2, slot_t, gate_t)


def _moe_kernel(x_ref, gc_ref, wg_ref, wu_ref, wd_ref, y_ref, yacc_ref, *, nf):
    f = pl.program_id(1)
    ng = x_ref.shape[0]
    tf = wg_ref.shape[1]

    def partial_outputs():
        w_gu = jnp.concatenate([wg_ref[...].astype(BF16), wu_ref[...].astype(BF16)], axis=1)
        wd = wd_ref[...].astype(BF16)
        for g in range(ng):
            gu = jnp.dot(x_ref[g], w_gu, preferred_element_type=F32)
            gate, up = gu[:, :tf], gu[:, tf:]
            hid = (gate * _sigmoid(gate) * up).astype(BF16)
            yield g, jnp.dot(hid, wd, preferred_element_type=F32)

    def finish(g, total):
        y_ref[g] = (total * gc_ref[g][:, 0:1]).astype(y_ref.dtype)

    if nf == 1:
        for g, part in partial_outputs():
            finish(g, part)
        return

    @pl.when(f == 0)
    def _():
        for g, part in partial_outputs():
            yacc_ref[g] = part

    @pl.when(jnp.logical_and(f > 0, f < nf - 1))
    def _():
        for g, part in partial_outputs():
            yacc_ref[g] += part

    @pl.when(f == nf - 1)
    def _():
        for g, part in partial_outputs():
            finish(g, yacc_ref[g] + part)


def _moe_call(x, gc, w_gate, w_up, w_down):
    ng, ne, rows, d = x.shape
    fdim = w_gate.shape[2]
    tf = min(fdim, 256)
    return pl.pallas_call(
        functools.partial(_moe_kernel, nf=fdim // tf), name="moe",
        grid=(ne, fdim // tf),
        in_specs=[pl.BlockSpec((ng, None, rows, d), lambda e, f: (0, e, 0, 0)),
                  pl.BlockSpec((ng, None, rows, LANES), lambda e, f: (0, e, 0, 0)),
                  pl.BlockSpec((None, d, tf), lambda e, f: (e, 0, f)),
                  pl.BlockSpec((None, d, tf), lambda e, f: (e, 0, f)),
                  pl.BlockSpec((None, tf, d), lambda e, f: (e, f, 0))],
        out_specs=pl.BlockSpec((ng, None, rows, d), lambda e, f: (0, e, 0, 0)),
        out_shape=jax.ShapeDtypeStruct((ng, ne, rows, d), BF16),
        scratch_shapes=[pltpu.VMEM((ng, rows, d), F32)],
        compiler_params=_params(2),
    )(x, gc, w_gate, w_up, w_down)


def _combine_kernel(tab_ref, x1_ref, mod_ref, slot_ref, fng_ref, y_hbm, ys_ref, yp_ref,
                    ybuf, obuf, acc_ref, sem, osem, *, tiles_per_group, n_lat_tiles, rows):
    i = pl.program_id(0)
    n_steps = pl.num_programs(0)
    tps, ne = sem.shape[1], sem.shape[2]
    win = DISPATCH_WIN
    tt = TOK_TILE
    per = LANES // win

    def first_slot(tile, e):
        return tab_ref[((tile // tiles_per_group) * ne + e) * (tiles_per_group + 1) + tile % tiles_per_group]

    def window_copies(step, buf):
        copies = []
        for s in range(tps):
            tile = step * tps + s
            g = tile // tiles_per_group
            copies += [pltpu.make_async_copy(
                y_hbm.at[g, e, pl.ds(_window_start(first_slot(tile, e), rows), win), :],
                ybuf.at[buf, s, pl.ds(e * win, win), :], sem.at[buf, s, e]) for e in range(ne)]
        return copies

    buf = i % 2

    @pl.when(i == 0)
    def _():
        for cp in window_copies(i, buf):
            cp.start()

    @pl.when(i + 1 < n_steps)
    def _():
        for cp in window_copies(i + 1, 1 - buf):
            cp.start()

    lane = lax.broadcasted_iota(I32, (tt, LANES), 1)
    tile_starts, onehots = [], []
    for s in range(tps):
        rs = slice(s * tt, (s + 1) * tt)
        starts = [_window_start(first_slot(i * tps + s, e), rows) for e in range(ne)]
        blocks = []
        for k in range(ne // per):
            slot = slot_ref[rs, k * per:k * per + 1]
            start = starts[k * per]
            for j in range(1, per):
                pick = lane >= j * win
                slot = jnp.where(pick, slot_ref[rs, k * per + j:k * per + j + 1], slot)
                start = jnp.where(pick, starts[k * per + j], start)
            blocks.append((slot - start == lane % win).astype(BF16))
        tile_starts.append(starts)
        onehots.append(jnp.concatenate(blocks, axis=1))

    for cp in window_copies(i, buf):
        cp.wait()
    for s in range(tps):
        acc_ref[s * tt:(s + 1) * tt, :] = jnp.dot(onehots[s], ybuf[buf, s], preferred_element_type=F32)

    for s in range(tps):
        rs = slice(s * tt, (s + 1) * tt)
        tile = i * tps + s
        g = tile // tiles_per_group
        starts = tile_starts[s]
        spans = [tab_ref[(g * ne + e) * (tiles_per_group + 1) + tile % tiles_per_group + 1] - starts[e]
                 for e in range(ne)]

        @pl.when(functools.reduce(jnp.logical_or, [span > win for span in spans]))
        def _(rs=rs, g=g, starts=starts, spans=spans):
            lane_w = lax.broadcasted_iota(I32, (tt, win), 1)
            for e in range(ne):
                def extra(w, c, e=e, first=starts[e]):
                    lo = first + w * win
                    start = _window_start(lo, rows)
                    cp = pltpu.make_async_copy(y_hbm.at[g, e, pl.ds(start, win), :], obuf, osem.at[0])
                    cp.start()
                    cp.wait()
                    slot = slot_ref[rs, e:e + 1]
                    hit = ((slot - start == lane_w) & (slot >= lo)).astype(BF16)
                    acc_ref[rs, :] += jnp.dot(hit, obuf[...], preferred_element_type=F32)
                    return c

                lax.fori_loop(1, (spans[e] + win - 1) // win, extra, 0)

    g2 = mod_ref[5:6, :]
    x2 = x1_ref[...] + g2 * acc_ref[...]
    y = _rms(x2) * fng_ref[...]

    @pl.when(i < n_lat_tiles)
    def _():
        ys_ref[...] = y

    @pl.when(i >= n_lat_tiles)
    def _():
        yp_ref[...] = y


def _combine_call(tab, x1, mod, slot_n, fng, y, *, n_lat_tokens, n_ctx_tokens, tiles_per_group):
    t_all, d = x1.shape
    tps = COMBINE_TILES_PER_STEP
    tm = tps * TOK_TILE
    nl = n_lat_tokens // tm
    ne = slot_n.shape[1]
    rows = y.shape[2]
    assert LANES % DISPATCH_WIN == 0 and ne % (LANES // DISPATCH_WIN) == 0 and tiles_per_group % tps == 0
    grid_spec = pltpu.PrefetchScalarGridSpec(
        num_scalar_prefetch=1,
        grid=(t_all // tm,),
        in_specs=[pl.BlockSpec((tm, d), lambda i, tab: (i, 0)),
                  pl.BlockSpec((None, N_MOD, d), lambda i, tab: (i * tps // tiles_per_group, 0, 0)),
                  pl.BlockSpec((tm, ne), lambda i, tab: (i, 0)),
                  pl.BlockSpec((1, d), lambda i, tab: (0, 0)),
                  pl.BlockSpec(memory_space=pl.ANY)],
        out_specs=[pl.BlockSpec((tm, d), lambda i, tab: (jnp.minimum(i, nl - 1), 0)),
                   pl.BlockSpec((tm, d), lambda i, tab: (jnp.maximum(i - nl, 0), 0))],
        scratch_shapes=[pltpu.VMEM((2, tps, ne * DISPATCH_WIN, d), BF16),
                        pltpu.VMEM((DISPATCH_WIN, d), BF16),
                        pltpu.VMEM((tm, d), F32),
                        pltpu.SemaphoreType.DMA((2, tps, ne)),
                        pltpu.SemaphoreType.DMA((1,))])
    return pl.pallas_call(
        functools.partial(_combine_kernel, tiles_per_group=tiles_per_group, n_lat_tiles=nl, rows=rows),
        name="combine",
        grid_spec=grid_spec,
        out_shape=[jax.ShapeDtypeStruct((n_lat_tokens, d), F32),
                   jax.ShapeDtypeStruct((n_ctx_tokens, d), F32)],
        compiler_params=_params(1),
    )(tab, x1, mod, slot_n, fng, y)


def kernel(x_prompt, x_sample, c, cache_diff_k, cache_diff_v, cache_gqa_k, cache_gqa_v, c_ctx, norm1_g, norm2_g, w_mod, b_mod, w_in, w_out, diff_lambda_q1, diff_lambda_k1, diff_lambda_q2, diff_lambda_k2, diff_subln_g, gqa_q_norm_g, gqa_k_norm_g, w_router, w_expert_gate, w_expert_up, w_expert_down, final_norm_g):
    nbc, seq, d = x_prompt.shape
    nbl, n_lat, _ = x_sample.shape
    depth, past = cache_diff_k.shape[1], cache_diff_k.shape[2]
    tc, tl = nbc * seq, nbl * n_lat
    assert depth == 1, "single trunk layer"
    assert tc == n_lat, "the context tokens must fill exactly one routing group"
    assert seq == TOK_TILE and nbl + 1 <= 8
    ne = N_EXPERTS
    tpg = n_lat // TOK_TILE
    ng = nbl + 1
    rows = CAPACITY_FACTOR * n_lat // ne

    xs = x_sample.reshape(tl, d)
    xp = x_prompt.reshape(tc, d)

    cond = jnp.concatenate([c, c_ctx[None, :], jnp.zeros((8 - ng, d), F32)], axis=0)
    mod = _mod_call(cond, w_mod[0], b_mod).reshape(8, N_MOD, d)

    tables = (_rope_tables(n_lat, DIFF_QK_DIM, TOK_TILE) + _rope_tables(n_lat, HEAD_DIM, TOK_TILE))
    qkv, ndk, ndv, ngk, ngv = _inproj_call(
        xs, xp, mod, norm1_g, w_in[0].astype(BF16), gqa_q_norm_g, gqa_k_norm_g, tables, n_lat)

    lamv = jnp.concatenate([diff_lambda_q1, diff_lambda_k1, diff_lambda_q2, diff_lambda_k2], axis=0)
    ckd = jnp.transpose(cache_diff_k, (0, 1, 3, 4, 5, 2)).reshape(nbl * N_DIFF_HEADS * LANES, past)
    cvd = cache_diff_v.reshape(nbl * past, N_DIFF_HEADS * LANES)
    ckg = cache_gqa_k.reshape(nbl * past, N_GQA_KV_HEADS * LANES)
    cvg = cache_gqa_v.reshape(nbl * past, N_GQA_KV_HEADS * LANES)
    odl = _attn_diff_call(qkv, ckd, cvd, lamv, diff_subln_g, row0=0, nb=nbl, n=n_lat, tq=1024, hps=1)
    ogl = _attn_gqa_call(qkv, ckg, cvg, row0=0, nb=nbl, n=n_lat, tq=256, kvps=N_GQA_KV_HEADS)
    odc = _attn_diff_call(qkv, None, None, lamv, diff_subln_g, row0=tl, nb=nbc, n=seq, tq=seq,
                          hps=N_DIFF_HEADS)
    ogc = _attn_gqa_call(qkv, None, None, row0=tl, nb=nbc, n=seq, tq=seq, kvps=N_GQA_KV_HEADS)

    x1, h2, aff = _outproj_call(odl, ogl, odc, ogc, xs, xp, mod, norm2_g, w_out[0].astype(BF16),
                                _split3_bf16(w_router[0]), ne, n_lat)
    aff_t = aff[:, :ne].T

    slot_l, starts_l = _route_call(aff_t, col0=0, n_sets=nbl, n=n_lat, offset_stride=0, name="route_lat")
    slot_c, starts_c = _route_call(aff_t, col0=tl, n_sets=nbc, n=seq,
                                   offset_stride=CAPACITY_FACTOR * seq // ne, name="route_ctx")
    slot_t = jnp.concatenate([slot_l, slot_c], axis=1)
    tab_l = starts_l[:, :, :tpg + 1]
    tab_c = jnp.concatenate([starts_c[:, :, 0].T, starts_c[-1:, :, 1].T], axis=1)[None]
    tab = jnp.concatenate([tab_l, tab_c], axis=0).reshape(-1)

    x_slots, gate_slots = _dispatch_call(tab, h2, slot_t, aff_t, ng=ng, rows=rows, tiles_per_group=tpg)
    y = _moe_call(x_slots, gate_slots, w_expert_gate[0], w_expert_up[0], w_expert_down[0])

    ys, yp = _combine_call(tab, x1, mod, slot_t.T, final_norm_g[None, :], y,
                           n_lat_tokens=tl, n_ctx_tokens=tc, tiles_per_group=tpg)

    return (yp.reshape(nbc, seq, d), ys.reshape(nbl, n_lat, d),
            ndk.reshape(nbc, 1, N_DIFF_HEADS, 2, DIFF_QK_DIM, seq).transpose(0, 1, 5, 2, 3, 4),
            ndv.reshape(nbc, 1, seq, N_DIFF_HEADS, HEAD_DIM),
            ngk.reshape(nbc, 1, seq, N_GQA_KV_HEADS, HEAD_DIM),
            ngv.reshape(nbc, 1, seq, N_GQA_KV_HEADS, HEAD_DIM))
```
